```python
import jax
import jax.numpy as jnp
from jax import lax
import numpy as np

D_MODEL = 2048
BATCH = 8
SEQ = 4096
DEPTH = 2

GRID_W = 64
CTX_LEN = 256
N_BRANCH = 4
MIX_W = D_MODEL // 4
NA_HEADS = 8
NA_HEAD_DIM = MIX_W // NA_HEADS
NA_WIN_H = 8
NA_WIN_W = 16
NA_KBLK_W = 2 * NA_WIN_W
NA_NBLK = GRID_W // NA_WIN_W
NA_SCALE = NA_HEAD_DIM ** -0.5
CONV_WIDTH = 3
FOURIER_GROUPS = 4
POOL_WINDOWS = (2, 4, 8, 16)
POOL_GC = MIX_W // len(POOL_WINDOWS)
N_EXPERTS = 16
EXPERT_FF = D_MODEL
EC_CAPACITY = 2
N_MOD = 6
EPS = 1e-6
NEG_INF = -1e30

IN_SPLITS = tuple(MIX_W * i for i in range(1, 9))
IN_COLS = 8 * MIX_W + N_BRANCH * D_MODEL
K_OFF = 4 * MIX_W
V_END = 6 * MIX_W

kernel_name = 'hybrid_gated_mixers_ec_moe_diffusion'


def rmsnorm(x, w):
    xf = x.astype(jnp.float32)
    y = xf * lax.rsqrt(jnp.mean(xf * xf, axis=-1, keepdims=True) + EPS)
    return (y * w.astype(jnp.float32)).astype(x.dtype)


def modulate(h, shift, scale):
    return h * (1 + scale) + shift


def short_conv(u, w):
    return lax.conv_general_dilated(
        u, w[:, None, :].astype(u.dtype), window_strides=(1,),
        padding=((CONV_WIDTH // 2, CONV_WIDTH // 2),),
        dimension_numbers=('NWC', 'WIO', 'NWC'), feature_group_count=u.shape[-1])


def fourier_mix(u):
    b, n, _ = u.shape
    ug = u.astype(jnp.float32).reshape(b, n, FOURIER_GROUPS, MIX_W // FOURIER_GROUPS)
    y = jnp.fft.fftn(ug, axes=(1, 3), norm='ortho').real
    return y.reshape(b, n, MIX_W).astype(u.dtype)


def multiscale_pool(u, w_grp, scale):
    b, n, _ = u.shape
    ng = len(POOL_WINDOWS)
    uf = u.astype(jnp.float32).reshape(b, n, ng, POOL_GC)
    csum = jnp.concatenate([jnp.zeros((b, 1, ng, POOL_GC), jnp.float32), lax.cumsum(uf, axis=1)], axis=1)
    t = np.arange(n)[:, None]
    win = np.asarray(POOL_WINDOWS)[None, :]
    lo = np.clip(t - win // 2, 0, n - 1)
    hi = np.clip(t + win - win // 2 - 1, 0, n - 1)
    gi = np.arange(ng)[None, :]
    count = (hi - lo + 1).astype(np.float32)[None, :, :, None]
    pooled = (csum[:, hi + 1, gi] - csum[:, lo, gi]) / count - uf
    y = jnp.einsum('bngc,gcd->bngd', pooled.astype(u.dtype), w_grp)
    return y.reshape(b, n, MIX_W) * scale


def na_tables(rows):
    kh = min(NA_WIN_H, rows)
    r = np.arange(rows)
    row_start = np.clip(r - kh // 2, 0, rows - kh)
    key_rows = row_start[:, None] + np.arange(kh)[None, :]
    dr = key_rows - r[:, None] + (NA_WIN_H - 1)
    qcol = np.arange(GRID_W).reshape(NA_NBLK, NA_WIN_W)
    kb_start = np.clip(np.arange(NA_NBLK) * NA_WIN_W - NA_WIN_W // 2, 0, GRID_W - NA_KBLK_W)
    key_cols = kb_start[:, None] + np.arange(NA_KBLK_W)[None, :]
    col_start = np.clip(qcol - NA_WIN_W // 2, 0, GRID_W - NA_WIN_W)
    kc = key_cols[:, None, :]
    cs = col_start[:, :, None]
    col_ok = (kc >= cs) & (kc < cs + NA_WIN_W)
    dc = np.clip(kc - qcol[:, :, None] + (NA_WIN_W - 1), 0, 2 * NA_WIN_W - 2)
    return kh, key_rows, key_cols, col_ok, dr, dc


def neighbourhood_attention(q, k, v, k_ctx, v_ctx, rpb):
    b, n, nh, dh = q.shape
    rows = n // GRID_W
    kh, key_rows, key_cols, col_ok, dr, dc = na_tables(rows)
    nkeys = kh * NA_KBLK_W
    qb = q.reshape(b, rows, NA_NBLK, NA_WIN_W, nh, dh)
    kg = k.reshape(b, rows, GRID_W, nh, dh)
    vg = v.reshape(b, rows, GRID_W, nh, dh)
    ri = key_rows[:, None, :, None]
    ci = key_cols[None, :, None, :]
    kb = kg[:, ri, ci]
    vb = vg[:, ri, ci]
    s_loc = jnp.einsum('brnqhd,brnijhd->bhrnqij', qb, kb, preferred_element_type=jnp.float32) * NA_SCALE
    bias = rpb[:, dr[:, None, None, :, None], dc[None, :, :, None, :]].astype(jnp.float32)
    s_loc = jnp.where(col_ok[:, :, None, :], s_loc + bias, NEG_INF)
    s_ctx = jnp.einsum('brnqhd,blhd->bhrnql', qb, k_ctx, preferred_element_type=jnp.float32) * NA_SCALE
    s = jnp.concatenate([s_loc.reshape(b, nh, rows, NA_NBLK, NA_WIN_W, nkeys), s_ctx], axis=-1)
    p = jax.nn.softmax(s, axis=-1).astype(v.dtype)
    p_loc = p[..., :nkeys].reshape(b, nh, rows, NA_NBLK, NA_WIN_W, kh, NA_KBLK_W)
    o = (jnp.einsum('bhrnqij,brnijhd->brnqhd', p_loc, vb)
         + jnp.einsum('bhrnql,blhd->brnqhd', p[..., nkeys:], v_ctx))
    return o.reshape(b, n, nh * dh)


def context_attention(q, k, v):
    b, l, nh, dh = q.shape
    s = jnp.einsum('blhd,bmhd->bhlm', q, k, preferred_element_type=jnp.float32) * NA_SCALE
    p = jax.nn.softmax(s, axis=-1).astype(v.dtype)
    return jnp.einsum('bhlm,bmhd->blhd', p, v).reshape(b, l, nh * dh)


def kv_heads(zk, zv, k_norm_w):
    b, n, _ = zk.shape
    k = rmsnorm(zk.reshape(b, n, NA_HEADS, NA_HEAD_DIM), k_norm_w)
    v = zv.reshape(b, n, NA_HEADS, NA_HEAD_DIM)
    return k, v


def token_mixer(z, kv_ctx, is_latent, conv_w, q_norm_w, k_norm_w, rpb, pool_w, pool_scale, w_branch, w_out):
    b, n, _ = z.shape
    xa, gb, gc, zq, zk, zv, zf, zp, zg = jnp.split(z, IN_SPLITS, axis=-1)
    y_conv = gb * short_conv(gc * xa, conv_w)
    q = rmsnorm(zq.reshape(b, n, NA_HEADS, NA_HEAD_DIM), q_norm_w)
    if is_latent:
        k, v = kv_heads(zk, zv, k_norm_w)
        y_att = neighbourhood_attention(q, k, v, kv_ctx[0], kv_ctx[1], rpb)
    else:
        y_att = context_attention(q, kv_ctx[0], kv_ctx[1])
    branches = (y_conv, y_att, fourier_mix(zf), multiscale_pool(zp, pool_w, pool_scale))
    gates = jax.nn.sigmoid(zg.reshape(b, n, N_BRANCH, D_MODEL))
    merged = gates[:, :, 0] * (branches[0] @ w_branch[0])
    for i in range(1, N_BRANCH):
        merged = merged + gates[:, :, i] * (branches[i] @ w_branch[i])
    return merged @ w_out


def expert_choice_ffn(h, w_router, w_gate, w_up, w_down):
    b, n, _ = h.shape
    cap = EC_CAPACITY * n // N_EXPERTS
    logits = jnp.einsum('bnd,de->ben', h, w_router, preferred_element_type=jnp.float32)
    affinity = jax.nn.softmax(logits, axis=1)
    gate, idx = lax.top_k(affinity, cap)
    bi = jnp.arange(b)[:, None, None]
    xs = h[bi, idx]
    a = jnp.einsum('becd,edf->becf', xs, w_gate)
    u = jnp.einsum('becd,edf->becf', xs, w_up)
    y = jnp.einsum('becf,efd->becd', jax.nn.silu(a) * u, w_down) * gate[..., None].astype(h.dtype)
    return jnp.zeros_like(h).at[bi, idx].add(y)


def setup_inputs(seed: int = 0) -> dict:
    key = jax.random.key(seed)
    ks = jax.random.split(key, 21)
    f32 = jnp.float32

    def nrm(k, shape, s):
        return jax.random.normal(k, shape, f32) * s

    return {
        'x': nrm(ks[0], (BATCH, SEQ, D_MODEL), 1.0),
        'c': nrm(ks[1], (BATCH, D_MODEL), 1.0),
        'ctx': nrm(ks[2], (BATCH, CTX_LEN, D_MODEL), 1.0),
        'c_ctx': nrm(ks[3], (D_MODEL,), 1.0),
        'w_ada': nrm(ks[4], (DEPTH, D_MODEL, N_MOD * D_MODEL), 0.5 * D_MODEL ** -0.5),
        'b_ada': nrm(ks[5], (DEPTH, N_MOD * D_MODEL), 0.02),
        'norm1_w': 1.0 + nrm(ks[6], (DEPTH, D_MODEL), 0.02),
        'norm2_w': 1.0 + nrm(ks[7], (DEPTH, D_MODEL), 0.02),
        'w_in': nrm(ks[8], (DEPTH, D_MODEL, IN_COLS), D_MODEL ** -0.5),
        'conv_w': nrm(ks[9], (DEPTH, CONV_WIDTH, MIX_W), CONV_WIDTH ** -0.5),
        'q_norm_w': 1.0 + nrm(ks[10], (DEPTH, NA_HEAD_DIM), 0.02),
        'k_norm_w': 1.0 + nrm(ks[11], (DEPTH, NA_HEAD_DIM), 0.02),
        'na_rpb': nrm(ks[12], (DEPTH, NA_HEADS, 2 * NA_WIN_H - 1, 2 * NA_WIN_W - 1), 0.1),
        'pool_w': nrm(ks[13], (DEPTH, len(POOL_WINDOWS), POOL_GC, POOL_GC), POOL_GC ** -0.5),
        'pool_scale': 1.0 + nrm(ks[14], (DEPTH, MIX_W), 0.1),
        'w_branch': nrm(ks[15], (DEPTH, N_BRANCH, MIX_W, D_MODEL), MIX_W ** -0.5),
        'w_out': nrm(ks[16], (DEPTH, D_MODEL, D_MODEL), D_MODEL ** -0.5),
        'w_router': nrm(ks[17], (DEPTH, D_MODEL, N_EXPERTS), D_MODEL ** -0.5),
        'w_exp_gate': nrm(ks[18], (DEPTH, N_EXPERTS, D_MODEL, EXPERT_FF), D_MODEL ** -0.5),
        'w_exp_up': nrm(ks[19], (DEPTH, N_EXPERTS, D_MODEL, EXPERT_FF), D_MODEL ** -0.5),
        'w_exp_down': nrm(ks[20], (DEPTH, N_EXPERTS, EXPERT_FF, D_MODEL), EXPERT_FF ** -0.5),
    }


def reference(x, c, ctx, c_ctx, w_ada, b_ada, norm1_w, norm2_w, w_in, conv_w, q_norm_w, k_norm_w,
              na_rpb, pool_w, pool_scale, w_branch, w_out, w_router, w_exp_gate, w_exp_up, w_exp_down):
    sc = jax.nn.silu(c)
    scc = jax.nn.silu(c_ctx)
    for l in range(DEPTH):
        last = l == DEPTH - 1
        mod = (sc @ w_ada[l] + b_ada[l])[:, None, :]
        sh1, sc1, g1, sh2, sc2, g2 = jnp.split(mod, N_MOD, axis=-1)
        n_cols = (2 if last else N_MOD) * D_MODEL
        mod_c = jnp.split(scc @ w_ada[l][:, :n_cols] + b_ada[l][:n_cols], n_cols // D_MODEL)
        hc = modulate(rmsnorm(ctx, norm1_w[l]), mod_c[0], mod_c[1])
        if last:
            zkv = hc @ w_in[l][:, K_OFF:V_END]
        else:
            zc = hc @ w_in[l]
            zkv = zc[..., K_OFF:V_END]
        kv_c = kv_heads(zkv[..., :MIX_W], zkv[..., MIX_W:], k_norm_w[l])
        mixer_p = (conv_w[l], q_norm_w[l], k_norm_w[l], na_rpb[l], pool_w[l], pool_scale[l], w_branch[l], w_out[l])
        ffn_p = (w_router[l], w_exp_gate[l], w_exp_up[l], w_exp_down[l])
        h = modulate(rmsnorm(x, norm1_w[l]), sh1, sc1)
        x = x + g1 * token_mixer(h @ w_in[l], kv_c, True, *mixer_p)
        h = modulate(rmsnorm(x, norm2_w[l]), sh2, sc2)
        x = x + g2 * expert_choice_ffn(h, *ffn_p)
        if not last:
            ctx = ctx + mod_c[2] * token_mixer(zc, kv_c, False, *mixer_p)
            hc2 = modulate(rmsnorm(ctx, norm2_w[l]), mod_c[3], mod_c[4])
            ctx = ctx + mod_c[5] * expert_choice_ffn(hc2, *ffn_p)
    return x
```

```python
import functools

import numpy as np
import jax
import jax.numpy as jnp
from jax import lax
from jax.experimental import pallas as pl
from jax.experimental.pallas import tpu as pltpu

F32 = jnp.float32
BF16 = jnp.bfloat16
I32 = jnp.int32
HIGHEST = lax.Precision.HIGHEST

D_MODEL = 2048
GRID_W = 64
MIX_W = D_MODEL // 4
N_HEADS = 8
HEAD_DIM = MIX_W // N_HEADS
WIN_H = 8
WIN_W = 16
ATT_SCALE = HEAD_DIM ** -0.5
POOL_WINDOWS = (2, 4, 8, 16)
POOL_GC = MIX_W // len(POOL_WINDOWS)
FOURIER_GC = MIX_W // 4
N_EXPERTS = 16
EC_CAPACITY = 2
N_MOD = 6
EPS = 1e-6
NEG_INF = -1e30
MIX_COLS = 8 * MIX_W
LANES = 128
HALO = 16
VMEM_LIMIT = 56 * 1024 * 1024

COL_XA, COL_GB, COL_GC, COL_Q, COL_K, COL_V, COL_F, COL_P = range(8)


def _params(sem, vmem=VMEM_LIMIT):
    return pltpu.CompilerParams(dimension_semantics=sem, vmem_limit_bytes=vmem)


def _dot(a, b):
    return jnp.dot(a, b, preferred_element_type=F32)


def _dot_nt(a, b):
    return lax.dot_general(a, b, (((1,), (1,)), ((), ())), preferred_element_type=F32)


def _split3(a):
    hi = a.astype(BF16)
    r1 = a - hi.astype(F32)
    mid = r1.astype(BF16)
    lo = (r1 - mid.astype(F32)).astype(BF16)
    return hi, mid, lo


def _ada_kernel(c_ref, w_ref, b_ref, o_ref):
    c = c_ref[...]
    s = c * jax.nn.sigmoid(c)
    o_ref[0] = jnp.dot(s, w_ref[0], precision=HIGHEST, preferred_element_type=F32) + b_ref[0]


def _ada(c16, w_ada, b_ada):
    depth, d, ncol = w_ada.shape
    tn = 1024
    return pl.pallas_call(
        _ada_kernel,
        grid=(depth, ncol // tn),
        in_specs=[
            pl.BlockSpec((16, d), lambda l, j: (0, 0)),
            pl.BlockSpec((1, d, tn), lambda l, j: (l, 0, j)),
            pl.BlockSpec((1, 1, tn), lambda l, j: (l, 0, j)),
        ],
        out_specs=pl.BlockSpec((1, 16, tn), lambda l, j: (l, 0, j)),
        out_shape=jax.ShapeDtypeStruct((depth, 16, ncol), F32),
        compiler_params=_params(("parallel", "parallel")),
        name="ada",
    )(c16, w_ada, b_ada.reshape(depth, 1, ncol))


def _prenorm_math(x, w, shift, scale):
    ms = jnp.mean(x * x, axis=-1, keepdims=True)
    y = x * lax.rsqrt(ms + EPS) * w
    return y * (1.0 + scale) + shift


def _prenorm_kernel(x_ref, w_ref, sh_ref, sc_ref, o_ref):
    o_ref[0] = _prenorm_math(x_ref[0], w_ref[...], sh_ref[0], sc_ref[0]).astype(BF16)


def _prenorm(x, w, shift, scale):
    b, n, d = x.shape
    tm = min(512, n)
    return pl.pallas_call(
        _prenorm_kernel,
        grid=(b, n // tm),
        in_specs=[
            pl.BlockSpec((1, tm, d), lambda i, j: (i, j, 0)),
            pl.BlockSpec((1, d), lambda i, j: (0, 0)),
            pl.BlockSpec((1, 1, d), lambda i, j: (i, 0, 0)),
            pl.BlockSpec((1, 1, d), lambda i, j: (i, 0, 0)),
        ],
        out_specs=pl.BlockSpec((1, tm, d), lambda i, j: (i, j, 0)),
        out_shape=jax.ShapeDtypeStruct((b, n, d), BF16),
        compiler_params=_params(("parallel", "parallel")),
        name="prenorm",
    )(x, w.reshape(1, d), shift, scale)


def _mm_kernel(a_ref, w_ref, o_ref):
    o_ref[...] = _dot(a_ref[...], w_ref[...]).astype(o_ref.dtype)


def _in_proj(h2d, w, col0, ncols):
    m, k = h2d.shape
    tm = min(1024, m)
    tn = 1024
    c0 = col0 // tn
    return pl.pallas_call(
        _mm_kernel,
        grid=(m // tm, ncols // tn),
        in_specs=[
            pl.BlockSpec((tm, k), lambda i, j: (i, 0)),
            pl.BlockSpec((k, tn), lambda i, j: (0, c0 + j)),
        ],
        out_specs=pl.BlockSpec((tm, tn), lambda i, j: (i, j)),
        out_shape=jax.ShapeDtypeStruct((m, ncols), BF16),
        compiler_params=_params(("parallel", "parallel")),
        name="in_proj",
    )(h2d, w)


def _stencil_kernel(xa_ref, gb_ref, gc_ref, zp_ref, xa_p, xa_n, gc_p, gc_n, zp_p, zp_n,
                    cw_ref, pw_ref, ps_ref, oc_ref, op_ref, *, tm, n_seq):
    i = pl.program_id(1)
    nt = pl.num_programs(1)
    rows = tm + 2 * HALO

    def ext(main_ref, prev_ref, next_ref):
        main = main_ref[0].astype(F32)
        prev = jnp.where(i > 0, prev_ref[0].astype(F32), 0.0)
        nxt = jnp.where(i < nt - 1, next_ref[0].astype(F32), 0.0)
        return jnp.concatenate([prev, main, nxt], axis=0)

    def shifted(a, d):
        if d == 0:
            return a[HALO:HALO + tm]
        return pltpu.roll(a, (-d) % rows, axis=0)[HALO:HALO + tm]

    u = ext(gc_ref, gc_p, gc_n) * ext(xa_ref, xa_p, xa_n)
    cw = cw_ref[...]
    y = cw[0:1] * shifted(u, -1) + cw[1:2] * shifted(u, 0) + cw[2:3] * shifted(u, 1)
    oc_ref[0] = (gb_ref[0].astype(F32) * y).astype(BF16)

    t = (i * tm + lax.broadcasted_iota(I32, (tm, 1), 0)).astype(F32)
    zp = ext(zp_ref, zp_p, zp_n)
    for g, win in enumerate(POOL_WINDOWS):
        ug = zp[:, g * POOL_GC:(g + 1) * POOL_GC]
        acc = ug + pltpu.roll(ug, 1, axis=0)
        half = 1
        while 2 * half < win:
            acc = pltpu.roll(acc, rows - half, axis=0) + pltpu.roll(acc, half, axis=0)
            half *= 2
        wsum = acc[HALO:HALO + tm]
        lo = jnp.maximum(t - (win // 2), 0.0)
        hi = jnp.minimum(t + (win - win // 2 - 1), float(n_seq - 1))
        pooled = wsum / (hi - lo + 1.0) - ug[HALO:HALO + tm]
        yg = _dot(pooled.astype(BF16), pw_ref[g])
        op_ref[0, :, g * POOL_GC:(g + 1) * POOL_GC] = (
            yg * ps_ref[:, g * POOL_GC:(g + 1) * POOL_GC]).astype(BF16)


def _stencil(zmix, conv_w, pool_w, pool_scale):
    b, n, _ = zmix.shape
    tm = min(512, n)
    hb = tm // HALO
    last = n // HALO - 1

    def main(col):
        return pl.BlockSpec((1, tm, MIX_W), lambda bi, i: (bi, i, col))

    def prev(col):
        return pl.BlockSpec((1, HALO, MIX_W), lambda bi, i: (bi, jnp.maximum(i * hb - 1, 0), col))

    def nxt(col):
        return pl.BlockSpec((1, HALO, MIX_W), lambda bi, i: (bi, jnp.minimum((i + 1) * hb, last), col))

    out_spec = pl.BlockSpec((1, tm, MIX_W), lambda bi, i: (bi, i, 0))
    return pl.pallas_call(
        functools.partial(_stencil_kernel, tm=tm, n_seq=n),
        grid=(b, n // tm),
        in_specs=[main(COL_XA), main(COL_GB), main(COL_GC), main(COL_P),
                  prev(COL_XA), nxt(COL_XA), prev(COL_GC), nxt(COL_GC), prev(COL_P), nxt(COL_P),
                  pl.BlockSpec((3, MIX_W), lambda bi, i: (0, 0)),
                  pl.BlockSpec((len(POOL_WINDOWS), POOL_GC, POOL_GC), lambda bi, i: (0, 0, 0)),
                  pl.BlockSpec((1, MIX_W), lambda bi, i: (0, 0))],
        out_specs=[out_spec, out_spec],
        out_shape=[jax.ShapeDtypeStruct((b, n, MIX_W), BF16)] * 2,
        compiler_params=_params(("parallel", "parallel")),
        name="stencil",
    )(zmix, zmix, zmix, zmix, zmix, zmix, zmix, zmix, zmix, zmix,
      conv_w, pool_w.astype(BF16), pool_scale.reshape(1, MIX_W))


def _fourier_kernel(cn_ref, sn_ref, u_ref, cc_ref, sc_ref, o_ref):
    u = u_ref[0]
    p = _dot(cn_ref[...], u).astype(BF16)
    q = _dot(sn_ref[...], u).astype(BF16)
    for g in range(MIX_W // FOURIER_GC):
        sl = slice(g * FOURIER_GC, (g + 1) * FOURIER_GC)
        o_ref[0, :, sl] = (_dot(p[:, sl], cc_ref[...]) - _dot(q[:, sl], sc_ref[...])).astype(BF16)


def _dft_mats(n, scale):
    k = jnp.arange(n, dtype=I32)
    r = (k[:, None] * k[None, :]) % n
    ang = r.astype(F32) * np.float32(2.0 * np.pi / n)
    return (jnp.cos(ang) * scale).astype(BF16), (jnp.sin(ang) * scale).astype(BF16)


def _fourier(zmix, cn, sn, cc, sc):
    b, n, _ = zmix.shape
    tk = min(512, n)
    return pl.pallas_call(
        _fourier_kernel,
        grid=(n // tk, b),
        in_specs=[
            pl.BlockSpec((tk, n), lambda k, bi: (k, 0)),
            pl.BlockSpec((tk, n), lambda k, bi: (k, 0)),
            pl.BlockSpec((1, n, MIX_W), lambda k, bi: (bi, 0, COL_F)),
            pl.BlockSpec((FOURIER_GC, FOURIER_GC), lambda k, bi: (0, 0)),
            pl.BlockSpec((FOURIER_GC, FOURIER_GC), lambda k, bi: (0, 0)),
        ],
        out_specs=pl.BlockSpec((1, tk, MIX_W), lambda k, bi: (bi, k, 0)),
        out_shape=jax.ShapeDtypeStruct((b, n, MIX_W), BF16),
        compiler_params=_params(("parallel", "parallel")),
        name="fourier",
    )(cn, sn, zmix, cc, sc)


def _head_rms(x, w, bd):
    x2 = x * x
    hi = x2.astype(BF16)
    lo = (x2 - hi.astype(F32)).astype(BF16)
    ms = (_dot(hi, bd) + _dot(lo, bd)) * (1.0 / HEAD_DIM)
    return x * lax.rsqrt(ms + EPS) * w


def _pair_stack(q2):
    lane = lax.broadcasted_iota(I32, q2.shape, 1)
    zero = jnp.zeros_like(q2)
    return jnp.concatenate([jnp.where(lane < HEAD_DIM, q2, zero),
                            jnp.where(lane >= HEAD_DIM, q2, zero)], axis=0)


def _pair_unstack(o2, m):
    lane = lax.broadcasted_iota(I32, (m, LANES), 1)
    return jnp.where(lane < HEAD_DIM, o2[:m], o2[m:])


def _na_kernel(q_ref, k_ref, v_ref, kc_ref, vc_ref, qw_ref, kw_ref, bd_ref, tb_ref, o_ref,
               kn_scr, kcn_scr, qn_scr, *, rt, n_rows):
    rb = pl.program_id(1)
    n_seq = n_rows * GRID_W
    chunk = 512

    @pl.when(rb == 0)
    def _():
        def body(c, carry):
            sl = pl.ds(pl.multiple_of(c * chunk, chunk), chunk)
            kn_scr[sl, :] = _head_rms(k_ref[0, sl, :].astype(F32), kw_ref[...], bd_ref[...]).astype(BF16)
            return carry
        lax.fori_loop(0, n_seq // chunk, body, 0)
        kcn_scr[...] = _head_rms(kc_ref[0].astype(F32), kw_ref[...], bd_ref[...]).astype(BF16)

    qn_scr[...] = (_head_rms(q_ref[0].astype(F32), qw_ref[...], bd_ref[...]) * ATT_SCALE).astype(BF16)

    kh = min(WIN_H, n_rows)
    nloc = kh * GRID_W

    def row_body(j, carry):
        r = rb * rt + j
        rs = jnp.clip(r - kh // 2, 0, n_rows - kh)
        dr0 = rs - r + (WIN_H - 1)
        q_rows = pl.ds(pl.multiple_of(j * GRID_W, GRID_W), GRID_W)
        k_rows = pl.ds(pl.multiple_of(rs * GRID_W, GRID_W), nloc)
        for p in range(N_HEADS // 2):
            cols = slice(p * LANES, (p + 1) * LANES)
            qs = _pair_stack(qn_scr[q_rows, cols])
            s_loc = _dot_nt(qs, kn_scr[k_rows, cols]) + tb_ref[dr0, p]
            s_ctx = _dot_nt(qs, kcn_scr[:, cols])
            m = jnp.maximum(jnp.max(s_loc, axis=1, keepdims=True), jnp.max(s_ctx, axis=1, keepdims=True))
            p_loc = jnp.exp(s_loc - m)
            p_ctx = jnp.exp(s_ctx - m)
            denom = jnp.sum(p_loc, axis=1, keepdims=True) + jnp.sum(p_ctx, axis=1, keepdims=True)
            o2 = _dot(p_loc.astype(BF16), v_ref[0, k_rows, cols]) + _dot(p_ctx.astype(BF16), vc_ref[0, :, cols])
            o2 = o2 / denom
            o_ref[0, q_rows, cols] = _pair_unstack(o2, GRID_W).astype(BF16)
        return carry

    lax.fori_loop(0, rt, row_body, 0)


def _na_bias_table(rpb, n_rows):
    kh = min(WIN_H, n_rows)
    c = np.arange(GRID_W)
    cs = np.clip(c - WIN_W // 2, 0, GRID_W - WIN_W)
    kc = np.arange(GRID_W)
    ok = (kc[None, :] >= cs[:, None]) & (kc[None, :] < cs[:, None] + WIN_W)
    dc = np.clip(kc[None, :] - c[:, None] + (WIN_W - 1), 0, 2 * WIN_W - 2)
    dr = np.arange(WIN_H)[:, None] + np.arange(kh)[None, :]
    dr = np.clip(dr, 0, 2 * WIN_H - 2)
    t = rpb[:, dr][:, :, :, dc]
    t = jnp.where(ok[None, None, None], t.astype(F32), NEG_INF)
    t = jnp.transpose(t, (1, 0, 3, 2, 4))
    return t.reshape(WIN_H, N_HEADS // 2, 2 * GRID_W, kh * GRID_W)


def _neighbourhood_attention(zmix, zc_mix, q_w, k_w, bd, tb):
    b, n, _ = zmix.shape
    l = zc_mix.shape[1]
    n_rows = n // GRID_W
    rt = 8
    nloc = min(WIN_H, n_rows) * GRID_W
    return pl.pallas_call(
        functools.partial(_na_kernel, rt=rt, n_rows=n_rows),
        grid=(b, n_rows // rt),
        in_specs=[
            pl.BlockSpec((1, rt * GRID_W, MIX_W), lambda bi, r: (bi, r, COL_Q)),
            pl.BlockSpec((1, n, MIX_W), lambda bi, r: (bi, 0, COL_K)),
            pl.BlockSpec((1, n, MIX_W), lambda bi, r: (bi, 0, COL_V)),
            pl.BlockSpec((1, l, MIX_W), lambda bi, r: (bi, 0, COL_K)),
            pl.BlockSpec((1, l, MIX_W), lambda bi, r: (bi, 0, COL_V)),
            pl.BlockSpec((1, MIX_W), lambda bi, r: (0, 0)),
            pl.BlockSpec((1, MIX_W), lambda bi, r: (0, 0)),
            pl.BlockSpec((MIX_W, MIX_W), lambda bi, r: (0, 0)),
            pl.BlockSpec((WIN_H, N_HEADS // 2, 2 * GRID_W, nloc), lambda bi, r: (0, 0, 0, 0)),
        ],
        out_specs=pl.BlockSpec((1, rt * GRID_W, MIX_W), lambda bi, r: (bi, r, 0)),
        out_shape=jax.ShapeDtypeStruct((b, n, MIX_W), BF16),
        scratch_shapes=[pltpu.VMEM((n, MIX_W), BF16), pltpu.VMEM((l, MIX_W), BF16),
                        pltpu.VMEM((rt * GRID_W, MIX_W), BF16)],
        compiler_params=_params(("parallel", "arbitrary")),
        name="nbr_attention",
    )(zmix, zmix, zmix, zc_mix, zc_mix, q_w, k_w, bd, tb)


def _ctx_attn_kernel(q_ref, k_ref, v_ref, qw_ref, kw_ref, bd_ref, o_ref):
    l = q_ref.shape[1]
    qn = (_head_rms(q_ref[0].astype(F32), qw_ref[...], bd_ref[...]) * ATT_SCALE).astype(BF16)
    kn = _head_rms(k_ref[0].astype(F32), kw_ref[...], bd_ref[...]).astype(BF16)
    for p in range(N_HEADS // 2):
        cols = slice(p * LANES, (p + 1) * LANES)
        qs = _pair_stack(qn[:, cols])
        s = _dot_nt(qs, kn[:, cols])
        m = jnp.max(s, axis=1, keepdims=True)
        e = jnp.exp(s - m)
        o2 = _dot(e.astype(BF16), v_ref[0, :, cols]) / jnp.sum(e, axis=1, keepdims=True)
        o_ref[0, :, cols] = _pair_unstack(o2, l).astype(BF16)


def _context_attention(zc_mix, q_w, k_w, bd):
    b, l, _ = zc_mix.shape
    vec = pl.BlockSpec((1, MIX_W), lambda bi: (0, 0))
    return pl.pallas_call(
        _ctx_attn_kernel,
        grid=(b,),
        in_specs=[
            pl.BlockSpec((1, l, MIX_W), lambda bi: (bi, 0, COL_Q)),
            pl.BlockSpec((1, l, MIX_W), lambda bi: (bi, 0, COL_K)),
            pl.BlockSpec((1, l, MIX_W), lambda bi: (bi, 0, COL_V)),
            vec, vec,
            pl.BlockSpec((MIX_W, MIX_W), lambda bi: (0, 0)),
        ],
        out_specs=pl.BlockSpec((1, l, MIX_W), lambda bi: (bi, 0, 0)),
        out_shape=jax.ShapeDtypeStruct((b, l, MIX_W), BF16),
        compiler_params=_params(("parallel",)),
        name="ctx_attention",
    )(zc_mix, zc_mix, zc_mix, q_w, k_w, bd)


def _merge_kernel(h_ref, y0, y1, y2, y3, g0, g1, g2, g3, wb_ref, o_ref):
    h = h_ref[...]
    acc = None
    for i, (y, g) in enumerate(((y0, g0), (y1, g1), (y2, g2), (y3, g3))):
        gate = jax.nn.sigmoid(_dot(h, g[...]))
        term = gate * _dot(y[...], wb_ref[i])
        acc = term if acc is None else acc + term
    o_ref[...] = acc.astype(BF16)


def _merge(h2d, branches, w_in_bf, w_branch_bf):
    m, d = h2d.shape
    tm = min(1024, m)
    tn = 256
    gate0 = MIX_COLS // tn

    def gspec(i):
        return pl.BlockSpec((d, tn), lambda r, n: (0, gate0 + i * (d // tn) + n))

    yspec = pl.BlockSpec((tm, MIX_W), lambda r, n: (r, 0))
    return pl.pallas_call(
        _merge_kernel,
        grid=(m // tm, d // tn),
        in_specs=[pl.BlockSpec((tm, d), lambda r, n: (r, 0)), yspec, yspec, yspec, yspec,
                  gspec(0), gspec(1), gspec(2), gspec(3),
                  pl.BlockSpec((4, MIX_W, tn), lambda r, n: (0, 0, n))],
        out_specs=pl.BlockSpec((tm, tn), lambda r, n: (r, n)),
        out_shape=jax.ShapeDtypeStruct((m, d), BF16),
        compiler_params=_params(("parallel", "parallel")),
        name="merge",
    )(h2d, *branches, w_in_bf, w_in_bf, w_in_bf, w_in_bf, w_branch_bf)


def _out_proj_kernel(m_ref, w_ref, x_ref, g_ref, nw_ref, sh_ref, sc_ref, wr_ref,
                     x1_ref, h2_ref, aff_ref, affn_ref):
    x1 = x_ref[0] + g_ref[0] * _dot(m_ref[0], w_ref[...])
    x1_ref[0] = x1
    h2 = _prenorm_math(x1, nw_ref[...], sh_ref[0], sc_ref[0])
    h2_ref[0] = h2.astype(BF16)
    logits = jnp.dot(h2, wr_ref[...], precision=HIGHEST, preferred_element_type=F32)
    lane = lax.broadcasted_iota(I32, logits.shape, 1)
    logits = jnp.where(lane < N_EXPERTS, logits, NEG_INF)
    e = jnp.exp(logits - jnp.max(logits, axis=1, keepdims=True))
    aff = e / jnp.sum(e, axis=1, keepdims=True)
    affn_ref[0] = aff
    aff_ref[0] = aff.T[:N_EXPERTS]


def _out_proj(merged, w_out_bf, x, gate, norm_w, shift, scale, wr_pad):
    b, n, d = x.shape
    tm = 256
    vec = pl.BlockSpec((1, 1, d), lambda i, j: (i, 0, 0))
    tile = pl.BlockSpec((1, tm, d), lambda i, j: (i, j, 0))
    return pl.pallas_call(
        _out_proj_kernel,
        grid=(b, n // tm),
        in_specs=[tile, pl.BlockSpec((d, d), lambda i, j: (0, 0)), tile, vec,
                  pl.BlockSpec((1, d), lambda i, j: (0, 0)), vec, vec,
                  pl.BlockSpec((d, LANES), lambda i, j: (0, 0))],
        out_specs=[tile, tile,
                   pl.BlockSpec((1, N_EXPERTS, tm), lambda i, j: (i, 0, j)),
                   pl.BlockSpec((1, tm, LANES), lambda i, j: (i, j, 0))],
        out_shape=[jax.ShapeDtypeStruct((b, n, d), F32), jax.ShapeDtypeStruct((b, n, d), BF16),
                   jax.ShapeDtypeStruct((b, N_EXPERTS, n), F32), jax.ShapeDtypeStruct((b, n, LANES), F32)],
        compiler_params=_params(("parallel", "parallel")),
        name="out_proj",
    )(merged, w_out_bf, x, gate, norm_w.reshape(1, d), shift, scale, wr_pad)


def _lane_cumsum(mask_f, ut):
    rows, n = mask_f.shape
    carry = jnp.zeros((rows, 1), F32)
    parts = []
    for j in range(n // LANES):
        c = _dot(mask_f[:, j * LANES:(j + 1) * LANES].astype(BF16), ut) + carry
        parts.append(c)
        carry = c[:, LANES - 1:LANES]
    return jnp.concatenate(parts, axis=1)


def _select_kernel(aff_ref, slot_ref, slott_ref, *, cap):
    aff = aff_ref[0]
    e, n = aff.shape
    bits = lax.bitcast_convert_type(aff, I32)

    def body(_, carry):
        lo, hi = carry
        mid = lo + ((hi - lo) >> 1)
        cnt = jnp.sum(jnp.where(bits >= mid, 1.0, 0.0), axis=1, keepdims=True)
        ge = cnt >= float(cap)
        return jnp.where(ge, mid, lo), jnp.where(ge, hi, mid)

    lo, _ = lax.fori_loop(0, 31, body, (jnp.zeros((e, 1), I32), jnp.full((e, 1), 0x7F800000, I32)))
    ri = lax.broadcasted_iota(I32, (LANES, LANES), 0)
    ci = lax.broadcasted_iota(I32, (LANES, LANES), 1)
    ut = jnp.where(ri <= ci, 1.0, 0.0).astype(BF16)
    gt = bits > lo
    eq = bits == lo
    need = float(cap) - jnp.sum(jnp.where(gt, 1.0, 0.0), axis=1, keepdims=True)
    ceq = _lane_cumsum(jnp.where(eq, 1.0, 0.0), ut)
    sel = gt | (eq & (ceq <= need))
    csel = _lane_cumsum(jnp.where(sel, 1.0, 0.0), ut)
    slot = jnp.where(sel, csel - 1.0, -1.0)
    slot_ref[0] = slot.astype(I32)
    padded = jnp.concatenate([slot, jnp.full((LANES - e, n), -1.0, F32)], axis=0)
    slott_ref[0] = padded.T


def _select(aff, cap):
    b, e, n = aff.shape
    return pl.pallas_call(
        functools.partial(_select_kernel, cap=cap),
        grid=(b,),
        in_specs=[pl.BlockSpec((1, e, n), lambda i: (i, 0, 0))],
        out_specs=[pl.BlockSpec((1, e, n), lambda i: (i, 0, 0)),
                   pl.BlockSpec((1, n, LANES), lambda i: (i, 0, 0))],
        out_shape=[jax.ShapeDtypeStruct((b, e, n), I32), jax.ShapeDtypeStruct((b, n, LANES), F32)],
        compiler_params=_params(("parallel",)),
        name="select",
    )(aff)


def _gather_kernel(slot_ref, h_ref, affn_ref, xs_ref, gate_ref, acc, gacc, *, cap):
    e = pl.program_id(1)
    kt = pl.program_id(2)
    slot = slot_ref[0, 0]
    tk = slot.shape[1]
    rows = lax.broadcasted_iota(I32, (cap, tk), 0)
    onehot = jnp.where(rows == slot, 1.0, 0.0).astype(BF16)
    contrib = _dot(onehot, h_ref[0])
    hi, mid, lo = _split3(affn_ref[0])
    g = _dot(onehot, hi) + _dot(onehot, mid) + _dot(onehot, lo)

    @pl.when(kt == 0)
    def _():
        acc[...] = contrib
        gacc[...] = g

    @pl.when(kt > 0)
    def _():
        acc[...] += contrib
        gacc[...] += g

    @pl.when(kt == pl.num_programs(2) - 1)
    def _():
        xs_ref[0, 0] = acc[...].astype(BF16)
        lane = lax.broadcasted_iota(I32, (cap, LANES), 1)
        col = jnp.sum(jnp.where(lane == e, gacc[...], 0.0), axis=1, keepdims=True)
        gate_ref[0, 0] = jnp.broadcast_to(col, (cap, LANES))


def _gather(slot, h2, affn, cap):
    b, n, d = h2.shape
    tk = min(1024, n)
    return pl.pallas_call(
        functools.partial(_gather_kernel, cap=cap),
        grid=(b, N_EXPERTS, n // tk),
        in_specs=[
            pl.BlockSpec((1, 1, 1, tk), lambda i, e, k: (i, e, 0, k)),
            pl.BlockSpec((1, tk, d), lambda i, e, k: (i, k, 0)),
            pl.BlockSpec((1, tk, LANES), lambda i, e, k: (i, k, 0)),
        ],
        out_specs=[pl.BlockSpec((1, 1, cap, d), lambda i, e, k: (e, i, 0, 0)),
                   pl.BlockSpec((1, 1, cap, LANES), lambda i, e, k: (e, i, 0, 0))],
        out_shape=[jax.ShapeDtypeStruct((N_EXPERTS, b, cap, d), BF16),
                   jax.ShapeDtypeStruct((N_EXPERTS, b, cap, LANES), F32)],
        scratch_shapes=[pltpu.VMEM((cap, d), F32), pltpu.VMEM((cap, LANES), F32)],
        compiler_params=_params(("parallel", "parallel", "arbitrary")),
        name="ec_gather",
    )(slot.reshape(b, N_EXPERTS, 1, n), h2, affn)


def _expert_kernel(xs_ref, gate_ref, wg_ref, wu_ref, wd_ref, ys_ref, acc):
    f = pl.program_id(2)
    x = xs_ref[0]
    a = _dot(x, wg_ref[0])
    u = _dot(x, wu_ref[0])
    hdn = (a * jax.nn.sigmoid(a) * u).astype(BF16)
    y = _dot(hdn, wd_ref[0])

    @pl.when(f == 0)
    def _():
        acc[...] = y

    @pl.when(f > 0)
    def _():
        acc[...] += y

    @pl.when(f == pl.num_programs(2) - 1)
    def _():
        g = gate_ref[0]
        gfull = jnp.concatenate([g] * (acc.shape[1] // LANES), axis=1)
        ys_ref[0] = (acc[...] * gfull).astype(BF16)


def _experts(xs, gate, wg, wu, wd):
    e, m, d = xs.shape
    ff = wg.shape[2]
    tm = min(1024, m)
    tf = 512
    return pl.pallas_call(
        _expert_kernel,
        grid=(e, m // tm, ff // tf),
        in_specs=[
            pl.BlockSpec((1, tm, d), lambda ei, i, f: (ei, i, 0)),
            pl.BlockSpec((1, tm, LANES), lambda ei, i, f: (ei, i, 0)),
            pl.BlockSpec((1, d, tf), lambda ei, i, f: (ei, 0, f)),
            pl.BlockSpec((1, d, tf), lambda ei, i, f: (ei, 0, f)),
            pl.BlockSpec((1, tf, d), lambda ei, i, f: (ei, f, 0)),
        ],
        out_specs=pl.BlockSpec((1, tm, d), lambda ei, i, f: (ei, i, 0)),
        out_shape=jax.ShapeDtypeStruct((e, m, d), BF16),
        scratch_shapes=[pltpu.VMEM((tm, d), F32)],
        compiler_params=_params(("parallel", "parallel", "arbitrary")),
        name="experts",
    )(xs, gate, wg, wu, wd)


def _scatter_kernel(slott_ref, ys_ref, x_ref, g_ref, o_ref, acc, *, cap):
    e = pl.program_id(2)
    st = slott_ref[0]
    tt = st.shape[0]
    lane = lax.broadcasted_iota(I32, (tt, LANES), 1)
    col = jnp.sum(jnp.where(lane == e, st, 0.0), axis=1, keepdims=True)
    cols = lax.broadcasted_iota(I32, (tt, cap), 1).astype(F32)
    onehot = jnp.where(cols == col, 1.0, 0.0).astype(BF16)
    contrib = _dot(onehot, ys_ref[0, 0])

    @pl.when(e == 0)
    def _():
        acc[...] = contrib

    @pl.when(e > 0)
    def _():
        acc[...] += contrib

    @pl.when(e == pl.num_programs(2) - 1)
    def _():
        o_ref[0] = x_ref[0] + g_ref[0] * acc[...]


def _scatter(slott, ys, x1, gate, cap):
    b, n, d = x1.shape
    tt = min(512, n)
    return pl.pallas_call(
        functools.partial(_scatter_kernel, cap=cap),
        grid=(b, n // tt, N_EXPERTS),
        in_specs=[
            pl.BlockSpec((1, tt, LANES), lambda i, t, e: (i, t, 0)),
            pl.BlockSpec((1, 1, cap, d), lambda i, t, e: (e, i, 0, 0)),
            pl.BlockSpec((1, tt, d), lambda i, t, e: (i, t, 0)),
            pl.BlockSpec((1, 1, d), lambda i, t, e: (i, 0, 0)),
        ],
        out_specs=pl.BlockSpec((1, tt, d), lambda i, t, e: (i, t, 0)),
        out_shape=jax.ShapeDtypeStruct((b, n, d), F32),
        scratch_shapes=[pltpu.VMEM((tt, d), F32)],
        compiler_params=_params(("parallel", "parallel", "arbitrary")),
        name="ec_scatter",
    )(slott, ys, x1, gate)


def _mod_chunks(mod_rows):
    b = mod_rows.shape[0]
    return [mod_rows[:, i * D_MODEL:(i + 1) * D_MODEL].reshape(b, 1, D_MODEL) for i in range(N_MOD)]


def _token_mixer(h, zmix, att, lw, dft):
    b, n, d = h.shape
    y_conv, y_pool = _stencil(zmix, lw["conv_w"], lw["pool_w"], lw["pool_scale"])
    y_f = _fourier(zmix, *dft)
    branches = [y.reshape(b * n, MIX_W) for y in (y_conv, att, y_f, y_pool)]
    merged = _merge(h.reshape(b * n, d), branches, lw["w_in"], lw["w_branch"])
    return merged.reshape(b, n, d)


def _ffn(x1, h2, aff, affn, g2, lw):
    b, n, d = x1.shape
    cap = EC_CAPACITY * n // N_EXPERTS
    slot, slott = _select(aff, cap)
    xs, gate = _gather(slot, h2, affn, cap)
    ys = _experts(xs.reshape(N_EXPERTS, b * cap, d), gate.reshape(N_EXPERTS, b * cap, LANES),
                  lw["w_gate"], lw["w_up"], lw["w_down"])
    return _scatter(slott, ys.reshape(N_EXPERTS, b, cap, d), x1, g2, cap)


def kernel(x, c, ctx, c_ctx, w_ada, b_ada, norm1_w, norm2_w, w_in, conv_w, q_norm_w, k_norm_w,
           na_rpb, pool_w, pool_scale, w_branch, w_out, w_router, w_exp_gate, w_exp_up, w_exp_down):
    bsz, n, d = x.shape
    l_ctx = ctx.shape[1]
    depth = w_ada.shape[0]
    n_rows = n // GRID_W

    c16 = jnp.zeros((16, d), F32).at[:bsz].set(c).at[bsz].set(c_ctx)
    mod_all = _ada(c16, w_ada, b_ada)

    hd = np.arange(MIX_W) // HEAD_DIM
    bd = jnp.asarray(hd[:, None] == hd[None, :], dtype=BF16)
    dft_lat = _dft_mats(n, 1.0) + _dft_mats(FOURIER_GC, (n * FOURIER_GC) ** -0.5)
    dft_ctx = _dft_mats(l_ctx, 1.0) + _dft_mats(FOURIER_GC, (l_ctx * FOURIER_GC) ** -0.5)

    for l in range(depth):
        last = l == depth - 1
        lw = {
            "w_in": w_in[l].astype(BF16), "conv_w": conv_w[l], "pool_w": pool_w[l],
            "pool_scale": pool_scale[l], "w_branch": w_branch[l].astype(BF16),
            "w_gate": w_exp_gate[l].astype(BF16), "w_up": w_exp_up[l].astype(BF16),
            "w_down": w_exp_down[l].astype(BF16),
        }
        w_out_bf = w_out[l].astype(BF16)
        wr_pad = jnp.zeros((d, LANES), F32).at[:, :N_EXPERTS].set(w_router[l])
        q_w = jnp.tile(q_norm_w[l], N_HEADS).reshape(1, MIX_W)
        k_w = jnp.tile(k_norm_w[l], N_HEADS).reshape(1, MIX_W)
        tb = _na_bias_table(na_rpb[l], n_rows)
        sh1, sc1, g1, sh2, sc2, g2 = _mod_chunks(mod_all[l, :bsz])
        mc = _mod_chunks(jnp.broadcast_to(mod_all[l, bsz:bsz + 1], (bsz, N_MOD * d)))

        hc = _prenorm(ctx, norm1_w[l], mc[0], mc[1])
        zc_mix = _in_proj(hc.reshape(bsz * l_ctx, d), lw["w_in"], 0, MIX_COLS).reshape(bsz, l_ctx, MIX_COLS)

        h = _prenorm(x, norm1_w[l], sh1, sc1)
        zmix = _in_proj(h.reshape(bsz * n, d), lw["w_in"], 0, MIX_COLS).reshape(bsz, n, MIX_COLS)
        att = _neighbourhood_attention(zmix, zc_mix, q_w, k_w, bd, tb)
        merged = _token_mixer(h, zmix, att, lw, dft_lat)
        x1, h2, aff, affn = _out_proj(merged, w_out_bf, x, g1, norm2_w[l], sh2, sc2, wr_pad)
        x = _ffn(x1, h2, aff, affn, g2, lw)

        if not last:
            att_c = _context_attention(zc_mix, q_w, k_w, bd)
            merged_c = _token_mixer(hc, zc_mix, att_c, lw, dft_ctx)
            c1, hc2, aff_c, affn_c = _out_proj(merged_c, w_out_bf, ctx, mc[2], norm2_w[l], mc[3], mc[4], wr_pad)
            ctx = _ffn(c1, hc2, aff_c, affn_c, mc[5], lw)
    return x
```

```python
import functools

import numpy as np
import jax
import jax.numpy as jnp
from jax import lax
from jax.experimental import pallas as pl
from jax.experimental.pallas import tpu as pltpu

F32 = jnp.float32
BF16 = jnp.bfloat16
I32 = jnp.int32
HIGHEST = lax.Precision.HIGHEST

D_MODEL = 2048
GRID_W = 64
MIX_W = D_MODEL // 4
N_HEADS = 8
HEAD_DIM = MIX_W // N_HEADS
WIN_H = 8
WIN_W = 16
ATT_SCALE = HEAD_DIM ** -0.5
POOL_WINDOWS = (2, 4, 8, 16)
POOL_GC = MIX_W // len(POOL_WINDOWS)
FOURIER_GC = MIX_W // 4
N_EXPERTS = 16
EC_CAPACITY = 2
N_MOD = 6
EPS = 1e-6
NEG_INF = -1e30
MIX_COLS = 8 * MIX_W
LANES = 128
HALO = 16
SEL_TILE = 256
VMEM_LIMIT = 56 * 1024 * 1024

COL_XA, COL_GB, COL_GC, COL_Q, COL_K, COL_V, COL_F, COL_P = range(8)


def _params(sem, vmem=VMEM_LIMIT):
    return pltpu.CompilerParams(dimension_semantics=sem, vmem_limit_bytes=vmem)


def _dot(a, b):
    return jnp.dot(a, b, preferred_element_type=F32)


def _dot_nt(a, b):
    return lax.dot_general(a, b, (((1,), (1,)), ((), ())), preferred_element_type=F32)


def _split3(a):
    hi = a.astype(BF16)
    r1 = a - hi.astype(F32)
    mid = r1.astype(BF16)
    lo = (r1 - mid.astype(F32)).astype(BF16)
    return hi, mid, lo


def _ada_kernel(c_ref, w_ref, b_ref, o_ref):
    c = c_ref[...]
    s = c * jax.nn.sigmoid(c)
    o_ref[0] = jnp.dot(s, w_ref[0], precision=HIGHEST, preferred_element_type=F32) + b_ref[0]


def _ada(c16, w_ada, b_ada):
    depth, d, ncol = w_ada.shape
    tn = 1024
    return pl.pallas_call(
        _ada_kernel,
        grid=(depth, ncol // tn),
        in_specs=[
            pl.BlockSpec((16, d), lambda l, j: (0, 0)),
            pl.BlockSpec((1, d, tn), lambda l, j: (l, 0, j)),
            pl.BlockSpec((1, 1, tn), lambda l, j: (l, 0, j)),
        ],
        out_specs=pl.BlockSpec((1, 16, tn), lambda l, j: (l, 0, j)),
        out_shape=jax.ShapeDtypeStruct((depth, 16, ncol), F32),
        compiler_params=_params(("parallel", "parallel")),
        name="ada",
    )(c16, w_ada, b_ada.reshape(depth, 1, ncol))


def _prenorm_math(x, w, shift, scale):
    ms = jnp.mean(x * x, axis=-1, keepdims=True)
    y = x * lax.rsqrt(ms + EPS) * w
    return y * (1.0 + scale) + shift


def _prenorm_kernel(x_ref, w_ref, sh_ref, sc_ref, o_ref):
    o_ref[0] = _prenorm_math(x_ref[0], w_ref[...], sh_ref[0], sc_ref[0]).astype(BF16)


def _prenorm(x, w, shift, scale):
    b, n, d = x.shape
    tm = min(512, n)
    return pl.pallas_call(
        _prenorm_kernel,
        grid=(b, n // tm),
        in_specs=[
            pl.BlockSpec((1, tm, d), lambda i, j: (i, j, 0)),
            pl.BlockSpec((1, d), lambda i, j: (0, 0)),
            pl.BlockSpec((1, 1, d), lambda i, j: (i, 0, 0)),
            pl.BlockSpec((1, 1, d), lambda i, j: (i, 0, 0)),
        ],
        out_specs=pl.BlockSpec((1, tm, d), lambda i, j: (i, j, 0)),
        out_shape=jax.ShapeDtypeStruct((b, n, d), BF16),
        compiler_params=_params(("parallel", "parallel")),
        name="prenorm",
    )(x, w.reshape(1, d), shift, scale)


def _mm_kernel(a_ref, w_ref, o_ref):
    o_ref[...] = _dot(a_ref[...], w_ref[...]).astype(o_ref.dtype)


def _in_proj(h2d, w, col0, ncols):
    m, k = h2d.shape
    tm = min(1024, m)
    tn = 1024
    c0 = col0 // tn
    return pl.pallas_call(
        _mm_kernel,
        grid=(m // tm, ncols // tn),
        in_specs=[
            pl.BlockSpec((tm, k), lambda i, j: (i, 0)),
            pl.BlockSpec((k, tn), lambda i, j: (0, c0 + j)),
        ],
        out_specs=pl.BlockSpec((tm, tn), lambda i, j: (i, j)),
        out_shape=jax.ShapeDtypeStruct((m, ncols), BF16),
        compiler_params=_params(("parallel", "parallel")),
        name="in_proj",
    )(h2d, w)


def _stencil_kernel(xa_ref, gb_ref, gc_ref, zp_ref, xa_p, xa_n, gc_p, gc_n, zp_p, zp_n,
                    cw_ref, pw_ref, ps_ref, oc_ref, op_ref, *, tm, n_seq):
    i = pl.program_id(1)
    nt = pl.num_programs(1)
    rows = tm + 2 * HALO

    def ext(main_ref, prev_ref, next_ref):
        main = main_ref[0].astype(F32)
        prev = jnp.where(i > 0, prev_ref[0].astype(F32), 0.0)
        nxt = jnp.where(i < nt - 1, next_ref[0].astype(F32), 0.0)
        return jnp.concatenate([prev, main, nxt], axis=0)

    def shifted(a, d):
        if d == 0:
            return a[HALO:HALO + tm]
        return pltpu.roll(a, (-d) % rows, axis=0)[HALO:HALO + tm]

    u = ext(gc_ref, gc_p, gc_n) * ext(xa_ref, xa_p, xa_n)
    cw = cw_ref[...]
    y = cw[0:1] * shifted(u, -1) + cw[1:2] * shifted(u, 0) + cw[2:3] * shifted(u, 1)
    oc_ref[0] = (gb_ref[0].astype(F32) * y).astype(BF16)

    t = (i * tm + lax.broadcasted_iota(I32, (tm, 1), 0)).astype(F32)
    zp = ext(zp_ref, zp_p, zp_n)
    for g, win in enumerate(POOL_WINDOWS):
        ug = zp[:, g * POOL_GC:(g + 1) * POOL_GC]
        acc = ug + pltpu.roll(ug, 1, axis=0)
        half = 1
        while 2 * half < win:
            acc = pltpu.roll(acc, rows - half, axis=0) + pltpu.roll(acc, half, axis=0)
            half *= 2
        wsum = acc[HALO:HALO + tm]
        lo = jnp.maximum(t - (win // 2), 0.0)
        hi = jnp.minimum(t + (win - win // 2 - 1), float(n_seq - 1))
        pooled = wsum / (hi - lo + 1.0) - ug[HALO:HALO + tm]
        yg = _dot(pooled.astype(BF16), pw_ref[g])
        op_ref[0, :, g * POOL_GC:(g + 1) * POOL_GC] = (
            yg * ps_ref[:, g * POOL_GC:(g + 1) * POOL_GC]).astype(BF16)


def _stencil(zmix, conv_w, pool_w, pool_scale):
    b, n, _ = zmix.shape
    tm = min(512, n)
    hb = tm // HALO
    last = n // HALO - 1

    def main(col):
        return pl.BlockSpec((1, tm, MIX_W), lambda bi, i: (bi, i, col))

    def prev(col):
        return pl.BlockSpec((1, HALO, MIX_W), lambda bi, i: (bi, jnp.maximum(i * hb - 1, 0), col))

    def nxt(col):
        return pl.BlockSpec((1, HALO, MIX_W), lambda bi, i: (bi, jnp.minimum((i + 1) * hb, last), col))

    out_spec = pl.BlockSpec((1, tm, MIX_W), lambda bi, i: (bi, i, 0))
    return pl.pallas_call(
        functools.partial(_stencil_kernel, tm=tm, n_seq=n),
        grid=(b, n // tm),
        in_specs=[main(COL_XA), main(COL_GB), main(COL_GC), main(COL_P),
                  prev(COL_XA), nxt(COL_XA), prev(COL_GC), nxt(COL_GC), prev(COL_P), nxt(COL_P),
                  pl.BlockSpec((3, MIX_W), lambda bi, i: (0, 0)),
                  pl.BlockSpec((len(POOL_WINDOWS), POOL_GC, POOL_GC), lambda bi, i: (0, 0, 0)),
                  pl.BlockSpec((1, MIX_W), lambda bi, i: (0, 0))],
        out_specs=[out_spec, out_spec],
        out_shape=[jax.ShapeDtypeStruct((b, n, MIX_W), BF16)] * 2,
        compiler_params=_params(("parallel", "parallel")),
        name="stencil",
    )(zmix, zmix, zmix, zmix, zmix, zmix, zmix, zmix, zmix, zmix,
      conv_w, pool_w.astype(BF16), pool_scale.reshape(1, MIX_W))


def _fourier_kernel(cn_ref, sn_ref, u_ref, cc_ref, sc_ref, o_ref):
    u = u_ref[0]
    p = _dot(cn_ref[...], u).astype(BF16)
    q = _dot(sn_ref[...], u).astype(BF16)
    for g in range(MIX_W // FOURIER_GC):
        sl = slice(g * FOURIER_GC, (g + 1) * FOURIER_GC)
        o_ref[0, :, sl] = (_dot(p[:, sl], cc_ref[...]) - _dot(q[:, sl], sc_ref[...])).astype(BF16)


def _dft_mats(n, scale):
    k = jnp.arange(n, dtype=I32)
    r = (k[:, None] * k[None, :]) % n
    ang = r.astype(F32) * np.float32(2.0 * np.pi / n)
    return (jnp.cos(ang) * scale).astype(BF16), (jnp.sin(ang) * scale).astype(BF16)


def _fourier(zmix, cn, sn, cc, sc):
    b, n, _ = zmix.shape
    tk = min(512, n)
    return pl.pallas_call(
        _fourier_kernel,
        grid=(n // tk, b),
        in_specs=[
            pl.BlockSpec((tk, n), lambda k, bi: (k, 0)),
            pl.BlockSpec((tk, n), lambda k, bi: (k, 0)),
            pl.BlockSpec((1, n, MIX_W), lambda k, bi: (bi, 0, COL_F)),
            pl.BlockSpec((FOURIER_GC, FOURIER_GC), lambda k, bi: (0, 0)),
            pl.BlockSpec((FOURIER_GC, FOURIER_GC), lambda k, bi: (0, 0)),
        ],
        out_specs=pl.BlockSpec((1, tk, MIX_W), lambda k, bi: (bi, k, 0)),
        out_shape=jax.ShapeDtypeStruct((b, n, MIX_W), BF16),
        compiler_params=_params(("parallel", "parallel")),
        name="fourier",
    )(cn, sn, zmix, cc, sc)


def _head_rms(x, w, bd):
    x2 = x * x
    hi = x2.astype(BF16)
    lo = (x2 - hi.astype(F32)).astype(BF16)
    ms = (_dot(hi, bd) + _dot(lo, bd)) * (1.0 / HEAD_DIM)
    return x * lax.rsqrt(ms + EPS) * w


def _pair_stack(q2):
    lane = lax.broadcasted_iota(I32, q2.shape, 1)
    zero = jnp.zeros_like(q2)
    return jnp.concatenate([jnp.where(lane < HEAD_DIM, q2, zero),
                            jnp.where(lane >= HEAD_DIM, q2, zero)], axis=0)


def _pair_unstack(o2, m):
    lane = lax.broadcasted_iota(I32, (m, LANES), 1)
    return jnp.where(lane < HEAD_DIM, o2[:m], o2[m:])


def _na_kernel(q_ref, k_ref, v_ref, kc_ref, vc_ref, qw_ref, kw_ref, bd_ref, tb_ref, o_ref,
               kn_scr, kcn_scr, qn_scr, *, rt, n_rows):
    rb = pl.program_id(1)
    n_seq = n_rows * GRID_W
    chunk = 512

    @pl.when(rb == 0)
    def _():
        def body(c, carry):
            sl = pl.ds(pl.multiple_of(c * chunk, chunk), chunk)
            kn_scr[sl, :] = _head_rms(k_ref[0, sl, :].astype(F32), kw_ref[...], bd_ref[...]).astype(BF16)
            return carry
        lax.fori_loop(0, n_seq // chunk, body, 0)
        kcn_scr[...] = _head_rms(kc_ref[0].astype(F32), kw_ref[...], bd_ref[...]).astype(BF16)

    qn_scr[...] = (_head_rms(q_ref[0].astype(F32), qw_ref[...], bd_ref[...]) * ATT_SCALE).astype(BF16)

    kh = min(WIN_H, n_rows)
    nloc = kh * GRID_W

    def row_body(j, carry):
        r = rb * rt + j
        rs = jnp.clip(r - kh // 2, 0, n_rows - kh)
        dr0 = rs - r + (WIN_H - 1)
        q_rows = pl.ds(pl.multiple_of(j * GRID_W, GRID_W), GRID_W)
        k_rows = pl.ds(pl.multiple_of(rs * GRID_W, GRID_W), nloc)
        for p in range(N_HEADS // 2):
            cols = slice(p * LANES, (p + 1) * LANES)
            qs = _pair_stack(qn_scr[q_rows, cols])
            s_loc = _dot_nt(qs, kn_scr[k_rows, cols]) + tb_ref[dr0, p]
            s_ctx = _dot_nt(qs, kcn_scr[:, cols])
            m = jnp.maximum(jnp.max(s_loc, axis=1, keepdims=True), jnp.max(s_ctx, axis=1, keepdims=True))
            p_loc = jnp.exp(s_loc - m)
            p_ctx = jnp.exp(s_ctx - m)
            denom = jnp.sum(p_loc, axis=1, keepdims=True) + jnp.sum(p_ctx, axis=1, keepdims=True)
            o2 = _dot(p_loc.astype(BF16), v_ref[0, k_rows, cols]) + _dot(p_ctx.astype(BF16), vc_ref[0, :, cols])
            o2 = o2 / denom
            o_ref[0, q_rows, cols] = _pair_unstack(o2, GRID_W).astype(BF16)
        return carry

    lax.fori_loop(0, rt, row_body, 0)


def _na_bias_table(rpb, n_rows):
    kh = min(WIN_H, n_rows)
    c = np.arange(GRID_W)
    cs = np.clip(c - WIN_W // 2, 0, GRID_W - WIN_W)
    kc = np.arange(GRID_W)
    ok = (kc[None, :] >= cs[:, None]) & (kc[None, :] < cs[:, None] + WIN_W)
    dc = np.clip(kc[None, :] - c[:, None] + (WIN_W - 1), 0, 2 * WIN_W - 2)
    dr = np.arange(WIN_H)[:, None] + np.arange(kh)[None, :]
    dr = np.clip(dr, 0, 2 * WIN_H - 2)
    t = rpb[:, dr][:, :, :, dc]
    t = jnp.where(ok[None, None, None], t.astype(F32), NEG_INF)
    t = jnp.transpose(t, (1, 0, 3, 2, 4))
    return t.reshape(WIN_H, N_HEADS // 2, 2 * GRID_W, kh * GRID_W)


def _neighbourhood_attention(zmix, zc_mix, q_w, k_w, bd, tb):
    b, n, _ = zmix.shape
    l = zc_mix.shape[1]
    n_rows = n // GRID_W
    rt = 8
    nloc = min(WIN_H, n_rows) * GRID_W
    return pl.pallas_call(
        functools.partial(_na_kernel, rt=rt, n_rows=n_rows),
        grid=(b, n_rows // rt),
        in_specs=[
            pl.BlockSpec((1, rt * GRID_W, MIX_W), lambda bi, r: (bi, r, COL_Q)),
            pl.BlockSpec((1, n, MIX_W), lambda bi, r: (bi, 0, COL_K)),
            pl.BlockSpec((1, n, MIX_W), lambda bi, r: (bi, 0, COL_V)),
            pl.BlockSpec((1, l, MIX_W), lambda bi, r: (bi, 0, COL_K)),
            pl.BlockSpec((1, l, MIX_W), lambda bi, r: (bi, 0, COL_V)),
            pl.BlockSpec((1, MIX_W), lambda bi, r: (0, 0)),
            pl.BlockSpec((1, MIX_W), lambda bi, r: (0, 0)),
            pl.BlockSpec((MIX_W, MIX_W), lambda bi, r: (0, 0)),
            pl.BlockSpec((WIN_H, N_HEADS // 2, 2 * GRID_W, nloc), lambda bi, r: (0, 0, 0, 0),
                         pipeline_mode=pl.Buffered(1)),
        ],
        out_specs=pl.BlockSpec((1, rt * GRID_W, MIX_W), lambda bi, r: (bi, r, 0)),
        out_shape=jax.ShapeDtypeStruct((b, n, MIX_W), BF16),
        scratch_shapes=[pltpu.VMEM((n, MIX_W), BF16), pltpu.VMEM((l, MIX_W), BF16),
                        pltpu.VMEM((rt * GRID_W, MIX_W), BF16)],
        compiler_params=_params(("parallel", "arbitrary")),
        name="nbr_attention",
    )(zmix, zmix, zmix, zc_mix, zc_mix, q_w, k_w, bd, tb)


def _ctx_attn_kernel(q_ref, k_ref, v_ref, qw_ref, kw_ref, bd_ref, o_ref):
    l = q_ref.shape[1]
    qn = (_head_rms(q_ref[0].astype(F32), qw_ref[...], bd_ref[...]) * ATT_SCALE).astype(BF16)
    kn = _head_rms(k_ref[0].astype(F32), kw_ref[...], bd_ref[...]).astype(BF16)
    for p in range(N_HEADS // 2):
        cols = slice(p * LANES, (p + 1) * LANES)
        qs = _pair_stack(qn[:, cols])
        s = _dot_nt(qs, kn[:, cols])
        m = jnp.max(s, axis=1, keepdims=True)
        e = jnp.exp(s - m)
        o2 = _dot(e.astype(BF16), v_ref[0, :, cols]) / jnp.sum(e, axis=1, keepdims=True)
        o_ref[0, :, cols] = _pair_unstack(o2, l).astype(BF16)


def _context_attention(zc_mix, q_w, k_w, bd):
    b, l, _ = zc_mix.shape
    vec = pl.BlockSpec((1, MIX_W), lambda bi: (0, 0))
    return pl.pallas_call(
        _ctx_attn_kernel,
        grid=(b,),
        in_specs=[
            pl.BlockSpec((1, l, MIX_W), lambda bi: (bi, 0, COL_Q)),
            pl.BlockSpec((1, l, MIX_W), lambda bi: (bi, 0, COL_K)),
            pl.BlockSpec((1, l, MIX_W), lambda bi: (bi, 0, COL_V)),
            vec, vec,
            pl.BlockSpec((MIX_W, MIX_W), lambda bi: (0, 0)),
        ],
        out_specs=pl.BlockSpec((1, l, MIX_W), lambda bi: (bi, 0, 0)),
        out_shape=jax.ShapeDtypeStruct((b, l, MIX_W), BF16),
        compiler_params=_params(("parallel",)),
        name="ctx_attention",
    )(zc_mix, zc_mix, zc_mix, q_w, k_w, bd)


def _merge_kernel(h_ref, y0, y1, y2, y3, g0, g1, g2, g3, wb_ref, o_ref):
    h = h_ref[...]
    acc = None
    for i, (y, g) in enumerate(((y0, g0), (y1, g1), (y2, g2), (y3, g3))):
        gate = jax.nn.sigmoid(_dot(h, g[...]))
        term = gate * _dot(y[...], wb_ref[i])
        acc = term if acc is None else acc + term
    o_ref[...] = acc.astype(BF16)


def _merge(h2d, branches, w_in_bf, w_branch_bf):
    m, d = h2d.shape
    tm = min(1024, m)
    tn = 256
    gate0 = MIX_COLS // tn

    def gspec(i):
        return pl.BlockSpec((d, tn), lambda r, n: (0, gate0 + i * (d // tn) + n))

    yspec = pl.BlockSpec((tm, MIX_W), lambda r, n: (r, 0))
    return pl.pallas_call(
        _merge_kernel,
        grid=(m // tm, d // tn),
        in_specs=[pl.BlockSpec((tm, d), lambda r, n: (r, 0)), yspec, yspec, yspec, yspec,
                  gspec(0), gspec(1), gspec(2), gspec(3),
                  pl.BlockSpec((4, MIX_W, tn), lambda r, n: (0, 0, n))],
        out_specs=pl.BlockSpec((tm, tn), lambda r, n: (r, n)),
        out_shape=jax.ShapeDtypeStruct((m, d), BF16),
        compiler_params=_params(("parallel", "parallel")),
        name="merge",
    )(h2d, *branches, w_in_bf, w_in_bf, w_in_bf, w_in_bf, w_branch_bf)


def _out_proj_kernel(m_ref, w_ref, x_ref, g_ref, nw_ref, sh_ref, sc_ref, wrh_ref, wrl_ref,
                     x1_ref, h2_ref, aff_ref, affn_ref):
    x1 = x_ref[0] + g_ref[0] * _dot(m_ref[0], w_ref[...])
    x1_ref[0] = x1
    h2 = _prenorm_math(x1, nw_ref[...], sh_ref[0], sc_ref[0])
    h_hi = h2.astype(BF16)
    h2_ref[0] = h_hi
    h_lo = (h2 - h_hi.astype(F32)).astype(BF16)
    logits = _dot(h_hi, wrh_ref[...]) + (_dot(h_lo, wrh_ref[...]) + _dot(h_hi, wrl_ref[...]))
    lane = lax.broadcasted_iota(I32, logits.shape, 1)
    logits = jnp.where(lane < N_EXPERTS, logits, NEG_INF)
    e = jnp.exp(logits - jnp.max(logits, axis=1, keepdims=True))
    aff = e / jnp.sum(e, axis=1, keepdims=True)
    affn_ref[0] = aff
    aff_ref[0] = aff.T[:N_EXPERTS]


def _out_proj(merged, w_out_bf, x, gate, norm_w, shift, scale, wr_hi, wr_lo):
    b, n, d = x.shape
    tm = min(512, n)
    vec = pl.BlockSpec((1, 1, d), lambda i, j: (i, 0, 0))
    tile = pl.BlockSpec((1, tm, d), lambda i, j: (i, j, 0))
    wr_spec = pl.BlockSpec((d, LANES), lambda i, j: (0, 0))
    return pl.pallas_call(
        _out_proj_kernel,
        grid=(b, n // tm),
        in_specs=[tile, pl.BlockSpec((d, d), lambda i, j: (0, 0), pipeline_mode=pl.Buffered(1)), tile, vec,
                  pl.BlockSpec((1, d), lambda i, j: (0, 0)), vec, vec, wr_spec, wr_spec],
        out_specs=[tile, tile,
                   pl.BlockSpec((1, N_EXPERTS, tm), lambda i, j: (i, 0, j)),
                   pl.BlockSpec((1, tm, LANES), lambda i, j: (i, j, 0))],
        out_shape=[jax.ShapeDtypeStruct((b, n, d), F32), jax.ShapeDtypeStruct((b, n, d), BF16),
                   jax.ShapeDtypeStruct((b, N_EXPERTS, n), F32), jax.ShapeDtypeStruct((b, n, LANES), F32)],
        compiler_params=_params(("parallel", "parallel")),
        name="out_proj",
    )(merged, w_out_bf, x, gate, norm_w.reshape(1, d), shift, scale, wr_hi, wr_lo)


def _lane_cumsum(mask_f, ut):
    rows, n = mask_f.shape
    carry = jnp.zeros((rows, 1), F32)
    parts = []
    for j in range(n // LANES):
        c = _dot(mask_f[:, j * LANES:(j + 1) * LANES].astype(BF16), ut) + carry
        parts.append(c)
        carry = c[:, LANES - 1:LANES]
    return jnp.concatenate(parts, axis=1)


def _select_kernel(aff_ref, slot_ref, slott_ref, *, cap):
    aff = aff_ref[0]
    e, n = aff.shape
    bits = lax.bitcast_convert_type(aff, I32)

    def body(_, carry):
        lo, hi = carry
        mid = lo + ((hi - lo) >> 1)
        cnt = jnp.sum(jnp.where(bits >= mid, 1.0, 0.0), axis=1, keepdims=True)
        ge = cnt >= float(cap)
        return jnp.where(ge, mid, lo), jnp.where(ge, hi, mid)

    lo, _ = lax.fori_loop(0, 31, body, (jnp.zeros((e, 1), I32), jnp.full((e, 1), 0x7F800000, I32)))
    ri = lax.broadcasted_iota(I32, (LANES, LANES), 0)
    ci = lax.broadcasted_iota(I32, (LANES, LANES), 1)
    ut = jnp.where(ri <= ci, 1.0, 0.0).astype(BF16)
    gt = bits > lo
    eq = bits == lo
    need = float(cap) - jnp.sum(jnp.where(gt, 1.0, 0.0), axis=1, keepdims=True)
    ceq = _lane_cumsum(jnp.where(eq, 1.0, 0.0), ut)
    sel = gt | (eq & (ceq <= need))
    csel = _lane_cumsum(jnp.where(sel, 1.0, 0.0), ut)
    slot = jnp.where(sel, csel - 1.0, -1.0)
    slot_ref[0] = slot.astype(I32)
    padded = jnp.concatenate([slot, jnp.full((LANES - e, n), -1.0, F32)], axis=0)
    slott_ref[0] = padded.T


def _select(aff, cap):
    b, e, n = aff.shape
    return pl.pallas_call(
        functools.partial(_select_kernel, cap=cap),
        grid=(b,),
        in_specs=[pl.BlockSpec((1, e, n), lambda i: (i, 0, 0))],
        out_specs=[pl.BlockSpec((1, e, n), lambda i: (i, 0, 0)),
                   pl.BlockSpec((1, n, LANES), lambda i: (i, 0, 0))],
        out_shape=[jax.ShapeDtypeStruct((b, e, n), I32), jax.ShapeDtypeStruct((b, n, LANES), F32)],
        compiler_params=_params(("parallel",)),
        name="select",
    )(aff)


def _gather_kernel(cum_ref, slot_ref, h_ref, affn_ref, xs_ref, gate_ref, acc, gacc, *, cap, tk, win):
    b = pl.program_id(0)
    e = pl.program_id(1)
    n = slot_ref.shape[3]
    nt = n // tk
    per_tile = tk // SEL_TILE
    base = (b * N_EXPERTS + e) * (n // SEL_TILE + 1)
    acc[...] = jnp.zeros_like(acc)
    gacc[...] = jnp.zeros_like(gacc)
    for k in range(nt):
        toks = slice(k * tk, (k + 1) * tk)
        c_lo = cum_ref[base + k * per_tile]
        c_hi = cum_ref[base + (k + 1) * per_tile]
        slot = slot_ref[0, 0, :, toks]
        hi, mid, lo = _split3(affn_ref[0, toks, :])

        def window(wi, carry):
            r0 = pl.multiple_of(wi * win, win)
            rows = r0 + lax.broadcasted_iota(I32, (win, tk), 0)
            onehot = jnp.where(rows == slot, 1.0, 0.0).astype(BF16)
            acc[pl.ds(r0, win), :] += _dot(onehot, h_ref[0, toks, :])
            gacc[pl.ds(r0, win), :] += _dot(onehot, hi) + _dot(onehot, mid) + _dot(onehot, lo)
            return carry

        lax.fori_loop(c_lo // win, (c_hi + win - 1) // win, window, 0)

    xs_ref[0, 0] = acc[...].astype(BF16)
    lane = lax.broadcasted_iota(I32, (cap, LANES), 1)
    col = jnp.sum(jnp.where(lane == e, gacc[...], 0.0), axis=1, keepdims=True)
    gate_ref[0, 0] = jnp.broadcast_to(col, (cap, LANES))


def _gather(cum, slot, h2, affn, cap):
    b, n, d = h2.shape
    tk = min(512, n)
    win = min(LANES, cap)
    grid_spec = pltpu.PrefetchScalarGridSpec(
        num_scalar_prefetch=1,
        grid=(b, N_EXPERTS),
        in_specs=[
            pl.BlockSpec((1, 1, 1, n), lambda i, e, c: (i, e, 0, 0)),
            pl.BlockSpec((1, n, d), lambda i, e, c: (i, 0, 0), pipeline_mode=pl.Buffered(1)),
            pl.BlockSpec((1, n, LANES), lambda i, e, c: (i, 0, 0)),
        ],
        out_specs=[pl.BlockSpec((1, 1, cap, d), lambda i, e, c: (e, i, 0, 0)),
                   pl.BlockSpec((1, 1, cap, LANES), lambda i, e, c: (e, i, 0, 0))],
        scratch_shapes=[pltpu.VMEM((cap, d), F32), pltpu.VMEM((cap, LANES), F32)],
    )
    return pl.pallas_call(
        functools.partial(_gather_kernel, cap=cap, tk=tk, win=win),
        grid_spec=grid_spec,
        out_shape=[jax.ShapeDtypeStruct((N_EXPERTS, b, cap, d), BF16),
                   jax.ShapeDtypeStruct((N_EXPERTS, b, cap, LANES), F32)],
        compiler_params=_params(("parallel", "arbitrary")),
        name="ec_gather",
    )(cum, slot.reshape(b, N_EXPERTS, 1, n), h2, affn)


def _expert_kernel(xs_ref, gate_ref, wg_ref, wu_ref, wd_ref, ys_ref, acc):
    f = pl.program_id(2)
    x = xs_ref[0]
    a = _dot(x, wg_ref[0, 0].astype(BF16))
    u = _dot(x, wu_ref[0, 0].astype(BF16))
    hdn = (a * jax.nn.sigmoid(a) * u).astype(BF16)
    y = _dot(hdn, wd_ref[0, 0].astype(BF16))

    @pl.when(f == 0)
    def _():
        acc[...] = y

    @pl.when(f > 0)
    def _():
        acc[...] += y

    @pl.when(f == pl.num_programs(2) - 1)
    def _():
        g = gate_ref[0]
        gfull = jnp.concatenate([g] * (acc.shape[1] // LANES), axis=1)
        ys_ref[0] = (acc[...] * gfull).astype(BF16)


def _experts(xs, gate, wg, wu, wd, layer):
    e, m, d = xs.shape
    ff = wg.shape[3]
    tm = min(1024, m)
    tf = 256
    return pl.pallas_call(
        _expert_kernel,
        grid=(e, m // tm, ff // tf),
        in_specs=[
            pl.BlockSpec((1, tm, d), lambda ei, i, f: (ei, i, 0)),
            pl.BlockSpec((1, tm, LANES), lambda ei, i, f: (ei, i, 0)),
            pl.BlockSpec((1, 1, d, tf), lambda ei, i, f: (layer, ei, 0, f)),
            pl.BlockSpec((1, 1, d, tf), lambda ei, i, f: (layer, ei, 0, f)),
            pl.BlockSpec((1, 1, tf, d), lambda ei, i, f: (layer, ei, f, 0)),
        ],
        out_specs=pl.BlockSpec((1, tm, d), lambda ei, i, f: (ei, i, 0)),
        out_shape=jax.ShapeDtypeStruct((e, m, d), BF16),
        scratch_shapes=[pltpu.VMEM((tm, d), F32)],
        compiler_params=_params(("parallel", "parallel", "arbitrary")),
        name="experts",
    )(xs, gate, wg, wu, wd)


def _scatter_kernel(cum_ref, slott_ref, ys_ref, x_ref, g_ref, o_ref, *, cap, tq, win):
    b = pl.program_id(0)
    q = pl.program_id(1)
    e = pl.program_id(2)
    n_tiles = pl.num_programs(1) * (tq // SEL_TILE)
    base = (b * N_EXPERTS + e) * (n_tiles + 1) + q * (tq // SEL_TILE)

    @pl.when(e == 0)
    def _():
        o_ref[...] = jnp.zeros_like(o_ref)

    lane = lax.broadcasted_iota(I32, (SEL_TILE, LANES), 1)
    cols = lax.broadcasted_iota(I32, (SEL_TILE, win), 1).astype(F32)
    for s in range(tq // SEL_TILE):
        toks = slice(s * SEL_TILE, (s + 1) * SEL_TILE)
        c_lo = cum_ref[base + s]
        c_hi = cum_ref[base + s + 1]
        col = jnp.sum(jnp.where(lane == e, slott_ref[0, toks, :], 0.0), axis=1, keepdims=True)

        def window(wi, carry):
            r0 = pl.multiple_of(wi * win, win)
            onehot = jnp.where(cols + r0.astype(F32) == col, 1.0, 0.0).astype(BF16)
            o_ref[0, toks, :] += _dot(onehot, ys_ref[0, 0, pl.ds(r0, win), :])
            return carry

        lax.fori_loop(c_lo // win, (c_hi + win - 1) // win, window, 0)

    @pl.when(e == pl.num_programs(2) - 1)
    def _():
        o_ref[0] = x_ref[0] + g_ref[0] * o_ref[0]


def _scatter(cum, slott, ys, x1, gate, cap):
    b, n, d = x1.shape
    tq = min(1024, n)
    win = min(SEL_TILE, cap)
    grid_spec = pltpu.PrefetchScalarGridSpec(
        num_scalar_prefetch=1,
        grid=(b, n // tq, N_EXPERTS),
        in_specs=[
            pl.BlockSpec((1, tq, LANES), lambda i, t, e, c: (i, t, 0)),
            pl.BlockSpec((1, 1, cap, d), lambda i, t, e, c: (e, i, 0, 0)),
            pl.BlockSpec((1, tq, d), lambda i, t, e, c: (i, t, 0)),
            pl.BlockSpec((1, 1, d), lambda i, t, e, c: (i, 0, 0)),
        ],
        out_specs=pl.BlockSpec((1, tq, d), lambda i, t, e, c: (i, t, 0)),
    )
    return pl.pallas_call(
        functools.partial(_scatter_kernel, cap=cap, tq=tq, win=win),
        grid_spec=grid_spec,
        out_shape=jax.ShapeDtypeStruct((b, n, d), F32),
        compiler_params=_params(("parallel", "parallel", "arbitrary")),
        name="ec_scatter",
    )(cum, slott, ys, x1, gate)


def _mod_chunks(mod_rows):
    b = mod_rows.shape[0]
    return [mod_rows[:, i * D_MODEL:(i + 1) * D_MODEL].reshape(b, 1, D_MODEL) for i in range(N_MOD)]


def _token_mixer(h, zmix, att, lw, dft):
    b, n, d = h.shape
    y_conv, y_pool = _stencil(zmix, lw["conv_w"], lw["pool_w"], lw["pool_scale"])
    y_f = _fourier(zmix, *dft)
    branches = [y.reshape(b * n, MIX_W) for y in (y_conv, att, y_f, y_pool)]
    merged = _merge(h.reshape(b * n, d), branches, lw["w_in"], lw["w_branch"])
    return merged.reshape(b, n, d)


def _ffn(x1, h2, aff, affn, g2, lw):
    b, n, d = x1.shape
    cap = EC_CAPACITY * n // N_EXPERTS
    slot, slott = _select(aff, cap)
    per_tile = jnp.sum((slot >= 0).reshape(b, N_EXPERTS, n // SEL_TILE, SEL_TILE), axis=-1, dtype=I32)
    cum = jnp.concatenate([jnp.zeros((b, N_EXPERTS, 1), I32), jnp.cumsum(per_tile, axis=-1)], axis=-1).reshape(-1)
    xs, gate = _gather(cum, slot, h2, affn, cap)
    ys = _experts(xs.reshape(N_EXPERTS, b * cap, d), gate.reshape(N_EXPERTS, b * cap, LANES),
                  lw["w_gate"], lw["w_up"], lw["w_down"], lw["layer"])
    return _scatter(cum, slott, ys.reshape(N_EXPERTS, b, cap, d), x1, g2, cap)


def kernel(x, c, ctx, c_ctx, w_ada, b_ada, norm1_w, norm2_w, w_in, conv_w, q_norm_w, k_norm_w,
           na_rpb, pool_w, pool_scale, w_branch, w_out, w_router, w_exp_gate, w_exp_up, w_exp_down):
    bsz, n, d = x.shape
    l_ctx = ctx.shape[1]
    depth = w_ada.shape[0]
    n_rows = n // GRID_W

    c16 = jnp.zeros((16, d), F32).at[:bsz].set(c).at[bsz].set(c_ctx)
    mod_all = _ada(c16, w_ada, b_ada)

    hd = np.arange(MIX_W) // HEAD_DIM
    bd = jnp.asarray(hd[:, None] == hd[None, :], dtype=BF16)
    dft_lat = _dft_mats(n, 1.0) + _dft_mats(FOURIER_GC, (n * FOURIER_GC) ** -0.5)
    dft_ctx = _dft_mats(l_ctx, 1.0) + _dft_mats(FOURIER_GC, (l_ctx * FOURIER_GC) ** -0.5)

    for l in range(depth):
        last = l == depth - 1
        lw = {
            "w_in": w_in[l].astype(BF16), "conv_w": conv_w[l], "pool_w": pool_w[l],
            "pool_scale": pool_scale[l], "w_branch": w_branch[l].astype(BF16),
            "w_gate": w_exp_gate, "w_up": w_exp_up, "w_down": w_exp_down, "layer": l,
        }
        w_out_bf = w_out[l].astype(BF16)
        wr_pad = jnp.zeros((d, LANES), F32).at[:, :N_EXPERTS].set(w_router[l])
        wr_hi = wr_pad.astype(BF16)
        wr_lo = (wr_pad - wr_hi.astype(F32)).astype(BF16)
        q_w = jnp.tile(q_norm_w[l], N_HEADS).reshape(1, MIX_W)
        k_w = jnp.tile(k_norm_w[l], N_HEADS).reshape(1, MIX_W)
        tb = _na_bias_table(na_rpb[l], n_rows)
        sh1, sc1, g1, sh2, sc2, g2 = _mod_chunks(mod_all[l, :bsz])
        mc = _mod_chunks(jnp.broadcast_to(mod_all[l, bsz:bsz + 1], (bsz, N_MOD * d)))

        hc = _prenorm(ctx, norm1_w[l], mc[0], mc[1])
        zc_mix = _in_proj(hc.reshape(bsz * l_ctx, d), lw["w_in"], 0, MIX_COLS).reshape(bsz, l_ctx, MIX_COLS)

        h = _prenorm(x, norm1_w[l], sh1, sc1)
        zmix = _in_proj(h.reshape(bsz * n, d), lw["w_in"], 0, MIX_COLS).reshape(bsz, n, MIX_COLS)
        att = _neighbourhood_attention(zmix, zc_mix, q_w, k_w, bd, tb)
        merged = _token_mixer(h, zmix, att, lw, dft_lat)
        x1, h2, aff, affn = _out_proj(merged, w_out_bf, x, g1, norm2_w[l], sh2, sc2, wr_hi, wr_lo)
        x = _ffn(x1, h2, aff, affn, g2, lw)

        if not last:
            att_c = _context_attention(zc_mix, q_w, k_w, bd)
            merged_c = _token_mixer(hc, zc_mix, att_c, lw, dft_ctx)
            c1, hc2, aff_c, affn_c = _out_proj(merged_c, w_out_bf, ctx, mc[2], norm2_w[l], mc[3], mc[4],
                                                   wr_hi, wr_lo)
            ctx = _ffn(c1, hc2, aff_c, affn_c, mc[5], lw)
    return x
```

```python
import functools

import numpy as np
import jax
import jax.numpy as jnp
from jax import lax
from jax.experimental import pallas as pl
from jax.experimental.pallas import tpu as pltpu

F32 = jnp.float32
BF16 = jnp.bfloat16
I32 = jnp.int32
HIGHEST = lax.Precision.HIGHEST

D_MODEL = 2048
GRID_W = 64
MIX_W = D_MODEL // 4
N_HEADS = 8
HEAD_DIM = MIX_W // N_HEADS
WIN_H = 8
WIN_W = 16
ATT_SCALE = HEAD_DIM ** -0.5
POOL_WINDOWS = (2, 4, 8, 16)
POOL_GC = MIX_W // len(POOL_WINDOWS)
FOURIER_GC = MIX_W // 4
N_EXPERTS = 16
EC_CAPACITY = 2
N_MOD = 6
EPS = 1e-6
NEG_INF = -1e30
MIX_COLS = 8 * MIX_W
LANES = 128
HALO = 16
SEL_TILE = 256
ROW_ALIGN = 16
EC_WINDOW = 64
VMEM_LIMIT = 56 * 1024 * 1024

COL_XA, COL_GB, COL_GC, COL_Q, COL_K, COL_V, COL_F, COL_P = range(8)


def _params(sem, vmem=VMEM_LIMIT):
    return pltpu.CompilerParams(dimension_semantics=sem, vmem_limit_bytes=vmem)


def _dot(a, b):
    return jnp.dot(a, b, preferred_element_type=F32)


def _dot_nt(a, b):
    return lax.dot_general(a, b, (((1,), (1,)), ((), ())), preferred_element_type=F32)


def _split3(a):
    hi = a.astype(BF16)
    r1 = a - hi.astype(F32)
    mid = r1.astype(BF16)
    lo = (r1 - mid.astype(F32)).astype(BF16)
    return hi, mid, lo


def _ada_kernel(c_ref, w_ref, b_ref, o_ref):
    c = c_ref[...]
    s = c * jax.nn.sigmoid(c)
    o_ref[0] = jnp.dot(s, w_ref[0], precision=HIGHEST, preferred_element_type=F32) + b_ref[0]


def _ada(c16, w_ada, b_ada):
    depth, d, ncol = w_ada.shape
    tn = 1024
    return pl.pallas_call(
        _ada_kernel,
        grid=(depth, ncol // tn),
        in_specs=[
            pl.BlockSpec((16, d), lambda l, j: (0, 0)),
            pl.BlockSpec((1, d, tn), lambda l, j: (l, 0, j)),
            pl.BlockSpec((1, 1, tn), lambda l, j: (l, 0, j)),
        ],
        out_specs=pl.BlockSpec((1, 16, tn), lambda l, j: (l, 0, j)),
        out_shape=jax.ShapeDtypeStruct((depth, 16, ncol), F32),
        compiler_params=_params(("parallel", "parallel")),
        name="ada",
    )(c16, w_ada, b_ada.reshape(depth, 1, ncol))


def _prenorm_math(x, w, shift, scale):
    ms = jnp.mean(x * x, axis=-1, keepdims=True)
    y = x * lax.rsqrt(ms + EPS) * w
    return y * (1.0 + scale) + shift


def _prenorm_kernel(x_ref, w_ref, sh_ref, sc_ref, o_ref):
    o_ref[0] = _prenorm_math(x_ref[0], w_ref[...], sh_ref[0], sc_ref[0]).astype(BF16)


def _prenorm(x, w, shift, scale):
    b, n, d = x.shape
    tm = min(512, n)
    return pl.pallas_call(
        _prenorm_kernel,
        grid=(b, n // tm),
        in_specs=[
            pl.BlockSpec((1, tm, d), lambda i, j: (i, j, 0)),
            pl.BlockSpec((1, d), lambda i, j: (0, 0)),
            pl.BlockSpec((1, 1, d), lambda i, j: (i, 0, 0)),
            pl.BlockSpec((1, 1, d), lambda i, j: (i, 0, 0)),
        ],
        out_specs=pl.BlockSpec((1, tm, d), lambda i, j: (i, j, 0)),
        out_shape=jax.ShapeDtypeStruct((b, n, d), BF16),
        compiler_params=_params(("parallel", "parallel")),
        name="prenorm",
    )(x, w.reshape(1, d), shift, scale)


def _mm_kernel(a_ref, w_ref, o_ref):
    o_ref[...] = _dot(a_ref[...], w_ref[...]).astype(o_ref.dtype)


def _in_proj(h2d, w, col0, ncols):
    m, k = h2d.shape
    tm = min(1024, m)
    tn = 1024
    c0 = col0 // tn
    return pl.pallas_call(
        _mm_kernel,
        grid=(m // tm, ncols // tn),
        in_specs=[
            pl.BlockSpec((tm, k), lambda i, j: (i, 0)),
            pl.BlockSpec((k, tn), lambda i, j: (0, c0 + j)),
        ],
        out_specs=pl.BlockSpec((tm, tn), lambda i, j: (i, j)),
        out_shape=jax.ShapeDtypeStruct((m, ncols), BF16),
        compiler_params=_params(("parallel", "parallel")),
        name="in_proj",
    )(h2d, w)


def _stencil_kernel(xa_ref, gb_ref, gc_ref, zp_ref, xa_p, xa_n, gc_p, gc_n, zp_p, zp_n,
                    cw_ref, pw_ref, ps_ref, oc_ref, op_ref, *, tm, n_seq):
    i = pl.program_id(1)
    nt = pl.num_programs(1)
    rows = tm + 2 * HALO

    def ext(main_ref, prev_ref, next_ref):
        main = main_ref[0].astype(F32)
        prev = jnp.where(i > 0, prev_ref[0].astype(F32), 0.0)
        nxt = jnp.where(i < nt - 1, next_ref[0].astype(F32), 0.0)
        return jnp.concatenate([prev, main, nxt], axis=0)

    def shifted(a, d):
        if d == 0:
            return a[HALO:HALO + tm]
        return pltpu.roll(a, (-d) % rows, axis=0)[HALO:HALO + tm]

    u = ext(gc_ref, gc_p, gc_n) * ext(xa_ref, xa_p, xa_n)
    cw = cw_ref[...]
    y = cw[0:1] * shifted(u, -1) + cw[1:2] * shifted(u, 0) + cw[2:3] * shifted(u, 1)
    oc_ref[0] = (gb_ref[0].astype(F32) * y).astype(BF16)

    t = (i * tm + lax.broadcasted_iota(I32, (tm, 1), 0)).astype(F32)
    zp = ext(zp_ref, zp_p, zp_n)
    for g, win in enumerate(POOL_WINDOWS):
        ug = zp[:, g * POOL_GC:(g + 1) * POOL_GC]
        acc = ug + pltpu.roll(ug, 1, axis=0)
        half = 1
        while 2 * half < win:
            acc = pltpu.roll(acc, rows - half, axis=0) + pltpu.roll(acc, half, axis=0)
            half *= 2
        wsum = acc[HALO:HALO + tm]
        lo = jnp.maximum(t - (win // 2), 0.0)
        hi = jnp.minimum(t + (win - win // 2 - 1), float(n_seq - 1))
        pooled = wsum / (hi - lo + 1.0) - ug[HALO:HALO + tm]
        yg = _dot(pooled.astype(BF16), pw_ref[g])
        op_ref[0, :, g * POOL_GC:(g + 1) * POOL_GC] = (
            yg * ps_ref[:, g * POOL_GC:(g + 1) * POOL_GC]).astype(BF16)


def _stencil(zmix, conv_w, pool_w, pool_scale):
    b, n, _ = zmix.shape
    tm = min(512, n)
    hb = tm // HALO
    last = n // HALO - 1

    def main(col):
        return pl.BlockSpec((1, tm, MIX_W), lambda bi, i: (bi, i, col))

    def prev(col):
        return pl.BlockSpec((1, HALO, MIX_W), lambda bi, i: (bi, jnp.maximum(i * hb - 1, 0), col))

    def nxt(col):
        return pl.BlockSpec((1, HALO, MIX_W), lambda bi, i: (bi, jnp.minimum((i + 1) * hb, last), col))

    out_spec = pl.BlockSpec((1, tm, MIX_W), lambda bi, i: (bi, i, 0))
    return pl.pallas_call(
        functools.partial(_stencil_kernel, tm=tm, n_seq=n),
        grid=(b, n // tm),
        in_specs=[main(COL_XA), main(COL_GB), main(COL_GC), main(COL_P),
                  prev(COL_XA), nxt(COL_XA), prev(COL_GC), nxt(COL_GC), prev(COL_P), nxt(COL_P),
                  pl.BlockSpec((3, MIX_W), lambda bi, i: (0, 0)),
                  pl.BlockSpec((len(POOL_WINDOWS), POOL_GC, POOL_GC), lambda bi, i: (0, 0, 0)),
                  pl.BlockSpec((1, MIX_W), lambda bi, i: (0, 0))],
        out_specs=[out_spec, out_spec],
        out_shape=[jax.ShapeDtypeStruct((b, n, MIX_W), BF16)] * 2,
        compiler_params=_params(("parallel", "parallel")),
        name="stencil",
    )(zmix, zmix, zmix, zmix, zmix, zmix, zmix, zmix, zmix, zmix,
      conv_w, pool_w.astype(BF16), pool_scale.reshape(1, MIX_W))


def _fourier_kernel(cn_ref, sn_ref, u_ref, cc_ref, sc_ref, o_ref):
    u = u_ref[0]
    p = _dot(cn_ref[...], u).astype(BF16)
    q = _dot(sn_ref[...], u).astype(BF16)
    for g in range(MIX_W // FOURIER_GC):
        sl = slice(g * FOURIER_GC, (g + 1) * FOURIER_GC)
        o_ref[0, :, sl] = (_dot(p[:, sl], cc_ref[...]) - _dot(q[:, sl], sc_ref[...])).astype(BF16)


def _dft_mats(n, scale):
    k = jnp.arange(n, dtype=I32)
    r = (k[:, None] * k[None, :]) % n
    ang = r.astype(F32) * np.float32(2.0 * np.pi / n)
    return (jnp.cos(ang) * scale).astype(BF16), (jnp.sin(ang) * scale).astype(BF16)


def _fourier(zmix, cn, sn, cc, sc):
    b, n, _ = zmix.shape
    tk = min(512, n)
    return pl.pallas_call(
        _fourier_kernel,
        grid=(n // tk, b),
        in_specs=[
            pl.BlockSpec((tk, n), lambda k, bi: (k, 0)),
            pl.BlockSpec((tk, n), lambda k, bi: (k, 0)),
            pl.BlockSpec((1, n, MIX_W), lambda k, bi: (bi, 0, COL_F)),
            pl.BlockSpec((FOURIER_GC, FOURIER_GC), lambda k, bi: (0, 0)),
            pl.BlockSpec((FOURIER_GC, FOURIER_GC), lambda k, bi: (0, 0)),
        ],
        out_specs=pl.BlockSpec((1, tk, MIX_W), lambda k, bi: (bi, k, 0)),
        out_shape=jax.ShapeDtypeStruct((b, n, MIX_W), BF16),
        compiler_params=_params(("parallel", "parallel")),
        name="fourier",
    )(cn, sn, zmix, cc, sc)


def _head_rms(x, w, bd):
    x2 = x * x
    hi = x2.astype(BF16)
    lo = (x2 - hi.astype(F32)).astype(BF16)
    ms = (_dot(hi, bd) + _dot(lo, bd)) * (1.0 / HEAD_DIM)
    return x * lax.rsqrt(ms + EPS) * w


def _pair_stack(q2):
    lane = lax.broadcasted_iota(I32, q2.shape, 1)
    zero = jnp.zeros_like(q2)
    return jnp.concatenate([jnp.where(lane < HEAD_DIM, q2, zero),
                            jnp.where(lane >= HEAD_DIM, q2, zero)], axis=0)


def _pair_unstack(o2, m):
    lane = lax.broadcasted_iota(I32, (m, LANES), 1)
    return jnp.where(lane < HEAD_DIM, o2[:m], o2[m:])


def _na_kernel(q_ref, k_ref, v_ref, kc_ref, vc_ref, qw_ref, kw_ref, bd_ref, tb_ref, o_ref,
               kn_scr, kcn_scr, qn_scr, *, rt, n_rows):
    rb = pl.program_id(1)
    n_seq = n_rows * GRID_W
    chunk = 512

    @pl.when(rb == 0)
    def _():
        def body(c, carry):
            sl = pl.ds(pl.multiple_of(c * chunk, chunk), chunk)
            kn_scr[sl, :] = _head_rms(k_ref[0, sl, :].astype(F32), kw_ref[...], bd_ref[...]).astype(BF16)
            return carry
        lax.fori_loop(0, n_seq // chunk, body, 0)
        kcn_scr[...] = _head_rms(kc_ref[0].astype(F32), kw_ref[...], bd_ref[...]).astype(BF16)

    qn_scr[...] = (_head_rms(q_ref[0].astype(F32), qw_ref[...], bd_ref[...]) * ATT_SCALE).astype(BF16)

    kh = min(WIN_H, n_rows)
    nloc = kh * GRID_W

    def row_body(j, carry):
        r = rb * rt + j
        rs = jnp.clip(r - kh // 2, 0, n_rows - kh)
        dr0 = rs - r + (WIN_H - 1)
        q_rows = pl.ds(pl.multiple_of(j * GRID_W, GRID_W), GRID_W)
        k_rows = pl.ds(pl.multiple_of(rs * GRID_W, GRID_W), nloc)
        for p in range(N_HEADS // 2):
            cols = slice(p * LANES, (p + 1) * LANES)
            qs = _pair_stack(qn_scr[q_rows, cols])
            s_loc = _dot_nt(qs, kn_scr[k_rows, cols]) + tb_ref[dr0, p]
            s_ctx = _dot_nt(qs, kcn_scr[:, cols])
            m = jnp.maximum(jnp.max(s_loc, axis=1, keepdims=True), jnp.max(s_ctx, axis=1, keepdims=True))
            p_loc = jnp.exp(s_loc - m)
            p_ctx = jnp.exp(s_ctx - m)
            denom = jnp.sum(p_loc, axis=1, keepdims=True) + jnp.sum(p_ctx, axis=1, keepdims=True)
            o2 = _dot(p_loc.astype(BF16), v_ref[0, k_rows, cols]) + _dot(p_ctx.astype(BF16), vc_ref[0, :, cols])
            o2 = o2 / denom
            o_ref[0, q_rows, cols] = _pair_unstack(o2, GRID_W).astype(BF16)
        return carry

    lax.fori_loop(0, rt, row_body, 0)


def _na_bias_table(rpb, n_rows):
    kh = min(WIN_H, n_rows)
    c = np.arange(GRID_W)
    cs = np.clip(c - WIN_W // 2, 0, GRID_W - WIN_W)
    kc = np.arange(GRID_W)
    ok = (kc[None, :] >= cs[:, None]) & (kc[None, :] < cs[:, None] + WIN_W)
    dc = np.clip(kc[None, :] - c[:, None] + (WIN_W - 1), 0, 2 * WIN_W - 2)
    dr = np.arange(WIN_H)[:, None] + np.arange(kh)[None, :]
    dr = np.clip(dr, 0, 2 * WIN_H - 2)
    t = rpb[:, dr][:, :, :, dc]
    t = jnp.where(ok[None, None, None], t.astype(F32), NEG_INF)
    t = jnp.transpose(t, (1, 0, 3, 2, 4))
    return t.reshape(WIN_H, N_HEADS // 2, 2 * GRID_W, kh * GRID_W)


def _neighbourhood_attention(zmix, zc_mix, q_w, k_w, bd, tb):
    b, n, _ = zmix.shape
    l = zc_mix.shape[1]
    n_rows = n // GRID_W
    rt = 8
    nloc = min(WIN_H, n_rows) * GRID_W
    return pl.pallas_call(
        functools.partial(_na_kernel, rt=rt, n_rows=n_rows),
        grid=(b, n_rows // rt),
        in_specs=[
            pl.BlockSpec((1, rt * GRID_W, MIX_W), lambda bi, r: (bi, r, COL_Q)),
            pl.BlockSpec((1, n, MIX_W), lambda bi, r: (bi, 0, COL_K)),
            pl.BlockSpec((1, n, MIX_W), lambda bi, r: (bi, 0, COL_V)),
            pl.BlockSpec((1, l, MIX_W), lambda bi, r: (bi, 0, COL_K)),
            pl.BlockSpec((1, l, MIX_W), lambda bi, r: (bi, 0, COL_V)),
            pl.BlockSpec((1, MIX_W), lambda bi, r: (0, 0)),
            pl.BlockSpec((1, MIX_W), lambda bi, r: (0, 0)),
            pl.BlockSpec((MIX_W, MIX_W), lambda bi, r: (0, 0)),
            pl.BlockSpec((WIN_H, N_HEADS // 2, 2 * GRID_W, nloc), lambda bi, r: (0, 0, 0, 0),
                         pipeline_mode=pl.Buffered(1)),
        ],
        out_specs=pl.BlockSpec((1, rt * GRID_W, MIX_W), lambda bi, r: (bi, r, 0)),
        out_shape=jax.ShapeDtypeStruct((b, n, MIX_W), BF16),
        scratch_shapes=[pltpu.VMEM((n, MIX_W), BF16), pltpu.VMEM((l, MIX_W), BF16),
                        pltpu.VMEM((rt * GRID_W, MIX_W), BF16)],
        compiler_params=_params(("parallel", "arbitrary")),
        name="nbr_attention",
    )(zmix, zmix, zmix, zc_mix, zc_mix, q_w, k_w, bd, tb)


def _ctx_attn_kernel(q_ref, k_ref, v_ref, qw_ref, kw_ref, bd_ref, o_ref):
    l = q_ref.shape[1]
    qn = (_head_rms(q_ref[0].astype(F32), qw_ref[...], bd_ref[...]) * ATT_SCALE).astype(BF16)
    kn = _head_rms(k_ref[0].astype(F32), kw_ref[...], bd_ref[...]).astype(BF16)
    for p in range(N_HEADS // 2):
        cols = slice(p * LANES, (p + 1) * LANES)
        qs = _pair_stack(qn[:, cols])
        s = _dot_nt(qs, kn[:, cols])
        m = jnp.max(s, axis=1, keepdims=True)
        e = jnp.exp(s - m)
        o2 = _dot(e.astype(BF16), v_ref[0, :, cols]) / jnp.sum(e, axis=1, keepdims=True)
        o_ref[0, :, cols] = _pair_unstack(o2, l).astype(BF16)


def _context_attention(zc_mix, q_w, k_w, bd):
    b, l, _ = zc_mix.shape
    vec = pl.BlockSpec((1, MIX_W), lambda bi: (0, 0))
    return pl.pallas_call(
        _ctx_attn_kernel,
        grid=(b,),
        in_specs=[
            pl.BlockSpec((1, l, MIX_W), lambda bi: (bi, 0, COL_Q)),
            pl.BlockSpec((1, l, MIX_W), lambda bi: (bi, 0, COL_K)),
            pl.BlockSpec((1, l, MIX_W), lambda bi: (bi, 0, COL_V)),
            vec, vec,
            pl.BlockSpec((MIX_W, MIX_W), lambda bi: (0, 0)),
        ],
        out_specs=pl.BlockSpec((1, l, MIX_W), lambda bi: (bi, 0, 0)),
        out_shape=jax.ShapeDtypeStruct((b, l, MIX_W), BF16),
        compiler_params=_params(("parallel",)),
        name="ctx_attention",
    )(zc_mix, zc_mix, zc_mix, q_w, k_w, bd)


def _merge_kernel(h_ref, y0, y1, y2, y3, g0, g1, g2, g3, wb_ref, o_ref):
    h = h_ref[...]
    acc = None
    for i, (y, g) in enumerate(((y0, g0), (y1, g1), (y2, g2), (y3, g3))):
        gate = jax.nn.sigmoid(_dot(h, g[...]))
        term = gate * _dot(y[...], wb_ref[i])
        acc = term if acc is None else acc + term
    o_ref[...] = acc.astype(BF16)


def _merge(h2d, branches, w_in_bf, w_branch_bf):
    m, d = h2d.shape
    tm = min(1024, m)
    tn = 256
    gate0 = MIX_COLS // tn

    def gspec(i):
        return pl.BlockSpec((d, tn), lambda r, n: (0, gate0 + i * (d // tn) + n))

    yspec = pl.BlockSpec((tm, MIX_W), lambda r, n: (r, 0))
    return pl.pallas_call(
        _merge_kernel,
        grid=(m // tm, d // tn),
        in_specs=[pl.BlockSpec((tm, d), lambda r, n: (r, 0)), yspec, yspec, yspec, yspec,
                  gspec(0), gspec(1), gspec(2), gspec(3),
                  pl.BlockSpec((4, MIX_W, tn), lambda r, n: (0, 0, n))],
        out_specs=pl.BlockSpec((tm, tn), lambda r, n: (r, n)),
        out_shape=jax.ShapeDtypeStruct((m, d), BF16),
        compiler_params=_params(("parallel", "parallel")),
        name="merge",
    )(h2d, *branches, w_in_bf, w_in_bf, w_in_bf, w_in_bf, w_branch_bf)


def _out_proj_kernel(m_ref, w_ref, x_ref, g_ref, nw_ref, sh_ref, sc_ref, wrh_ref, wrl_ref,
                     x1_ref, h2_ref, aff_ref, affn_ref):
    x1 = x_ref[0] + g_ref[0] * _dot(m_ref[0], w_ref[...])
    x1_ref[0] = x1
    h2 = _prenorm_math(x1, nw_ref[...], sh_ref[0], sc_ref[0])
    h_hi = h2.astype(BF16)
    h2_ref[0] = h_hi
    h_lo = (h2 - h_hi.astype(F32)).astype(BF16)
    logits = _dot(h_hi, wrh_ref[...]) + (_dot(h_lo, wrh_ref[...]) + _dot(h_hi, wrl_ref[...]))
    lane = lax.broadcasted_iota(I32, logits.shape, 1)
    logits = jnp.where(lane < N_EXPERTS, logits, NEG_INF)
    e = jnp.exp(logits - jnp.max(logits, axis=1, keepdims=True))
    aff = e / jnp.sum(e, axis=1, keepdims=True)
    aff_ref[0] = aff.T[:N_EXPERTS]
    hi = aff.astype(BF16).astype(F32)
    r1 = aff - hi
    mid = r1.astype(BF16).astype(F32)
    lo = (r1 - mid).astype(BF16).astype(F32)
    packed = hi + pltpu.roll(mid, N_EXPERTS, axis=1) + pltpu.roll(lo, 2 * N_EXPERTS, axis=1)
    affn_ref[0] = packed.astype(BF16)


def _out_proj(merged, w_out_bf, x, gate, norm_w, shift, scale, wr_hi, wr_lo):
    b, n, d = x.shape
    tm = min(512, n)
    vec = pl.BlockSpec((1, 1, d), lambda i, j: (i, 0, 0))
    tile = pl.BlockSpec((1, tm, d), lambda i, j: (i, j, 0))
    wr_spec = pl.BlockSpec((d, LANES), lambda i, j: (0, 0))
    return pl.pallas_call(
        _out_proj_kernel,
        grid=(b, n // tm),
        in_specs=[tile, pl.BlockSpec((d, d), lambda i, j: (0, 0), pipeline_mode=pl.Buffered(1)), tile, vec,
                  pl.BlockSpec((1, d), lambda i, j: (0, 0)), vec, vec, wr_spec, wr_spec],
        out_specs=[tile, tile,
                   pl.BlockSpec((1, N_EXPERTS, tm), lambda i, j: (i, 0, j)),
                   pl.BlockSpec((1, tm, LANES), lambda i, j: (i, j, 0))],
        out_shape=[jax.ShapeDtypeStruct((b, n, d), F32), jax.ShapeDtypeStruct((b, n, d), BF16),
                   jax.ShapeDtypeStruct((b, N_EXPERTS, n), F32), jax.ShapeDtypeStruct((b, n, LANES), BF16)],
        compiler_params=_params(("parallel", "parallel")),
        name="out_proj",
    )(merged, w_out_bf, x, gate, norm_w.reshape(1, d), shift, scale, wr_hi, wr_lo)


def _lane_cumsum(mask_f, ut):
    rows, n = mask_f.shape
    carry = jnp.zeros((rows, 1), F32)
    parts = []
    for j in range(n // LANES):
        c = _dot(mask_f[:, j * LANES:(j + 1) * LANES].astype(BF16), ut) + carry
        parts.append(c)
        carry = c[:, LANES - 1:LANES]
    return jnp.concatenate(parts, axis=1)


def _select_kernel(aff_ref, slot_ref, slott_ref, *, cap):
    aff = aff_ref[0]
    e, n = aff.shape
    bits = lax.bitcast_convert_type(aff, I32)

    def body(_, carry):
        lo, hi = carry
        mid = lo + ((hi - lo) >> 1)
        cnt = jnp.sum(jnp.where(bits >= mid, 1.0, 0.0), axis=1, keepdims=True)
        ge = cnt >= float(cap)
        return jnp.where(ge, mid, lo), jnp.where(ge, hi, mid)

    lo, _ = lax.fori_loop(0, 31, body, (jnp.zeros((e, 1), I32), jnp.full((e, 1), 0x7F800000, I32)))
    ri = lax.broadcasted_iota(I32, (LANES, LANES), 0)
    ci = lax.broadcasted_iota(I32, (LANES, LANES), 1)
    ut = jnp.where(ri <= ci, 1.0, 0.0).astype(BF16)
    gt = bits > lo
    eq = bits == lo
    need = float(cap) - jnp.sum(jnp.where(gt, 1.0, 0.0), axis=1, keepdims=True)
    ceq = _lane_cumsum(jnp.where(eq, 1.0, 0.0), ut)
    sel = gt | (eq & (ceq <= need))
    csel = _lane_cumsum(jnp.where(sel, 1.0, 0.0), ut)
    slot = jnp.where(sel, csel - 1.0, -1.0)
    slot_ref[0] = slot.astype(I32)
    padded = jnp.concatenate([slot, jnp.full((LANES - e, n), -1.0, F32)], axis=0)
    slott_ref[0] = padded.T


def _select(aff, cap):
    b, e, n = aff.shape
    return pl.pallas_call(
        functools.partial(_select_kernel, cap=cap),
        grid=(b,),
        in_specs=[pl.BlockSpec((1, e, n), lambda i: (i, 0, 0))],
        out_specs=[pl.BlockSpec((1, e, n), lambda i: (i, 0, 0)),
                   pl.BlockSpec((1, n, LANES), lambda i: (i, 0, 0))],
        out_shape=[jax.ShapeDtypeStruct((b, e, n), I32), jax.ShapeDtypeStruct((b, n, LANES), F32)],
        compiler_params=_params(("parallel",)),
        name="select",
    )(aff)


def _window(cum_ref, b, e, j, r, n_tiles, cap, win):
    c_lo = cum_ref[(b * N_EXPERTS + e) * (n_tiles + 1) + j]
    first = (c_lo // ROW_ALIGN) * ROW_ALIGN + r * win
    start = pl.multiple_of(jnp.minimum(first, cap - win), ROW_ALIGN)
    return first, start


def _gather_kernel(cum_ref, rnd_ref, slot_ref, h_ref, affn_ref, xs_ref, gate_ref, gacc, *, cap, win, n_tiles):
    b = pl.program_id(0)
    dq = pl.program_id(1)
    xs_ref[...] = jnp.zeros_like(xs_ref)

    @pl.when(dq == 0)
    def _():
        gacc[...] = jnp.zeros_like(gacc)

    rows = lax.broadcasted_iota(I32, (win, SEL_TILE), 0)

    def tile_body(j, carry):
        toks = pl.ds(pl.multiple_of(j * SEL_TILE, SEL_TILE), SEL_TILE)
        slots = slot_ref[0, j]
        h_tile = h_ref[0, toks, :]

        def round_body(r, carry2):
            starts, pieces = [], []
            for e in range(N_EXPERTS):
                first, start = _window(cum_ref, b, e, j, r, n_tiles, cap, win)
                s_row = slots[e:e + 1, :]
                hit = (rows + start == s_row) & (s_row >= first)
                pieces.append(jnp.where(hit, 1.0, 0.0).astype(BF16))
                starts.append(start)
            onehot = jnp.concatenate(pieces, axis=0)
            res = _dot(onehot, h_tile)
            for e in range(N_EXPERTS):
                sl = pl.ds(starts[e], win)
                xs_ref[e, 0, sl, :] = (xs_ref[e, 0, sl, :].astype(F32) + res[e * win:(e + 1) * win]).astype(BF16)

            @pl.when(dq == 0)
            def _():
                g = _dot(onehot, affn_ref[0, toks, :])
                for e in range(N_EXPERTS):
                    gacc[e, pl.ds(starts[e], win), :] += g[e * win:(e + 1) * win]
            return carry2

        lax.fori_loop(0, rnd_ref[b * n_tiles + j], round_body, 0)
        return carry

    lax.fori_loop(0, n_tiles, tile_body, 0)

    @pl.when(dq == 0)
    def _():
        lane = lax.broadcasted_iota(I32, (cap, LANES), 1)
        for e in range(N_EXPERTS):
            mine = (lane == e) | (lane == N_EXPERTS + e) | (lane == 2 * N_EXPERTS + e)
            col = jnp.sum(jnp.where(mine, gacc[e], 0.0), axis=1, keepdims=True)
            gate_ref[e, 0] = jnp.broadcast_to(col, (cap, LANES))


def _gather(cum, rounds, slot, h2, affn, cap):
    b, n, d = h2.shape
    n_tiles = n // SEL_TILE
    win = min(EC_WINDOW, cap)
    dn = 512
    slot_tiles = slot.reshape(b, N_EXPERTS, n_tiles, SEL_TILE).transpose(0, 2, 1, 3)
    grid_spec = pltpu.PrefetchScalarGridSpec(
        num_scalar_prefetch=2,
        grid=(b, d // dn),
        in_specs=[
            pl.BlockSpec((1, n_tiles, N_EXPERTS, SEL_TILE), lambda i, q, c, r: (i, 0, 0, 0)),
            pl.BlockSpec((1, n, dn), lambda i, q, c, r: (i, 0, q)),
            pl.BlockSpec((1, n, LANES), lambda i, q, c, r: (i, 0, 0)),
        ],
        out_specs=[pl.BlockSpec((N_EXPERTS, 1, cap, dn), lambda i, q, c, r: (0, i, 0, q)),
                   pl.BlockSpec((N_EXPERTS, 1, cap, LANES), lambda i, q, c, r: (0, i, 0, 0))],
        scratch_shapes=[pltpu.VMEM((N_EXPERTS, cap, LANES), F32)],
    )
    return pl.pallas_call(
        functools.partial(_gather_kernel, cap=cap, win=win, n_tiles=n_tiles),
        grid_spec=grid_spec,
        out_shape=[jax.ShapeDtypeStruct((N_EXPERTS, b, cap, d), BF16),
                   jax.ShapeDtypeStruct((N_EXPERTS, b, cap, LANES), F32)],
        compiler_params=_params(("parallel", "arbitrary")),
        name="ec_gather",
    )(cum, rounds, slot_tiles, h2, affn)


def _expert_up_kernel(xs_ref, wg_ref, wu_ref, h_ref, wg_bf, wu_bf):
    @pl.when(pl.program_id(2) == 0)
    def _():
        wg_bf[...] = wg_ref[0, 0].astype(BF16)
        wu_bf[...] = wu_ref[0, 0].astype(BF16)

    x = xs_ref[0]
    a = _dot(x, wg_bf[...])
    u = _dot(x, wu_bf[...])
    h_ref[0] = (a * jax.nn.sigmoid(a) * u).astype(BF16)


def _expert_down_kernel(h_ref, wd_ref, gate_ref, ys_ref, wd_bf):
    @pl.when(pl.program_id(2) == 0)
    def _():
        wd_bf[...] = wd_ref[0, 0].astype(BF16)

    y = _dot(h_ref[0], wd_bf[...])
    g = gate_ref[0]
    ys_ref[0] = (y * jnp.concatenate([g] * (y.shape[1] // LANES), axis=1)).astype(BF16)


def _experts(xs, gate, wg, wu, wd, layer):
    e, m, d = xs.shape
    ff = wg.shape[3]
    tm = min(1024, m)
    tf = 512
    tn = 512
    sem = ("parallel", "parallel", "arbitrary")
    hidden = pl.pallas_call(
        _expert_up_kernel,
        grid=(e, ff // tf, m // tm),
        in_specs=[
            pl.BlockSpec((1, tm, d), lambda ei, f, i: (ei, i, 0)),
            pl.BlockSpec((1, 1, d, tf), lambda ei, f, i: (layer, ei, 0, f)),
            pl.BlockSpec((1, 1, d, tf), lambda ei, f, i: (layer, ei, 0, f)),
        ],
        out_specs=pl.BlockSpec((1, tm, tf), lambda ei, f, i: (ei, i, f)),
        out_shape=jax.ShapeDtypeStruct((e, m, ff), BF16),
        scratch_shapes=[pltpu.VMEM((d, tf), BF16), pltpu.VMEM((d, tf), BF16)],
        compiler_params=_params(sem),
        name="expert_up",
    )(xs, wg, wu)
    return pl.pallas_call(
        _expert_down_kernel,
        grid=(e, d // tn, m // tm),
        in_specs=[
            pl.BlockSpec((1, tm, ff), lambda ei, j, i: (ei, i, 0)),
            pl.BlockSpec((1, 1, ff, tn), lambda ei, j, i: (layer, ei, 0, j)),
            pl.BlockSpec((1, tm, LANES), lambda ei, j, i: (ei, i, 0)),
        ],
        out_specs=pl.BlockSpec((1, tm, tn), lambda ei, j, i: (ei, i, j)),
        out_shape=jax.ShapeDtypeStruct((e, m, d), BF16),
        scratch_shapes=[pltpu.VMEM((ff, tn), BF16)],
        compiler_params=_params(sem),
        name="expert_down",
    )(hidden, wd, gate)


def _scatter_kernel(cum_ref, rnd_ref, slott_ref, ys_ref, x_ref, g_ref, o_ref, *, cap, win, n_tiles):
    b = pl.program_id(0)
    j = pl.program_id(1)
    st = slott_ref[0]
    per_block = LANES // win
    lane = lax.broadcasted_iota(I32, (SEL_TILE, LANES), 1)
    lane_f = lane.astype(F32)
    group = lane // win

    def round_body(r, acc):
        blocks, pieces = [], []
        for k in range(N_EXPERTS // per_block):
            target = None
            for i in range(per_block):
                e = k * per_block + i
                first, start = _window(cum_ref, b, e, j, r, n_tiles, cap, win)
                col = st[:, e:e + 1]
                tgt = jnp.where(col >= first.astype(F32), col - (start - i * win).astype(F32), -1.0)
                target = tgt if target is None else jnp.where(group == i, tgt, target)
                pieces.append(ys_ref[e, 0, pl.ds(start, win), :])
            blocks.append(jnp.where(lane_f == target, 1.0, 0.0).astype(BF16))
        onehot = jnp.concatenate(blocks, axis=1)
        stacked = jnp.concatenate(pieces, axis=0)
        return acc + _dot(onehot, stacked)

    acc = lax.fori_loop(0, rnd_ref[b * n_tiles + j], round_body, jnp.zeros(o_ref.shape[1:], F32))
    o_ref[0] = x_ref[0] + g_ref[0] * acc


def _scatter(cum, rounds, slott, ys, x1, gate, cap):
    b, n, d = x1.shape
    n_tiles = n // SEL_TILE
    win = min(EC_WINDOW, cap)
    grid_spec = pltpu.PrefetchScalarGridSpec(
        num_scalar_prefetch=2,
        grid=(b, n_tiles),
        in_specs=[
            pl.BlockSpec((1, SEL_TILE, LANES), lambda i, t, c, r: (i, t, 0)),
            pl.BlockSpec((N_EXPERTS, 1, cap, d), lambda i, t, c, r: (0, i, 0, 0), pipeline_mode=pl.Buffered(1)),
            pl.BlockSpec((1, SEL_TILE, d), lambda i, t, c, r: (i, t, 0)),
            pl.BlockSpec((1, 1, d), lambda i, t, c, r: (i, 0, 0)),
        ],
        out_specs=pl.BlockSpec((1, SEL_TILE, d), lambda i, t, c, r: (i, t, 0)),
    )
    return pl.pallas_call(
        functools.partial(_scatter_kernel, cap=cap, win=win, n_tiles=n_tiles),
        grid_spec=grid_spec,
        out_shape=jax.ShapeDtypeStruct((b, n, d), F32),
        compiler_params=_params(("parallel", "arbitrary")),
        name="ec_scatter",
    )(cum, rounds, slott, ys, x1, gate)


def _mod_chunks(mod_rows):
    b = mod_rows.shape[0]
    return [mod_rows[:, i * D_MODEL:(i + 1) * D_MODEL].reshape(b, 1, D_MODEL) for i in range(N_MOD)]


def _token_mixer(h, zmix, att, lw, dft):
    b, n, d = h.shape
    y_conv, y_pool = _stencil(zmix, lw["conv_w"], lw["pool_w"], lw["pool_scale"])
    y_f = _fourier(zmix, *dft)
    branches = [y.reshape(b * n, MIX_W) for y in (y_conv, att, y_f, y_pool)]
    merged = _merge(h.reshape(b * n, d), branches, lw["w_in"], lw["w_branch"])
    return merged.reshape(b, n, d)


def _ffn(x1, h2, aff, affn, g2, lw):
    b, n, d = x1.shape
    cap = EC_CAPACITY * n // N_EXPERTS
    slot, slott = _select(aff, cap)
    per_tile = jnp.sum((slot >= 0).reshape(b, N_EXPERTS, n // SEL_TILE, SEL_TILE), axis=-1, dtype=I32)
    cum = jnp.concatenate([jnp.zeros((b, N_EXPERTS, 1), I32), jnp.cumsum(per_tile, axis=-1)], axis=-1)
    win = min(EC_WINDOW, cap)
    span = cum[..., 1:] - (cum[..., :-1] // ROW_ALIGN) * ROW_ALIGN
    rounds = jnp.max((span + win - 1) // win, axis=1).reshape(-1)
    cum = cum.reshape(-1)
    xs, gate = _gather(cum, rounds, slot, h2, affn, cap)
    ys = _experts(xs.reshape(N_EXPERTS, b * cap, d), gate.reshape(N_EXPERTS, b * cap, LANES),
                  lw["w_gate"], lw["w_up"], lw["w_down"], lw["layer"])
    return _scatter(cum, rounds, slott, ys.reshape(N_EXPERTS, b, cap, d), x1, g2, cap)


def kernel(x, c, ctx, c_ctx, w_ada, b_ada, norm1_w, norm2_w, w_in, conv_w, q_norm_w, k_norm_w,
           na_rpb, pool_w, pool_scale, w_branch, w_out, w_router, w_exp_gate, w_exp_up, w_exp_down):
    bsz, n, d = x.shape
    l_ctx = ctx.shape[1]
    depth = w_ada.shape[0]
    n_rows = n // GRID_W

    c16 = jnp.zeros((16, d), F32).at[:bsz].set(c).at[bsz].set(c_ctx)
    mod_all = _ada(c16, w_ada, b_ada)

    hd = np.arange(MIX_W) // HEAD_DIM
    bd = jnp.asarray(hd[:, None] == hd[None, :], dtype=BF16)
    dft_lat = _dft_mats(n, 1.0) + _dft_mats(FOURIER_GC, (n * FOURIER_GC) ** -0.5)
    dft_ctx = _dft_mats(l_ctx, 1.0) + _dft_mats(FOURIER_GC, (l_ctx * FOURIER_GC) ** -0.5)

    for l in range(depth):
        last = l == depth - 1
        lw = {
            "w_in": w_in[l].astype(BF16), "conv_w": conv_w[l], "pool_w": pool_w[l],
            "pool_scale": pool_scale[l], "w_branch": w_branch[l].astype(BF16),
            "w_gate": w_exp_gate, "w_up": w_exp_up, "w_down": w_exp_down, "layer": l,
        }
        w_out_bf = w_out[l].astype(BF16)
        wr_pad = jnp.zeros((d, LANES), F32).at[:, :N_EXPERTS].set(w_router[l])
        wr_hi = wr_pad.astype(BF16)
        wr_lo = (wr_pad - wr_hi.astype(F32)).astype(BF16)
        q_w = jnp.tile(q_norm_w[l], N_HEADS).reshape(1, MIX_W)
        k_w = jnp.tile(k_norm_w[l], N_HEADS).reshape(1, MIX_W)
        tb = _na_bias_table(na_rpb[l], n_rows)
        sh1, sc1, g1, sh2, sc2, g2 = _mod_chunks(mod_all[l, :bsz])
        mc = _mod_chunks(jnp.broadcast_to(mod_all[l, bsz:bsz + 1], (bsz, N_MOD * d)))

        hc = _prenorm(ctx, norm1_w[l], mc[0], mc[1])
        zc_mix = _in_proj(hc.reshape(bsz * l_ctx, d), lw["w_in"], 0, MIX_COLS).reshape(bsz, l_ctx, MIX_COLS)

        h = _prenorm(x, norm1_w[l], sh1, sc1)
        zmix = _in_proj(h.reshape(bsz * n, d), lw["w_in"], 0, MIX_COLS).reshape(bsz, n, MIX_COLS)
        att = _neighbourhood_attention(zmix, zc_mix, q_w, k_w, bd, tb)
        merged = _token_mixer(h, zmix, att, lw, dft_lat)
        x1, h2, aff, affn = _out_proj(merged, w_out_bf, x, g1, norm2_w[l], sh2, sc2, wr_hi, wr_lo)
        x = _ffn(x1, h2, aff, affn, g2, lw)

        if not last:
            att_c = _context_attention(zc_mix, q_w, k_w, bd)
            merged_c = _token_mixer(hc, zc_mix, att_c, lw, dft_ctx)
            c1, hc2, aff_c, affn_c = _out_proj(merged_c, w_out_bf, ctx, mc[2], norm2_w[l], mc[3], mc[4],
                                                   wr_hi, wr_lo)
            ctx = _ffn(c1, hc2, aff_c, affn_c, mc[5], lw)
    return x
```

```python
import functools

import numpy as np
import jax
import jax.numpy as jnp
from jax import lax
from jax.experimental import pallas as pl
from jax.experimental.pallas import tpu as pltpu

F32 = jnp.float32
BF16 = jnp.bfloat16
I32 = jnp.int32
HIGHEST = lax.Precision.HIGHEST

D_MODEL = 2048
GRID_W = 64
MIX_W = D_MODEL // 4
N_HEADS = 8
HEAD_DIM = MIX_W // N_HEADS
WIN_H = 8
WIN_W = 16
ATT_SCALE = HEAD_DIM ** -0.5
POOL_WINDOWS = (2, 4, 8, 16)
POOL_GC = MIX_W // len(POOL_WINDOWS)
FOURIER_GC = MIX_W // 4
N_EXPERTS = 16
EC_CAPACITY = 2
N_MOD = 6
EPS = 1e-6
NEG_INF = -1e30
MIX_COLS = 8 * MIX_W
LANES = 128
HALO = 16
SEL_TILE = 256
ROW_ALIGN = 16
EC_WINDOW = 64
ROWS_PER_PASS = 2
VMEM_LIMIT = 56 * 1024 * 1024

COL_XA, COL_GB, COL_GC, COL_Q, COL_K, COL_V, COL_F, COL_P = range(8)


def _params(sem, vmem=VMEM_LIMIT):
    return pltpu.CompilerParams(dimension_semantics=sem, vmem_limit_bytes=vmem)


def _dot(a, b):
    return jnp.dot(a, b, preferred_element_type=F32)


def _dot_nt(a, b):
    return lax.dot_general(a, b, (((1,), (1,)), ((), ())), preferred_element_type=F32)


def _ada_kernel(c_ref, w_ref, b_ref, o_ref):
    c = c_ref[...]
    s = c * jax.nn.sigmoid(c)
    o_ref[0] = jnp.dot(s, w_ref[0], precision=HIGHEST, preferred_element_type=F32) + b_ref[0]


def _ada(c16, w_ada, b_ada):
    depth, d, ncol = w_ada.shape
    tn = 1024
    return pl.pallas_call(
        _ada_kernel,
        grid=(depth, ncol // tn),
        in_specs=[
            pl.BlockSpec((16, d), lambda l, j: (0, 0)),
            pl.BlockSpec((1, d, tn), lambda l, j: (l, 0, j)),
            pl.BlockSpec((1, 1, tn), lambda l, j: (l, 0, j)),
        ],
        out_specs=pl.BlockSpec((1, 16, tn), lambda l, j: (l, 0, j)),
        out_shape=jax.ShapeDtypeStruct((depth, 16, ncol), F32),
        compiler_params=_params(("parallel", "parallel")),
        name="ada",
    )(c16, w_ada, b_ada.reshape(depth, 1, ncol))


def _prenorm_math(x, w, shift, scale):
    ms = jnp.mean(x * x, axis=-1, keepdims=True)
    y = x * lax.rsqrt(ms + EPS) * w
    return y * (1.0 + scale) + shift


def _prenorm_kernel(x_ref, w_ref, sh_ref, sc_ref, o_ref):
    o_ref[0] = _prenorm_math(x_ref[0], w_ref[...], sh_ref[0], sc_ref[0]).astype(BF16)


def _prenorm(x, w, shift, scale):
    b, n, d = x.shape
    tm = min(512, n)
    return pl.pallas_call(
        _prenorm_kernel,
        grid=(b, n // tm),
        in_specs=[
            pl.BlockSpec((1, tm, d), lambda i, j: (i, j, 0)),
            pl.BlockSpec((1, d), lambda i, j: (0, 0)),
            pl.BlockSpec((1, 1, d), lambda i, j: (i, 0, 0)),
            pl.BlockSpec((1, 1, d), lambda i, j: (i, 0, 0)),
        ],
        out_specs=pl.BlockSpec((1, tm, d), lambda i, j: (i, j, 0)),
        out_shape=jax.ShapeDtypeStruct((b, n, d), BF16),
        compiler_params=_params(("parallel", "parallel")),
        name="prenorm",
    )(x, w.reshape(1, d), shift, scale)


def _mm_kernel(a_ref, w_ref, o_ref):
    o_ref[...] = _dot(a_ref[...], w_ref[...]).astype(o_ref.dtype)


def _in_proj(h2d, w, col0, ncols):
    m, k = h2d.shape
    tm = min(1024, m)
    tn = 1024
    c0 = col0 // tn
    return pl.pallas_call(
        _mm_kernel,
        grid=(m // tm, ncols // tn),
        in_specs=[
            pl.BlockSpec((tm, k), lambda i, j: (i, 0)),
            pl.BlockSpec((k, tn), lambda i, j: (0, c0 + j)),
        ],
        out_specs=pl.BlockSpec((tm, tn), lambda i, j: (i, j)),
        out_shape=jax.ShapeDtypeStruct((m, ncols), BF16),
        compiler_params=_params(("parallel", "parallel")),
        name="in_proj",
    )(h2d, w)


def _stencil_kernel(xa_ref, gb_ref, gc_ref, zp_ref, xa_p, xa_n, gc_p, gc_n, zp_p, zp_n,
                    cw_ref, pw_ref, ps_ref, oc_ref, op_ref, *, tm, n_seq):
    i = pl.program_id(1)
    nt = pl.num_programs(1)
    rows = tm + 2 * HALO

    def ext(main_ref, prev_ref, next_ref):
        main = main_ref[0].astype(F32)
        prev = jnp.where(i > 0, prev_ref[0].astype(F32), 0.0)
        nxt = jnp.where(i < nt - 1, next_ref[0].astype(F32), 0.0)
        return jnp.concatenate([prev, main, nxt], axis=0)

    def shifted(a, d):
        if d == 0:
            return a[HALO:HALO + tm]
        return pltpu.roll(a, (-d) % rows, axis=0)[HALO:HALO + tm]

    u = ext(gc_ref, gc_p, gc_n) * ext(xa_ref, xa_p, xa_n)
    cw = cw_ref[...]
    y = cw[0:1] * shifted(u, -1) + cw[1:2] * shifted(u, 0) + cw[2:3] * shifted(u, 1)
    oc_ref[0] = (gb_ref[0].astype(F32) * y).astype(BF16)

    t = (i * tm + lax.broadcasted_iota(I32, (tm, 1), 0)).astype(F32)
    zp = ext(zp_ref, zp_p, zp_n)
    for g, win in enumerate(POOL_WINDOWS):
        ug = zp[:, g * POOL_GC:(g + 1) * POOL_GC]
        acc = ug + pltpu.roll(ug, 1, axis=0)
        half = 1
        while 2 * half < win:
            acc = pltpu.roll(acc, rows - half, axis=0) + pltpu.roll(acc, half, axis=0)
            half *= 2
        wsum = acc[HALO:HALO + tm]
        lo = jnp.maximum(t - (win // 2), 0.0)
        hi = jnp.minimum(t + (win - win // 2 - 1), float(n_seq - 1))
        pooled = wsum / (hi - lo + 1.0) - ug[HALO:HALO + tm]
        yg = _dot(pooled.astype(BF16), pw_ref[g])
        op_ref[0, :, g * POOL_GC:(g + 1) * POOL_GC] = (
            yg * ps_ref[:, g * POOL_GC:(g + 1) * POOL_GC]).astype(BF16)


def _stencil(zmix, conv_w, pool_w, pool_scale):
    b, n, _ = zmix.shape
    tm = min(512, n)
    hb = tm // HALO
    last = n // HALO - 1

    def main(col):
        return pl.BlockSpec((1, tm, MIX_W), lambda bi, i: (bi, i, col))

    def prev(col):
        return pl.BlockSpec((1, HALO, MIX_W), lambda bi, i: (bi, jnp.maximum(i * hb - 1, 0), col))

    def nxt(col):
        return pl.BlockSpec((1, HALO, MIX_W), lambda bi, i: (bi, jnp.minimum((i + 1) * hb, last), col))

    out_spec = pl.BlockSpec((1, tm, MIX_W), lambda bi, i: (bi, i, 0))
    return pl.pallas_call(
        functools.partial(_stencil_kernel, tm=tm, n_seq=n),
        grid=(b, n // tm),
        in_specs=[main(COL_XA), main(COL_GB), main(COL_GC), main(COL_P),
                  prev(COL_XA), nxt(COL_XA), prev(COL_GC), nxt(COL_GC), prev(COL_P), nxt(COL_P),
                  pl.BlockSpec((3, MIX_W), lambda bi, i: (0, 0)),
                  pl.BlockSpec((len(POOL_WINDOWS), POOL_GC, POOL_GC), lambda bi, i: (0, 0, 0)),
                  pl.BlockSpec((1, MIX_W), lambda bi, i: (0, 0))],
        out_specs=[out_spec, out_spec],
        out_shape=[jax.ShapeDtypeStruct((b, n, MIX_W), BF16)] * 2,
        compiler_params=_params(("parallel", "parallel")),
        name="stencil",
    )(zmix, zmix, zmix, zmix, zmix, zmix, zmix, zmix, zmix, zmix,
      conv_w, pool_w.astype(BF16), pool_scale.reshape(1, MIX_W))


def _fourier_kernel(cn_ref, sn_ref, u_ref, cc_ref, sc_ref, o_ref):
    u = u_ref[0]
    p = _dot(cn_ref[...], u).astype(BF16)
    q = _dot(sn_ref[...], u).astype(BF16)
    for g in range(MIX_W // FOURIER_GC):
        sl = slice(g * FOURIER_GC, (g + 1) * FOURIER_GC)
        o_ref[0, :, sl] = (_dot(p[:, sl], cc_ref[...]) - _dot(q[:, sl], sc_ref[...])).astype(BF16)


def _dft_mats(n, scale):
    k = jnp.arange(n, dtype=I32)[:, None]

    def table(count, period):
        r = (k * jnp.arange(count, dtype=I32)[None, :]) % period
        ang = r.astype(F32) * np.float32(2.0 * np.pi / period)
        return jnp.cos(ang), jnp.sin(ang)

    inner = 64
    if n <= inner or n % inner:
        c, s = table(n, n)
    else:
        ca, sa = table(n // inner, n // inner)
        cb, sb = table(inner, n)
        c = (ca[:, :, None] * cb[:, None, :] - sa[:, :, None] * sb[:, None, :]).reshape(n, n)
        s = (sa[:, :, None] * cb[:, None, :] + ca[:, :, None] * sb[:, None, :]).reshape(n, n)
    return (c * scale).astype(BF16), (s * scale).astype(BF16)


def _fourier(zmix, cn, sn, cc, sc):
    b, n, _ = zmix.shape
    tk = min(512, n)
    return pl.pallas_call(
        _fourier_kernel,
        grid=(n // tk, b),
        in_specs=[
            pl.BlockSpec((tk, n), lambda k, bi: (k, 0)),
            pl.BlockSpec((tk, n), lambda k, bi: (k, 0)),
            pl.BlockSpec((1, n, MIX_W), lambda k, bi: (bi, 0, COL_F)),
            pl.BlockSpec((FOURIER_GC, FOURIER_GC), lambda k, bi: (0, 0)),
            pl.BlockSpec((FOURIER_GC, FOURIER_GC), lambda k, bi: (0, 0)),
        ],
        out_specs=pl.BlockSpec((1, tk, MIX_W), lambda k, bi: (bi, k, 0)),
        out_shape=jax.ShapeDtypeStruct((b, n, MIX_W), BF16),
        compiler_params=_params(("parallel", "parallel")),
        name="fourier",
    )(cn, sn, zmix, cc, sc)


def _head_rms(x, w, bd):
    x2 = x * x
    hi = x2.astype(BF16)
    lo = (x2 - hi.astype(F32)).astype(BF16)
    ms = (_dot(hi, bd) + _dot(lo, bd)) * (1.0 / HEAD_DIM)
    return x * lax.rsqrt(ms + EPS) * w


def _pair_stack(q2):
    lane = lax.broadcasted_iota(I32, q2.shape, 1)
    zero = jnp.zeros_like(q2)
    return jnp.concatenate([jnp.where(lane < HEAD_DIM, q2, zero),
                            jnp.where(lane >= HEAD_DIM, q2, zero)], axis=0)


def _pair_unstack(o2, m):
    lane = lax.broadcasted_iota(I32, (m, LANES), 1)
    return jnp.where(lane < HEAD_DIM, o2[:m], o2[m:])


def _na_kernel(q_ref, k_ref, v_ref, kc_ref, vc_ref, qw_ref, kw_ref, bd_ref, tb_ref, o_ref,
               kn_scr, kcn_scr, qn_scr, *, rt, n_rows):
    rb = pl.program_id(1)
    n_seq = n_rows * GRID_W
    chunk = 512

    @pl.when(rb == 0)
    def _():
        def body(c, carry):
            sl = pl.ds(pl.multiple_of(c * chunk, chunk), chunk)
            kn_scr[sl, :] = _head_rms(k_ref[0, sl, :].astype(F32), kw_ref[...], bd_ref[...]).astype(BF16)
            return carry
        lax.fori_loop(0, n_seq // chunk, body, 0)
        kcn_scr[...] = _head_rms(kc_ref[0].astype(F32), kw_ref[...], bd_ref[...]).astype(BF16)

    qn_scr[...] = (_head_rms(q_ref[0].astype(F32), qw_ref[...], bd_ref[...]) * ATT_SCALE).astype(BF16)

    kh = min(WIN_H, n_rows)
    nloc = kh * GRID_W

    def row_body(jj, carry):
        units = []
        for i in range(ROWS_PER_PASS):
            j = jj * ROWS_PER_PASS + i
            r = rb * rt + j
            rs = jnp.clip(r - kh // 2, 0, n_rows - kh)
            q_rows = pl.ds(pl.multiple_of(j * GRID_W, GRID_W), GRID_W)
            k_rows = pl.ds(pl.multiple_of(rs * GRID_W, GRID_W), nloc)
            for p in range(N_HEADS // 2):
                units.append((q_rows, k_rows, rs - r + (WIN_H - 1), p, slice(p * LANES, (p + 1) * LANES)))
        qs = [_pair_stack(qn_scr[q_rows, cols]) for q_rows, _, _, _, cols in units]
        s_loc = [_dot_nt(q, kn_scr[k_rows, cols]) for q, (_, k_rows, _, _, cols) in zip(qs, units)]
        s_ctx = [_dot_nt(q, kcn_scr[:, cols]) for q, (_, _, _, _, cols) in zip(qs, units)]
        s_loc = [s + tb_ref[dr0, p] for s, (_, _, dr0, p, _) in zip(s_loc, units)]
        m = [jnp.maximum(jnp.max(a, axis=1, keepdims=True), jnp.max(c, axis=1, keepdims=True))
             for a, c in zip(s_loc, s_ctx)]
        p_loc = [jnp.exp(a - mx) for a, mx in zip(s_loc, m)]
        p_ctx = [jnp.exp(c - mx) for c, mx in zip(s_ctx, m)]
        denom = [jnp.sum(a, axis=1, keepdims=True) + jnp.sum(c, axis=1, keepdims=True) for a, c in zip(p_loc, p_ctx)]
        o2 = [_dot(a.astype(BF16), v_ref[0, k_rows, cols]) + _dot(c.astype(BF16), vc_ref[0, :, cols])
              for a, c, (_, k_rows, _, _, cols) in zip(p_loc, p_ctx, units)]
        for o, d, (q_rows, _, _, _, cols) in zip(o2, denom, units):
            o_ref[0, q_rows, cols] = _pair_unstack(o / d, GRID_W).astype(BF16)
        return carry

    lax.fori_loop(0, rt // ROWS_PER_PASS, row_body, 0)


def _na_bias_table(rpb, n_rows):
    kh = min(WIN_H, n_rows)
    c = np.arange(GRID_W)
    cs = np.clip(c - WIN_W // 2, 0, GRID_W - WIN_W)
    kc = np.arange(GRID_W)
    ok = (kc[None, :] >= cs[:, None]) & (kc[None, :] < cs[:, None] + WIN_W)
    edge = GRID_W - WIN_W
    padded = jnp.pad(rpb.astype(F32), ((0, 0), (0, 0), (edge, edge)))
    toeplitz = jnp.stack([padded[:, :, GRID_W - 1 - q:2 * GRID_W - 1 - q] for q in range(GRID_W)], axis=2)
    masked = jnp.where(ok[None, None], toeplitz, NEG_INF)
    t = jnp.stack([masked[:, d0:d0 + kh] for d0 in range(WIN_H)], axis=0)
    t = jnp.transpose(t, (0, 1, 3, 2, 4))
    return t.reshape(WIN_H, N_HEADS // 2, 2 * GRID_W, kh * GRID_W)


def _neighbourhood_attention(zmix, zc_mix, q_w, k_w, bd, tb):
    b, n, _ = zmix.shape
    l = zc_mix.shape[1]
    n_rows = n // GRID_W
    rt = 8
    nloc = min(WIN_H, n_rows) * GRID_W
    return pl.pallas_call(
        functools.partial(_na_kernel, rt=rt, n_rows=n_rows),
        grid=(b, n_rows // rt),
        in_specs=[
            pl.BlockSpec((1, rt * GRID_W, MIX_W), lambda bi, r: (bi, r, COL_Q)),
            pl.BlockSpec((1, n, MIX_W), lambda bi, r: (bi, 0, COL_K)),
            pl.BlockSpec((1, n, MIX_W), lambda bi, r: (bi, 0, COL_V)),
            pl.BlockSpec((1, l, MIX_W), lambda bi, r: (bi, 0, COL_K)),
            pl.BlockSpec((1, l, MIX_W), lambda bi, r: (bi, 0, COL_V)),
            pl.BlockSpec((1, MIX_W), lambda bi, r: (0, 0)),
            pl.BlockSpec((1, MIX_W), lambda bi, r: (0, 0)),
            pl.BlockSpec((MIX_W, MIX_W), lambda bi, r: (0, 0)),
            pl.BlockSpec((WIN_H, N_HEADS // 2, 2 * GRID_W, nloc), lambda bi, r: (0, 0, 0, 0),
                         pipeline_mode=pl.Buffered(1)),
        ],
        out_specs=pl.BlockSpec((1, rt * GRID_W, MIX_W), lambda bi, r: (bi, r, 0)),
        out_shape=jax.ShapeDtypeStruct((b, n, MIX_W), BF16),
        scratch_shapes=[pltpu.VMEM((n, MIX_W), BF16), pltpu.VMEM((l, MIX_W), BF16),
                        pltpu.VMEM((rt * GRID_W, MIX_W), BF16)],
        compiler_params=_params(("parallel", "arbitrary")),
        name="nbr_attention",
    )(zmix, zmix, zmix, zc_mix, zc_mix, q_w, k_w, bd, tb)


def _ctx_attn_kernel(q_ref, k_ref, v_ref, qw_ref, kw_ref, bd_ref, o_ref):
    l = q_ref.shape[1]
    qn = (_head_rms(q_ref[0].astype(F32), qw_ref[...], bd_ref[...]) * ATT_SCALE).astype(BF16)
    kn = _head_rms(k_ref[0].astype(F32), kw_ref[...], bd_ref[...]).astype(BF16)
    for p in range(N_HEADS // 2):
        cols = slice(p * LANES, (p + 1) * LANES)
        qs = _pair_stack(qn[:, cols])
        s = _dot_nt(qs, kn[:, cols])
        m = jnp.max(s, axis=1, keepdims=True)
        e = jnp.exp(s - m)
        o2 = _dot(e.astype(BF16), v_ref[0, :, cols]) / jnp.sum(e, axis=1, keepdims=True)
        o_ref[0, :, cols] = _pair_unstack(o2, l).astype(BF16)


def _context_attention(zc_mix, q_w, k_w, bd):
    b, l, _ = zc_mix.shape
    vec = pl.BlockSpec((1, MIX_W), lambda bi: (0, 0))
    return pl.pallas_call(
        _ctx_attn_kernel,
        grid=(b,),
        in_specs=[
            pl.BlockSpec((1, l, MIX_W), lambda bi: (bi, 0, COL_Q)),
            pl.BlockSpec((1, l, MIX_W), lambda bi: (bi, 0, COL_K)),
            pl.BlockSpec((1, l, MIX_W), lambda bi: (bi, 0, COL_V)),
            vec, vec,
            pl.BlockSpec((MIX_W, MIX_W), lambda bi: (0, 0)),
        ],
        out_specs=pl.BlockSpec((1, l, MIX_W), lambda bi: (bi, 0, 0)),
        out_shape=jax.ShapeDtypeStruct((b, l, MIX_W), BF16),
        compiler_params=_params(("parallel",)),
        name="ctx_attention",
    )(zc_mix, zc_mix, zc_mix, q_w, k_w, bd)


def _merge_kernel(h_ref, y0, y1, y2, y3, g0, g1, g2, g3, wb_ref, o_ref):
    h = h_ref[...]
    acc = None
    for i, (y, g) in enumerate(((y0, g0), (y1, g1), (y2, g2), (y3, g3))):
        gate = jax.nn.sigmoid(_dot(h, g[...]))
        term = gate * _dot(y[...], wb_ref[i])
        acc = term if acc is None else acc + term
    o_ref[...] = acc.astype(BF16)


def _merge(h2d, branches, w_in_bf, w_branch_bf):
    m, d = h2d.shape
    tm = min(1024, m)
    tn = 256
    gate0 = MIX_COLS // tn

    def gspec(i):
        return pl.BlockSpec((d, tn), lambda r, n: (0, gate0 + i * (d // tn) + n))

    yspec = pl.BlockSpec((tm, MIX_W), lambda r, n: (r, 0))
    return pl.pallas_call(
        _merge_kernel,
        grid=(m // tm, d // tn),
        in_specs=[pl.BlockSpec((tm, d), lambda r, n: (r, 0)), yspec, yspec, yspec, yspec,
                  gspec(0), gspec(1), gspec(2), gspec(3),
                  pl.BlockSpec((4, MIX_W, tn), lambda r, n: (0, 0, n))],
        out_specs=pl.BlockSpec((tm, tn), lambda r, n: (r, n)),
        out_shape=jax.ShapeDtypeStruct((m, d), BF16),
        compiler_params=_params(("parallel", "parallel")),
        name="merge",
    )(h2d, *branches, w_in_bf, w_in_bf, w_in_bf, w_in_bf, w_branch_bf)


def _out_proj_kernel(m_ref, w_ref, x_ref, g_ref, nw_ref, sh_ref, sc_ref, wrc_ref,
                     x1_ref, h2_ref, aff_ref, affn_ref):
    x1 = x_ref[0] + g_ref[0] * _dot(m_ref[0], w_ref[...])
    x1_ref[0] = x1
    h2 = _prenorm_math(x1, nw_ref[...], sh_ref[0], sc_ref[0])
    h_hi = h2.astype(BF16)
    h2_ref[0] = h_hi
    h_lo = (h2 - h_hi.astype(F32)).astype(BF16)
    both = _dot(h_hi, wrc_ref[...])
    logits = both[:, :LANES] + (both[:, LANES:] + _dot(h_lo, wrc_ref[:, :LANES]))
    lane = lax.broadcasted_iota(I32, logits.shape, 1)
    logits = jnp.where(lane < N_EXPERTS, logits, NEG_INF)
    e = jnp.exp(logits - jnp.max(logits, axis=1, keepdims=True))
    aff = e / jnp.sum(e, axis=1, keepdims=True)
    aff_ref[0] = aff.T[:N_EXPERTS]
    hi = aff.astype(BF16).astype(F32)
    r1 = aff - hi
    mid = r1.astype(BF16).astype(F32)
    lo = (r1 - mid).astype(BF16).astype(F32)
    packed = hi + pltpu.roll(mid, N_EXPERTS, axis=1) + pltpu.roll(lo, 2 * N_EXPERTS, axis=1)
    affn_ref[0] = packed.astype(BF16)


def _out_proj(merged, w_out_bf, x, gate, norm_w, shift, scale, wr_cat):
    b, n, d = x.shape
    tm = min(512, n)
    vec = pl.BlockSpec((1, 1, d), lambda i, j: (i, 0, 0))
    tile = pl.BlockSpec((1, tm, d), lambda i, j: (i, j, 0))
    wr_spec = pl.BlockSpec((d, 2 * LANES), lambda i, j: (0, 0))
    return pl.pallas_call(
        _out_proj_kernel,
        grid=(b, n // tm),
        in_specs=[tile, pl.BlockSpec((d, d), lambda i, j: (0, 0), pipeline_mode=pl.Buffered(1)), tile, vec,
                  pl.BlockSpec((1, d), lambda i, j: (0, 0)), vec, vec, wr_spec],
        out_specs=[tile, tile,
                   pl.BlockSpec((1, N_EXPERTS, tm), lambda i, j: (i, 0, j)),
                   pl.BlockSpec((1, tm, LANES), lambda i, j: (i, j, 0))],
        out_shape=[jax.ShapeDtypeStruct((b, n, d), F32), jax.ShapeDtypeStruct((b, n, d), BF16),
                   jax.ShapeDtypeStruct((b, N_EXPERTS, n), F32), jax.ShapeDtypeStruct((b, n, LANES), BF16)],
        compiler_params=_params(("parallel", "parallel")),
        name="out_proj",
    )(merged, w_out_bf, x, gate, norm_w.reshape(1, d), shift, scale, wr_cat)


def _lane_cumsum(mask_f, ut):
    rows, n = mask_f.shape
    carry = jnp.zeros((rows, 1), F32)
    parts = []
    for j in range(n // LANES):
        c = _dot(mask_f[:, j * LANES:(j + 1) * LANES].astype(BF16), ut) + carry
        parts.append(c)
        carry = c[:, LANES - 1:LANES]
    return jnp.concatenate(parts, axis=1)


def _select_kernel(aff_ref, slot_ref, slott_ref, *, cap):
    aff = aff_ref[0]
    e, n = aff.shape
    bits = lax.bitcast_convert_type(aff, I32)

    def body(_, carry):
        lo, hi = carry
        mid = lo + ((hi - lo) >> 1)
        cnt = jnp.sum(jnp.where(bits >= mid, 1.0, 0.0), axis=1, keepdims=True)
        ge = cnt >= float(cap)
        return jnp.where(ge, mid, lo), jnp.where(ge, hi, mid)

    lo, _ = lax.fori_loop(0, 31, body, (jnp.zeros((e, 1), I32), jnp.full((e, 1), 0x7F800000, I32)))
    ri = lax.broadcasted_iota(I32, (LANES, LANES), 0)
    ci = lax.broadcasted_iota(I32, (LANES, LANES), 1)
    ut = jnp.where(ri <= ci, 1.0, 0.0).astype(BF16)
    gt = bits > lo
    eq = bits == lo
    need = float(cap) - jnp.sum(jnp.where(gt, 1.0, 0.0), axis=1, keepdims=True)
    ceq = _lane_cumsum(jnp.where(eq, 1.0, 0.0), ut)
    sel = gt | (eq & (ceq <= need))
    csel = _lane_cumsum(jnp.where(sel, 1.0, 0.0), ut)
    slot = jnp.where(sel, csel - 1.0, -1.0)
    slot_ref[0] = slot.astype(I32)
    padded = jnp.concatenate([slot, jnp.full((LANES - e, n), -1.0, F32)], axis=0)
    slott_ref[0] = padded.T


def _select(aff, cap):
    b, e, n = aff.shape
    return pl.pallas_call(
        functools.partial(_select_kernel, cap=cap),
        grid=(b,),
        in_specs=[pl.BlockSpec((1, e, n), lambda i: (i, 0, 0))],
        out_specs=[pl.BlockSpec((1, e, n), lambda i: (i, 0, 0)),
                   pl.BlockSpec((1, n, LANES), lambda i: (i, 0, 0))],
        out_shape=[jax.ShapeDtypeStruct((b, e, n), I32), jax.ShapeDtypeStruct((b, n, LANES), F32)],
        compiler_params=_params(("parallel",)),
        name="select",
    )(aff)


def _window(cum_ref, b, e, j, r, n_tiles, cap, win):
    c_lo = cum_ref[(b * N_EXPERTS + e) * (n_tiles + 1) + j]
    first = (c_lo // ROW_ALIGN) * ROW_ALIGN + r * win
    start = pl.multiple_of(jnp.minimum(first, cap - win), ROW_ALIGN)
    return first, start


def _gather_kernel(cum_ref, rnd_ref, slot_ref, h_ref, affn_ref, xs_ref, gate_ref, gacc, *, cap, win, n_tiles):
    b = pl.program_id(0)
    dq = pl.program_id(1)
    xs_ref[...] = jnp.zeros_like(xs_ref)

    @pl.when(dq == 0)
    def _():
        gacc[...] = jnp.zeros_like(gacc)

    rows = lax.broadcasted_iota(I32, (win, SEL_TILE), 0)

    def tile_body(j, carry):
        toks = pl.ds(pl.multiple_of(j * SEL_TILE, SEL_TILE), SEL_TILE)
        slots = slot_ref[0, j]
        h_tile = h_ref[0, toks, :]

        def round_body(r, carry2):
            starts, pieces = [], []
            for e in range(N_EXPERTS):
                first, start = _window(cum_ref, b, e, j, r, n_tiles, cap, win)
                s_row = slots[e:e + 1, :]
                hit = (rows + start == s_row) & (s_row >= first)
                pieces.append(jnp.where(hit, 1.0, 0.0).astype(BF16))
                starts.append(start)
            onehot = jnp.concatenate(pieces, axis=0)
            res = _dot(onehot, h_tile)
            for e in range(N_EXPERTS):
                sl = pl.ds(starts[e], win)
                xs_ref[e, 0, sl, :] = (xs_ref[e, 0, sl, :].astype(F32) + res[e * win:(e + 1) * win]).astype(BF16)

            @pl.when(dq == 0)
            def _():
                g = _dot(onehot, affn_ref[0, toks, :])
                for e in range(N_EXPERTS):
                    gacc[e, pl.ds(starts[e], win), :] += g[e * win:(e + 1) * win]
            return carry2

        lax.fori_loop(0, rnd_ref[b * n_tiles + j], round_body, 0)
        return carry

    lax.fori_loop(0, n_tiles, tile_body, 0)

    @pl.when(dq == 0)
    def _():
        lane = lax.broadcasted_iota(I32, (cap, LANES), 1)
        for e in range(N_EXPERTS):
            mine = (lane == e) | (lane == N_EXPERTS + e) | (lane == 2 * N_EXPERTS + e)
            col = jnp.sum(jnp.where(mine, gacc[e], 0.0), axis=1, keepdims=True)
            gate_ref[e, 0] = jnp.broadcast_to(col, (cap, LANES))


def _gather(cum, rounds, slot, h2, affn, cap):
    b, n, d = h2.shape
    n_tiles = n // SEL_TILE
    win = min(EC_WINDOW, cap)
    dn = 512
    slot_tiles = slot.reshape(b, N_EXPERTS, n_tiles, SEL_TILE).transpose(0, 2, 1, 3)
    grid_spec = pltpu.PrefetchScalarGridSpec(
        num_scalar_prefetch=2,
        grid=(b, d // dn),
        in_specs=[
            pl.BlockSpec((1, n_tiles, N_EXPERTS, SEL_TILE), lambda i, q, c, r: (i, 0, 0, 0)),
            pl.BlockSpec((1, n, dn), lambda i, q, c, r: (i, 0, q)),
            pl.BlockSpec((1, n, LANES), lambda i, q, c, r: (i, 0, 0)),
        ],
        out_specs=[pl.BlockSpec((N_EXPERTS, 1, cap, dn), lambda i, q, c, r: (0, i, 0, q)),
                   pl.BlockSpec((N_EXPERTS, 1, cap, LANES), lambda i, q, c, r: (0, i, 0, 0))],
        scratch_shapes=[pltpu.VMEM((N_EXPERTS, cap, LANES), F32)],
    )
    return pl.pallas_call(
        functools.partial(_gather_kernel, cap=cap, win=win, n_tiles=n_tiles),
        grid_spec=grid_spec,
        out_shape=[jax.ShapeDtypeStruct((N_EXPERTS, b, cap, d), BF16),
                   jax.ShapeDtypeStruct((N_EXPERTS, b, cap, LANES), F32)],
        compiler_params=_params(("parallel", "arbitrary")),
        name="ec_gather",
    )(cum, rounds, slot_tiles, h2, affn)


def _expert_up_kernel(xs_ref, wg_ref, wu_ref, h_ref, wg_bf, wu_bf):
    @pl.when(pl.program_id(2) == 0)
    def _():
        wg_bf[...] = wg_ref[0, 0].astype(BF16)
        wu_bf[...] = wu_ref[0, 0].astype(BF16)

    x = xs_ref[0]
    a = _dot(x, wg_bf[...])
    u = _dot(x, wu_bf[...])
    h_ref[0] = (a * jax.nn.sigmoid(a) * u).astype(BF16)


def _expert_down_kernel(h_ref, wd_ref, gate_ref, ys_ref, wd_bf):
    @pl.when(pl.program_id(2) == 0)
    def _():
        wd_bf[...] = wd_ref[0, 0].astype(BF16)

    y = _dot(h_ref[0], wd_bf[...])
    g = gate_ref[0]
    ys_ref[0] = (y * jnp.concatenate([g] * (y.shape[1] // LANES), axis=1)).astype(BF16)


def _experts(xs, gate, wg, wu, wd, layer):
    e, m, d = xs.shape
    ff = wg.shape[3]
    tm = min(2048, m)
    tf = 512
    tn = 1024
    sem = ("parallel", "parallel", "arbitrary")
    hidden = pl.pallas_call(
        _expert_up_kernel,
        grid=(e, ff // tf, m // tm),
        in_specs=[
            pl.BlockSpec((1, tm, d), lambda ei, f, i: (ei, i, 0)),
            pl.BlockSpec((1, 1, d, tf), lambda ei, f, i: (layer, ei, 0, f)),
            pl.BlockSpec((1, 1, d, tf), lambda ei, f, i: (layer, ei, 0, f)),
        ],
        out_specs=pl.BlockSpec((1, tm, tf), lambda ei, f, i: (ei, i, f)),
        out_shape=jax.ShapeDtypeStruct((e, m, ff), BF16),
        scratch_shapes=[pltpu.VMEM((d, tf), BF16), pltpu.VMEM((d, tf), BF16)],
        compiler_params=_params(sem),
        name="expert_up",
    )(xs, wg, wu)
    td = min(1024, m)
    return pl.pallas_call(
        _expert_down_kernel,
        grid=(e, d // tn, m // td),
        in_specs=[
            pl.BlockSpec((1, td, ff), lambda ei, j, i: (ei, i, 0)),
            pl.BlockSpec((1, 1, ff, tn), lambda ei, j, i: (layer, ei, 0, j)),
            pl.BlockSpec((1, td, LANES), lambda ei, j, i: (ei, i, 0)),
        ],
        out_specs=pl.BlockSpec((1, td, tn), lambda ei, j, i: (ei, i, j)),
        out_shape=jax.ShapeDtypeStruct((e, m, d), BF16),
        scratch_shapes=[pltpu.VMEM((ff, tn), BF16)],
        compiler_params=_params(sem),
        name="expert_down",
    )(hidden, wd, gate)


def _scatter_kernel(cum_ref, rnd_ref, slott_ref, ys_ref, x_ref, g_ref, o_ref, *, cap, win, n_tiles):
    b = pl.program_id(0)
    j = pl.program_id(1)
    st = slott_ref[0]
    per_block = LANES // win
    lane = lax.broadcasted_iota(I32, (SEL_TILE, LANES), 1)
    lane_f = lane.astype(F32)
    group = lane // win

    def round_body(r, acc):
        blocks, pieces = [], []
        for k in range(N_EXPERTS // per_block):
            target = None
            for i in range(per_block):
                e = k * per_block + i
                first, start = _window(cum_ref, b, e, j, r, n_tiles, cap, win)
                col = st[:, e:e + 1]
                tgt = jnp.where(col >= first.astype(F32), col - (start - i * win).astype(F32), -1.0)
                target = tgt if target is None else jnp.where(group == i, tgt, target)
                pieces.append(ys_ref[e, 0, pl.ds(start, win), :])
            blocks.append(jnp.where(lane_f == target, 1.0, 0.0).astype(BF16))
        onehot = jnp.concatenate(blocks, axis=1)
        stacked = jnp.concatenate(pieces, axis=0)
        return acc + _dot(onehot, stacked)

    acc = lax.fori_loop(0, rnd_ref[b * n_tiles + j], round_body, jnp.zeros(o_ref.shape[1:], F32))
    o_ref[0] = x_ref[0] + g_ref[0] * acc


def _scatter(cum, rounds, slott, ys, x1, gate, cap):
    b, n, d = x1.shape
    n_tiles = n // SEL_TILE
    win = min(EC_WINDOW, cap)
    grid_spec = pltpu.PrefetchScalarGridSpec(
        num_scalar_prefetch=2,
        grid=(b, n_tiles),
        in_specs=[
            pl.BlockSpec((1, SEL_TILE, LANES), lambda i, t, c, r: (i, t, 0)),
            pl.BlockSpec((N_EXPERTS, 1, cap, d), lambda i, t, c, r: (0, i, 0, 0), pipeline_mode=pl.Buffered(1)),
            pl.BlockSpec((1, SEL_TILE, d), lambda i, t, c, r: (i, t, 0)),
            pl.BlockSpec((1, 1, d), lambda i, t, c, r: (i, 0, 0)),
        ],
        out_specs=pl.BlockSpec((1, SEL_TILE, d), lambda i, t, c, r: (i, t, 0)),
    )
    return pl.pallas_call(
        functools.partial(_scatter_kernel, cap=cap, win=win, n_tiles=n_tiles),
        grid_spec=grid_spec,
        out_shape=jax.ShapeDtypeStruct((b, n, d), F32),
        compiler_params=_params(("parallel", "arbitrary")),
        name="ec_scatter",
    )(cum, rounds, slott, ys, x1, gate)


def _mod_chunks(mod_rows):
    b = mod_rows.shape[0]
    return [mod_rows[:, i * D_MODEL:(i + 1) * D_MODEL].reshape(b, 1, D_MODEL) for i in range(N_MOD)]


def _token_mixer(h, zmix, att, lw, dft):
    b, n, d = h.shape
    y_conv, y_pool = _stencil(zmix, lw["conv_w"], lw["pool_w"], lw["pool_scale"])
    y_f = _fourier(zmix, *dft)
    branches = [y.reshape(b * n, MIX_W) for y in (y_conv, att, y_f, y_pool)]
    merged = _merge(h.reshape(b * n, d), branches, lw["w_in"], lw["w_branch"])
    return merged.reshape(b, n, d)


def _ffn(x1, h2, aff, affn, g2, lw):
    b, n, d = x1.shape
    cap = EC_CAPACITY * n // N_EXPERTS
    slot, slott = _select(aff, cap)
    per_tile = jnp.sum((slot >= 0).reshape(b, N_EXPERTS, n // SEL_TILE, SEL_TILE), axis=-1, dtype=I32)
    cum = jnp.concatenate([jnp.zeros((b, N_EXPERTS, 1), I32), jnp.cumsum(per_tile, axis=-1)], axis=-1)
    win = min(EC_WINDOW, cap)
    span = cum[..., 1:] - (cum[..., :-1] // ROW_ALIGN) * ROW_ALIGN
    rounds = jnp.max((span + win - 1) // win, axis=1).reshape(-1)
    cum = cum.reshape(-1)
    xs, gate = _gather(cum, rounds, slot, h2, affn, cap)
    ys = _experts(xs.reshape(N_EXPERTS, b * cap, d), gate.reshape(N_EXPERTS, b * cap, LANES),
                  lw["w_gate"], lw["w_up"], lw["w_down"], lw["layer"])
    return _scatter(cum, rounds, slott, ys.reshape(N_EXPERTS, b, cap, d), x1, g2, cap)


def kernel(x, c, ctx, c_ctx, w_ada, b_ada, norm1_w, norm2_w, w_in, conv_w, q_norm_w, k_norm_w,
           na_rpb, pool_w, pool_scale, w_branch, w_out, w_router, w_exp_gate, w_exp_up, w_exp_down):
    bsz, n, d = x.shape
    l_ctx = ctx.shape[1]
    depth = w_ada.shape[0]
    n_rows = n // GRID_W

    c16 = jnp.zeros((16, d), F32).at[:bsz].set(c).at[bsz].set(c_ctx)
    mod_all = _ada(c16, w_ada, b_ada)

    hd = np.arange(MIX_W) // HEAD_DIM
    bd = jnp.asarray(hd[:, None] == hd[None, :], dtype=BF16)
    dft_lat = _dft_mats(n, 1.0) + _dft_mats(FOURIER_GC, (n * FOURIER_GC) ** -0.5)
    dft_ctx = _dft_mats(l_ctx, 1.0) + _dft_mats(FOURIER_GC, (l_ctx * FOURIER_GC) ** -0.5)

    for l in range(depth):
        last = l == depth - 1
        lw = {
            "w_in": w_in[l].astype(BF16), "conv_w": conv_w[l], "pool_w": pool_w[l],
            "pool_scale": pool_scale[l], "w_branch": w_branch[l].astype(BF16),
            "w_gate": w_exp_gate, "w_up": w_exp_up, "w_down": w_exp_down, "layer": l,
        }
        w_out_bf = w_out[l].astype(BF16)
        wr_pad = jnp.zeros((d, LANES), F32).at[:, :N_EXPERTS].set(w_router[l])
        wr_hi = wr_pad.astype(BF16)
        wr_cat = jnp.concatenate([wr_hi, (wr_pad - wr_hi.astype(F32)).astype(BF16)], axis=1)
        q_w = jnp.tile(q_norm_w[l], N_HEADS).reshape(1, MIX_W)
        k_w = jnp.tile(k_norm_w[l], N_HEADS).reshape(1, MIX_W)
        tb = _na_bias_table(na_rpb[l], n_rows)
        sh1, sc1, g1, sh2, sc2, g2 = _mod_chunks(mod_all[l, :bsz])
        mc = _mod_chunks(jnp.broadcast_to(mod_all[l, bsz:bsz + 1], (bsz, N_MOD * d)))

        hc = _prenorm(ctx, norm1_w[l], mc[0], mc[1])
        zc_mix = _in_proj(hc.reshape(bsz * l_ctx, d), lw["w_in"], 0, MIX_COLS).reshape(bsz, l_ctx, MIX_COLS)

        h = _prenorm(x, norm1_w[l], sh1, sc1)
        zmix = _in_proj(h.reshape(bsz * n, d), lw["w_in"], 0, MIX_COLS).reshape(bsz, n, MIX_COLS)
        att = _neighbourhood_attention(zmix, zc_mix, q_w, k_w, bd, tb)
        merged = _token_mixer(h, zmix, att, lw, dft_lat)
        x1, h2, aff, affn = _out_proj(merged, w_out_bf, x, g1, norm2_w[l], sh2, sc2, wr_cat)
        x = _ffn(x1, h2, aff, affn, g2, lw)

        if not last:
            att_c = _context_attention(zc_mix, q_w, k_w, bd)
            merged_c = _token_mixer(hc, zc_mix, att_c, lw, dft_ctx)
            c1, hc2, aff_c, affn_c = _out_proj(merged_c, w_out_bf, ctx, mc[2], norm2_w[l], mc[3], mc[4], wr_cat)
            ctx = _ffn(c1, hc2, aff_c, affn_c, mc[5], lw)
    return x
```

```python
import functools

import numpy as np
import jax
import jax.numpy as jnp
from jax import lax
from jax.experimental import pallas as pl
from jax.experimental.pallas import tpu as pltpu

F32 = jnp.float32
BF16 = jnp.bfloat16
I32 = jnp.int32
HIGHEST = lax.Precision.HIGHEST

D_MODEL = 2048
GRID_W = 64
MIX_W = D_MODEL // 4
N_HEADS = 8
HEAD_DIM = MIX_W // N_HEADS
WIN_H = 8
WIN_W = 16
ATT_SCALE = HEAD_DIM ** -0.5
POOL_WINDOWS = (2, 4, 8, 16)
POOL_GC = MIX_W // len(POOL_WINDOWS)
FOURIER_GC = MIX_W // 4
N_EXPERTS = 16
EC_CAPACITY = 2
N_MOD = 6
EPS = 1e-6
NEG_INF = -1e30
MIX_COLS = 8 * MIX_W
LANES = 128
HALO = 16
SEL_TILE = 256
ROW_ALIGN = 16
EC_WINDOW = 64
ROWS_PER_PASS = 2
VMEM_LIMIT = 56 * 1024 * 1024

COL_XA, COL_GB, COL_GC, COL_Q, COL_K, COL_V, COL_F, COL_P = range(8)


def _params(sem, vmem=VMEM_LIMIT):
    return pltpu.CompilerParams(dimension_semantics=sem, vmem_limit_bytes=vmem)


def _dot(a, b):
    return jnp.dot(a, b, preferred_element_type=F32)


def _dot_nt(a, b):
    return lax.dot_general(a, b, (((1,), (1,)), ((), ())), preferred_element_type=F32)


def _ada_kernel(c_ref, w_ref, b_ref, o_ref):
    c = c_ref[...]
    s = c * jax.nn.sigmoid(c)
    o_ref[0] = jnp.dot(s, w_ref[0], precision=HIGHEST, preferred_element_type=F32) + b_ref[0]


def _ada(c16, w_ada, b_ada):
    depth, d, ncol = w_ada.shape
    tn = 1024
    return pl.pallas_call(
        _ada_kernel,
        grid=(depth, ncol // tn),
        in_specs=[
            pl.BlockSpec((16, d), lambda l, j: (0, 0)),
            pl.BlockSpec((1, d, tn), lambda l, j: (l, 0, j)),
            pl.BlockSpec((1, 1, tn), lambda l, j: (l, 0, j)),
        ],
        out_specs=pl.BlockSpec((1, 16, tn), lambda l, j: (l, 0, j)),
        out_shape=jax.ShapeDtypeStruct((depth, 16, ncol), F32),
        compiler_params=_params(("parallel", "parallel")),
        name="ada",
    )(c16, w_ada, b_ada.reshape(depth, 1, ncol))


def _prenorm_math(x, w, shift, scale):
    ms = jnp.mean(x * x, axis=-1, keepdims=True)
    y = x * lax.rsqrt(ms + EPS) * w
    return y * (1.0 + scale) + shift


def _prenorm_kernel(x_ref, w_ref, sh_ref, sc_ref, o_ref):
    o_ref[0] = _prenorm_math(x_ref[0], w_ref[...], sh_ref[0], sc_ref[0]).astype(BF16)


def _prenorm(x, w, shift, scale):
    b, n, d = x.shape
    tm = min(512, n)
    return pl.pallas_call(
        _prenorm_kernel,
        grid=(b, n // tm),
        in_specs=[
            pl.BlockSpec((1, tm, d), lambda i, j: (i, j, 0)),
            pl.BlockSpec((1, d), lambda i, j: (0, 0)),
            pl.BlockSpec((1, 1, d), lambda i, j: (i, 0, 0)),
            pl.BlockSpec((1, 1, d), lambda i, j: (i, 0, 0)),
        ],
        out_specs=pl.BlockSpec((1, tm, d), lambda i, j: (i, j, 0)),
        out_shape=jax.ShapeDtypeStruct((b, n, d), BF16),
        compiler_params=_params(("parallel", "parallel")),
        name="prenorm",
    )(x, w.reshape(1, d), shift, scale)


def _mm_kernel(a_ref, w_ref, o_ref, w_bf):
    @pl.when(pl.program_id(1) == 0)
    def _():
        w_bf[...] = w_ref[0].astype(BF16)

    o_ref[...] = _dot(a_ref[...], w_bf[...]).astype(o_ref.dtype)


def _in_proj(h2d, w, layer, ncols):
    m, k = h2d.shape
    tm = min(1024, m)
    tn = 1024
    return pl.pallas_call(
        _mm_kernel,
        grid=(ncols // tn, m // tm),
        in_specs=[
            pl.BlockSpec((tm, k), lambda j, i: (i, 0)),
            pl.BlockSpec((1, k, tn), lambda j, i: (layer, 0, j)),
        ],
        out_specs=pl.BlockSpec((tm, tn), lambda j, i: (i, j)),
        out_shape=jax.ShapeDtypeStruct((m, ncols), BF16),
        scratch_shapes=[pltpu.VMEM((k, tn), BF16)],
        compiler_params=_params(("parallel", "arbitrary")),
        name="in_proj",
    )(h2d, w)


def _stencil_kernel(xa_ref, gb_ref, gc_ref, zp_ref, xa_p, xa_n, gc_p, gc_n, zp_p, zp_n,
                    cw_ref, pw_ref, ps_ref, oc_ref, op_ref, *, tm, n_seq):
    i = pl.program_id(1)
    nt = pl.num_programs(1)
    rows = tm + 2 * HALO

    def ext(main_ref, prev_ref, next_ref):
        main = main_ref[0].astype(F32)
        prev = jnp.where(i > 0, prev_ref[0].astype(F32), 0.0)
        nxt = jnp.where(i < nt - 1, next_ref[0].astype(F32), 0.0)
        return jnp.concatenate([prev, main, nxt], axis=0)

    def shifted(a, d):
        if d == 0:
            return a[HALO:HALO + tm]
        return pltpu.roll(a, (-d) % rows, axis=0)[HALO:HALO + tm]

    u = ext(gc_ref, gc_p, gc_n) * ext(xa_ref, xa_p, xa_n)
    cw = cw_ref[...]
    y = cw[0:1] * shifted(u, -1) + cw[1:2] * shifted(u, 0) + cw[2:3] * shifted(u, 1)
    oc_ref[0] = (gb_ref[0].astype(F32) * y).astype(BF16)

    t = (i * tm + lax.broadcasted_iota(I32, (tm, 1), 0)).astype(F32)
    zp = ext(zp_ref, zp_p, zp_n)
    for g, win in enumerate(POOL_WINDOWS):
        ug = zp[:, g * POOL_GC:(g + 1) * POOL_GC]
        acc = ug + pltpu.roll(ug, 1, axis=0)
        half = 1
        while 2 * half < win:
            acc = pltpu.roll(acc, rows - half, axis=0) + pltpu.roll(acc, half, axis=0)
            half *= 2
        wsum = acc[HALO:HALO + tm]
        lo = jnp.maximum(t - (win // 2), 0.0)
        hi = jnp.minimum(t + (win - win // 2 - 1), float(n_seq - 1))
        pooled = wsum / (hi - lo + 1.0) - ug[HALO:HALO + tm]
        yg = _dot(pooled.astype(BF16), pw_ref[g])
        op_ref[0, :, g * POOL_GC:(g + 1) * POOL_GC] = (
            yg * ps_ref[:, g * POOL_GC:(g + 1) * POOL_GC]).astype(BF16)


def _stencil(zmix, conv_w, pool_w, pool_scale):
    b, n, _ = zmix.shape
    tm = min(512, n)
    hb = tm // HALO
    last = n // HALO - 1

    def main(col):
        return pl.BlockSpec((1, tm, MIX_W), lambda bi, i: (bi, i, col))

    def prev(col):
        return pl.BlockSpec((1, HALO, MIX_W), lambda bi, i: (bi, jnp.maximum(i * hb - 1, 0), col))

    def nxt(col):
        return pl.BlockSpec((1, HALO, MIX_W), lambda bi, i: (bi, jnp.minimum((i + 1) * hb, last), col))

    out_spec = pl.BlockSpec((1, tm, MIX_W), lambda bi, i: (bi, i, 0))
    return pl.pallas_call(
        functools.partial(_stencil_kernel, tm=tm, n_seq=n),
        grid=(b, n // tm),
        in_specs=[main(COL_XA), main(COL_GB), main(COL_GC), main(COL_P),
                  prev(COL_XA), nxt(COL_XA), prev(COL_GC), nxt(COL_GC), prev(COL_P), nxt(COL_P),
                  pl.BlockSpec((3, MIX_W), lambda bi, i: (0, 0)),
                  pl.BlockSpec((len(POOL_WINDOWS), POOL_GC, POOL_GC), lambda bi, i: (0, 0, 0)),
                  pl.BlockSpec((1, MIX_W), lambda bi, i: (0, 0))],
        out_specs=[out_spec, out_spec],
        out_shape=[jax.ShapeDtypeStruct((b, n, MIX_W), BF16)] * 2,
        compiler_params=_params(("parallel", "parallel")),
        name="stencil",
    )(zmix, zmix, zmix, zmix, zmix, zmix, zmix, zmix, zmix, zmix,
      conv_w, pool_w.astype(BF16), pool_scale.reshape(1, MIX_W))


def _fourier_kernel(cn_ref, sn_ref, u_ref, ur_ref, um_ref, cc_ref, sc_ref, o_ref):
    u = u_ref[0].astype(F32)
    ur = ur_ref[0].astype(F32)
    half, tk = u.shape[0], cn_ref.shape[0]
    t = lax.broadcasted_iota(I32, (half, 1), 0)
    even = jnp.where(t == 0, u, u + ur).astype(BF16)
    odd = (u - ur).astype(BF16)
    k = pl.program_id(0) * tk + lax.broadcasted_iota(I32, (tk, 1), 0)
    sign = (1 - 2 * jnp.bitwise_and(k, 1)).astype(F32)
    p = (_dot(cn_ref[...], even) + sign * um_ref[0, 0:1, :].astype(F32)).astype(BF16)
    q = _dot(sn_ref[...], odd).astype(BF16)
    for g in range(MIX_W // FOURIER_GC):
        sl = slice(g * FOURIER_GC, (g + 1) * FOURIER_GC)
        o_ref[0, :, sl] = (_dot(p[:, sl], cc_ref[...]) - _dot(q[:, sl], sc_ref[...])).astype(BF16)


def _dft_mats(n, scale, ncols=None):
    ncols = n if ncols is None else ncols
    k = jnp.arange(n, dtype=I32)[:, None]

    def table(count, period):
        r = (k * jnp.arange(count, dtype=I32)[None, :]) % period
        ang = r.astype(F32) * np.float32(2.0 * np.pi / period)
        return jnp.cos(ang), jnp.sin(ang)

    inner = 64
    if n <= inner or n % inner or ncols % inner:
        c, s = table(ncols, n)
    else:
        ca, sa = table(ncols // inner, n // inner)
        cb, sb = table(inner, n)
        c = (ca[:, :, None] * cb[:, None, :] - sa[:, :, None] * sb[:, None, :]).reshape(n, ncols)
        s = (sa[:, :, None] * cb[:, None, :] + ca[:, :, None] * sb[:, None, :]).reshape(n, ncols)
    return (c * scale).astype(BF16), (s * scale).astype(BF16)


def _fourier(zmix, cn, sn, cc, sc):
    b, n, _ = zmix.shape
    half = n // 2
    tk = min(512, n)
    zf = zmix[:, :, COL_F * MIX_W:(COL_F + 1) * MIX_W]
    u_rev = jnp.roll(jnp.flip(zf, axis=1), 1, axis=1)[:, :half]
    return pl.pallas_call(
        _fourier_kernel,
        grid=(n // tk, b),
        in_specs=[
            pl.BlockSpec((tk, half), lambda k, bi: (k, 0)),
            pl.BlockSpec((tk, half), lambda k, bi: (k, 0)),
            pl.BlockSpec((1, half, MIX_W), lambda k, bi: (bi, 0, COL_F)),
            pl.BlockSpec((1, half, MIX_W), lambda k, bi: (bi, 0, 0)),
            pl.BlockSpec((1, HALO, MIX_W), lambda k, bi: (bi, half // HALO, COL_F)),
            pl.BlockSpec((FOURIER_GC, FOURIER_GC), lambda k, bi: (0, 0)),
            pl.BlockSpec((FOURIER_GC, FOURIER_GC), lambda k, bi: (0, 0)),
        ],
        out_specs=pl.BlockSpec((1, tk, MIX_W), lambda k, bi: (bi, k, 0)),
        out_shape=jax.ShapeDtypeStruct((b, n, MIX_W), BF16),
        compiler_params=_params(("parallel", "parallel")),
        name="fourier",
    )(cn, sn, zmix, u_rev, zmix, cc, sc)


def _head_rms(x, w, bd):
    x2 = x * x
    hi = x2.astype(BF16)
    lo = (x2 - hi.astype(F32)).astype(BF16)
    ms = (_dot(hi, bd) + _dot(lo, bd)) * (1.0 / HEAD_DIM)
    return x * lax.rsqrt(ms + EPS) * w


def _pair_stack(q2):
    lane = lax.broadcasted_iota(I32, q2.shape, 1)
    zero = jnp.zeros_like(q2)
    return jnp.concatenate([jnp.where(lane < HEAD_DIM, q2, zero),
                            jnp.where(lane >= HEAD_DIM, q2, zero)], axis=0)


def _pair_unstack(o2, m):
    lane = lax.broadcasted_iota(I32, (m, LANES), 1)
    return jnp.where(lane < HEAD_DIM, o2[:m], o2[m:])


def _na_kernel(q_ref, k_ref, v_ref, kc_ref, vc_ref, qw_ref, kw_ref, bd_ref, tb_ref, o_ref,
               kn_scr, kcn_scr, qn_scr, *, rt, n_rows):
    rb = pl.program_id(1)
    n_seq = n_rows * GRID_W
    chunk = 512

    @pl.when(rb == 0)
    def _():
        def body(c, carry):
            sl = pl.ds(pl.multiple_of(c * chunk, chunk), chunk)
            kn_scr[sl, :] = _head_rms(k_ref[0, sl, :].astype(F32), kw_ref[...], bd_ref[...]).astype(BF16)
            return carry
        lax.fori_loop(0, n_seq // chunk, body, 0)
        kcn_scr[...] = _head_rms(kc_ref[0].astype(F32), kw_ref[...], bd_ref[...]).astype(BF16)

    qn_scr[...] = (_head_rms(q_ref[0].astype(F32), qw_ref[...], bd_ref[...]) * ATT_SCALE).astype(BF16)

    kh = min(WIN_H, n_rows)
    nloc = kh * GRID_W

    def row_body(jj, carry):
        units = []
        for i in range(ROWS_PER_PASS):
            j = jj * ROWS_PER_PASS + i
            r = rb * rt + j
            rs = jnp.clip(r - kh // 2, 0, n_rows - kh)
            q_rows = pl.ds(pl.multiple_of(j * GRID_W, GRID_W), GRID_W)
            k_rows = pl.ds(pl.multiple_of(rs * GRID_W, GRID_W), nloc)
            for p in range(N_HEADS // 2):
                units.append((q_rows, k_rows, rs - r + (WIN_H - 1), p, slice(p * LANES, (p + 1) * LANES)))
        qs = [_pair_stack(qn_scr[q_rows, cols]) for q_rows, _, _, _, cols in units]
        s_loc = [_dot_nt(q, kn_scr[k_rows, cols]) for q, (_, k_rows, _, _, cols) in zip(qs, units)]
        s_ctx = [_dot_nt(q, kcn_scr[:, cols]) for q, (_, _, _, _, cols) in zip(qs, units)]
        s_loc = [s + tb_ref[dr0, p] for s, (_, _, dr0, p, _) in zip(s_loc, units)]
        m = [jnp.maximum(jnp.max(a, axis=1, keepdims=True), jnp.max(c, axis=1, keepdims=True))
             for a, c in zip(s_loc, s_ctx)]
        p_loc = [jnp.exp(a - mx) for a, mx in zip(s_loc, m)]
        p_ctx = [jnp.exp(c - mx) for c, mx in zip(s_ctx, m)]
        denom = [jnp.sum(a, axis=1, keepdims=True) + jnp.sum(c, axis=1, keepdims=True) for a, c in zip(p_loc, p_ctx)]
        o2 = [_dot(a.astype(BF16), v_ref[0, k_rows, cols]) + _dot(c.astype(BF16), vc_ref[0, :, cols])
              for a, c, (_, k_rows, _, _, cols) in zip(p_loc, p_ctx, units)]
        for o, d, (q_rows, _, _, _, cols) in zip(o2, denom, units):
            o_ref[0, q_rows, cols] = _pair_unstack(o / d, GRID_W).astype(BF16)
        return carry

    lax.fori_loop(0, rt // ROWS_PER_PASS, row_body, 0)


def _na_bias_table(rpb, n_rows):
    kh = min(WIN_H, n_rows)
    c = np.arange(GRID_W)
    cs = np.clip(c - WIN_W // 2, 0, GRID_W - WIN_W)
    kc = np.arange(GRID_W)
    ok = (kc[None, :] >= cs[:, None]) & (kc[None, :] < cs[:, None] + WIN_W)
    edge = GRID_W - WIN_W
    padded = jnp.pad(rpb.astype(F32), ((0, 0), (0, 0), (edge, edge)))
    toeplitz = jnp.stack([padded[:, :, GRID_W - 1 - q:2 * GRID_W - 1 - q] for q in range(GRID_W)], axis=2)
    masked = jnp.where(ok[None, None], toeplitz, NEG_INF)
    t = jnp.stack([masked[:, d0:d0 + kh] for d0 in range(WIN_H)], axis=0)
    t = jnp.transpose(t, (0, 1, 3, 2, 4))
    return t.reshape(WIN_H, N_HEADS // 2, 2 * GRID_W, kh * GRID_W)


def _neighbourhood_attention(zmix, zc_mix, q_w, k_w, bd, tb):
    b, n, _ = zmix.shape
    l = zc_mix.shape[1]
    n_rows = n // GRID_W
    rt = 8
    nloc = min(WIN_H, n_rows) * GRID_W
    return pl.pallas_call(
        functools.partial(_na_kernel, rt=rt, n_rows=n_rows),
        grid=(b, n_rows // rt),
        in_specs=[
            pl.BlockSpec((1, rt * GRID_W, MIX_W), lambda bi, r: (bi, r, COL_Q)),
            pl.BlockSpec((1, n, MIX_W), lambda bi, r: (bi, 0, COL_K)),
            pl.BlockSpec((1, n, MIX_W), lambda bi, r: (bi, 0, COL_V)),
            pl.BlockSpec((1, l, MIX_W), lambda bi, r: (bi, 0, COL_K)),
            pl.BlockSpec((1, l, MIX_W), lambda bi, r: (bi, 0, COL_V)),
            pl.BlockSpec((1, MIX_W), lambda bi, r: (0, 0)),
            pl.BlockSpec((1, MIX_W), lambda bi, r: (0, 0)),
            pl.BlockSpec((MIX_W, MIX_W), lambda bi, r: (0, 0)),
            pl.BlockSpec((WIN_H, N_HEADS // 2, 2 * GRID_W, nloc), lambda bi, r: (0, 0, 0, 0),
                         pipeline_mode=pl.Buffered(1)),
        ],
        out_specs=pl.BlockSpec((1, rt * GRID_W, MIX_W), lambda bi, r: (bi, r, 0)),
        out_shape=jax.ShapeDtypeStruct((b, n, MIX_W), BF16),
        scratch_shapes=[pltpu.VMEM((n, MIX_W), BF16), pltpu.VMEM((l, MIX_W), BF16),
                        pltpu.VMEM((rt * GRID_W, MIX_W), BF16)],
        compiler_params=_params(("parallel", "arbitrary")),
        name="nbr_attention",
    )(zmix, zmix, zmix, zc_mix, zc_mix, q_w, k_w, bd, tb)


def _ctx_attn_kernel(q_ref, k_ref, v_ref, qw_ref, kw_ref, bd_ref, o_ref):
    l = q_ref.shape[1]
    qn = (_head_rms(q_ref[0].astype(F32), qw_ref[...], bd_ref[...]) * ATT_SCALE).astype(BF16)
    kn = _head_rms(k_ref[0].astype(F32), kw_ref[...], bd_ref[...]).astype(BF16)
    for p in range(N_HEADS // 2):
        cols = slice(p * LANES, (p + 1) * LANES)
        qs = _pair_stack(qn[:, cols])
        s = _dot_nt(qs, kn[:, cols])
        m = jnp.max(s, axis=1, keepdims=True)
        e = jnp.exp(s - m)
        o2 = _dot(e.astype(BF16), v_ref[0, :, cols]) / jnp.sum(e, axis=1, keepdims=True)
        o_ref[0, :, cols] = _pair_unstack(o2, l).astype(BF16)


def _context_attention(zc_mix, q_w, k_w, bd):
    b, l, _ = zc_mix.shape
    vec = pl.BlockSpec((1, MIX_W), lambda bi: (0, 0))
    return pl.pallas_call(
        _ctx_attn_kernel,
        grid=(b,),
        in_specs=[
            pl.BlockSpec((1, l, MIX_W), lambda bi: (bi, 0, COL_Q)),
            pl.BlockSpec((1, l, MIX_W), lambda bi: (bi, 0, COL_K)),
            pl.BlockSpec((1, l, MIX_W), lambda bi: (bi, 0, COL_V)),
            vec, vec,
            pl.BlockSpec((MIX_W, MIX_W), lambda bi: (0, 0)),
        ],
        out_specs=pl.BlockSpec((1, l, MIX_W), lambda bi: (bi, 0, 0)),
        out_shape=jax.ShapeDtypeStruct((b, l, MIX_W), BF16),
        compiler_params=_params(("parallel",)),
        name="ctx_attention",
    )(zc_mix, zc_mix, zc_mix, q_w, k_w, bd)


def _merge_kernel(h_ref, y0, y1, y2, y3, g0, g1, g2, g3, wb_ref, o_ref, wg_bf, wb_bf):
    @pl.when(pl.program_id(1) == 0)
    def _():
        for i, g in enumerate((g0, g1, g2, g3)):
            wg_bf[i] = g[0].astype(BF16)
        wb_bf[...] = wb_ref[0].astype(BF16)

    h = h_ref[...]
    acc = None
    for i, y in enumerate((y0, y1, y2, y3)):
        gate = jax.nn.sigmoid(_dot(h, wg_bf[i]))
        term = gate * _dot(y[...], wb_bf[i])
        acc = term if acc is None else acc + term
    o_ref[...] = acc.astype(BF16)


def _merge(h2d, branches, w_in, w_branch, layer):
    m, d = h2d.shape
    tm = min(1024, m)
    tn = 256
    gate0 = MIX_COLS // tn

    def gspec(i):
        return pl.BlockSpec((1, d, tn), lambda n, r: (layer, 0, gate0 + i * (d // tn) + n))

    yspec = pl.BlockSpec((tm, MIX_W), lambda n, r: (r, 0))
    return pl.pallas_call(
        _merge_kernel,
        grid=(d // tn, m // tm),
        in_specs=[pl.BlockSpec((tm, d), lambda n, r: (r, 0)), yspec, yspec, yspec, yspec,
                  gspec(0), gspec(1), gspec(2), gspec(3),
                  pl.BlockSpec((1, 4, MIX_W, tn), lambda n, r: (layer, 0, 0, n))],
        out_specs=pl.BlockSpec((tm, tn), lambda n, r: (r, n)),
        out_shape=jax.ShapeDtypeStruct((m, d), BF16),
        scratch_shapes=[pltpu.VMEM((4, d, tn), BF16), pltpu.VMEM((4, MIX_W, tn), BF16)],
        compiler_params=_params(("parallel", "arbitrary")),
        name="merge",
    )(h2d, *branches, w_in, w_in, w_in, w_in, w_branch)


def _out_proj_kernel(m_ref, w_ref, x_ref, g_ref, nw_ref, sh_ref, sc_ref, wrc_ref,
                     x1_ref, h2_ref, aff_ref, affn_ref):
    x1 = x_ref[0] + g_ref[0] * _dot(m_ref[0], w_ref[...])
    x1_ref[0] = x1
    h2 = _prenorm_math(x1, nw_ref[...], sh_ref[0], sc_ref[0])
    h_hi = h2.astype(BF16)
    h2_ref[0] = h_hi
    h_lo = (h2 - h_hi.astype(F32)).astype(BF16)
    both = _dot(h_hi, wrc_ref[...])
    logits = both[:, :LANES] + (both[:, LANES:] + _dot(h_lo, wrc_ref[:, :LANES]))
    lane = lax.broadcasted_iota(I32, logits.shape, 1)
    logits = jnp.where(lane < N_EXPERTS, logits, NEG_INF)
    e = jnp.exp(logits - jnp.max(logits, axis=1, keepdims=True))
    aff = e / jnp.sum(e, axis=1, keepdims=True)
    aff_ref[0] = aff.T[:N_EXPERTS]
    hi = aff.astype(BF16).astype(F32)
    r1 = aff - hi
    mid = r1.astype(BF16).astype(F32)
    lo = (r1 - mid).astype(BF16).astype(F32)
    packed = hi + pltpu.roll(mid, N_EXPERTS, axis=1) + pltpu.roll(lo, 2 * N_EXPERTS, axis=1)
    affn_ref[0] = packed.astype(BF16)


def _out_proj(merged, w_out_bf, x, gate, norm_w, shift, scale, wr_cat):
    b, n, d = x.shape
    tm = min(512, n)
    vec = pl.BlockSpec((1, 1, d), lambda i, j: (i, 0, 0))
    tile = pl.BlockSpec((1, tm, d), lambda i, j: (i, j, 0))
    wr_spec = pl.BlockSpec((d, 2 * LANES), lambda i, j: (0, 0))
    return pl.pallas_call(
        _out_proj_kernel,
        grid=(b, n // tm),
        in_specs=[tile, pl.BlockSpec((d, d), lambda i, j: (0, 0), pipeline_mode=pl.Buffered(1)), tile, vec,
                  pl.BlockSpec((1, d), lambda i, j: (0, 0)), vec, vec, wr_spec],
        out_specs=[tile, tile,
                   pl.BlockSpec((1, N_EXPERTS, tm), lambda i, j: (i, 0, j)),
                   pl.BlockSpec((1, tm, LANES), lambda i, j: (i, j, 0))],
        out_shape=[jax.ShapeDtypeStruct((b, n, d), F32), jax.ShapeDtypeStruct((b, n, d), BF16),
                   jax.ShapeDtypeStruct((b, N_EXPERTS, n), F32), jax.ShapeDtypeStruct((b, n, LANES), BF16)],
        compiler_params=_params(("parallel", "parallel")),
        name="out_proj",
    )(merged, w_out_bf, x, gate, norm_w.reshape(1, d), shift, scale, wr_cat)


def _lane_cumsum(mask_f, ut):
    rows, n = mask_f.shape
    carry = jnp.zeros((rows, 1), F32)
    parts = []
    for j in range(n // LANES):
        c = _dot(mask_f[:, j * LANES:(j + 1) * LANES].astype(BF16), ut) + carry
        parts.append(c)
        carry = c[:, LANES - 1:LANES]
    return jnp.concatenate(parts, axis=1)


def _select_kernel(aff_ref, slot_ref, slott_ref, *, cap):
    aff = aff_ref[0]
    e, n = aff.shape
    bits = lax.bitcast_convert_type(aff, I32)

    def body(_, carry):
        lo, hi = carry
        mid = lo + ((hi - lo) >> 1)
        cnt = jnp.sum(jnp.where(bits >= mid, 1.0, 0.0), axis=1, keepdims=True)
        ge = cnt >= float(cap)
        return jnp.where(ge, mid, lo), jnp.where(ge, hi, mid)

    lo, _ = lax.fori_loop(0, 31, body, (jnp.zeros((e, 1), I32), jnp.full((e, 1), 0x7F800000, I32)))
    ri = lax.broadcasted_iota(I32, (LANES, LANES), 0)
    ci = lax.broadcasted_iota(I32, (LANES, LANES), 1)
    ut = jnp.where(ri <= ci, 1.0, 0.0).astype(BF16)
    gt = bits > lo
    eq = bits == lo
    need = float(cap) - jnp.sum(jnp.where(gt, 1.0, 0.0), axis=1, keepdims=True)
    ceq = _lane_cumsum(jnp.where(eq, 1.0, 0.0), ut)
    sel = gt | (eq & (ceq <= need))
    csel = _lane_cumsum(jnp.where(sel, 1.0, 0.0), ut)
    slot = jnp.where(sel, csel - 1.0, -1.0)
    slot_ref[0] = slot.astype(I32)
    padded = jnp.concatenate([slot, jnp.full((LANES - e, n), -1.0, F32)], axis=0)
    slott_ref[0] = padded.T


def _select(aff, cap):
    b, e, n = aff.shape
    return pl.pallas_call(
        functools.partial(_select_kernel, cap=cap),
        grid=(b,),
        in_specs=[pl.BlockSpec((1, e, n), lambda i: (i, 0, 0))],
        out_specs=[pl.BlockSpec((1, e, n), lambda i: (i, 0, 0)),
                   pl.BlockSpec((1, n, LANES), lambda i: (i, 0, 0))],
        out_shape=[jax.ShapeDtypeStruct((b, e, n), I32), jax.ShapeDtypeStruct((b, n, LANES), F32)],
        compiler_params=_params(("parallel",)),
        name="select",
    )(aff)


def _window(cum_ref, b, e, j, r, n_tiles, cap, win):
    c_lo = cum_ref[(b * N_EXPERTS + e) * (n_tiles + 1) + j]
    first = (c_lo // ROW_ALIGN) * ROW_ALIGN + r * win
    start = pl.multiple_of(jnp.minimum(first, cap - win), ROW_ALIGN)
    return first, start


def _gather_kernel(cum_ref, rnd_ref, slot_ref, h_ref, affn_ref, xs_ref, gate_ref, gacc, *, cap, win, n_tiles):
    b = pl.program_id(0)
    dq = pl.program_id(1)
    xs_ref[...] = jnp.zeros_like(xs_ref)

    @pl.when(dq == 0)
    def _():
        gacc[...] = jnp.zeros_like(gacc)

    rows = lax.broadcasted_iota(I32, (win, SEL_TILE), 0)

    def tile_body(j, carry):
        toks = pl.ds(pl.multiple_of(j * SEL_TILE, SEL_TILE), SEL_TILE)
        slots = slot_ref[0, j]
        h_tile = h_ref[0, toks, :]

        def round_body(r, carry2):
            starts, pieces = [], []
            for e in range(N_EXPERTS):
                first, start = _window(cum_ref, b, e, j, r, n_tiles, cap, win)
                s_row = slots[e:e + 1, :]
                hit = (rows + start == s_row) & (s_row >= first)
                pieces.append(jnp.where(hit, 1.0, 0.0).astype(BF16))
                starts.append(start)
            onehot = jnp.concatenate(pieces, axis=0)
            res = _dot(onehot, h_tile)
            for e in range(N_EXPERTS):
                sl = pl.ds(starts[e], win)
                xs_ref[e, 0, sl, :] = (xs_ref[e, 0, sl, :].astype(F32) + res[e * win:(e + 1) * win]).astype(BF16)

            @pl.when(dq == 0)
            def _():
                g = _dot(onehot, affn_ref[0, toks, :])
                for e in range(N_EXPERTS):
                    gacc[e, pl.ds(starts[e], win), :] += g[e * win:(e + 1) * win]
            return carry2

        lax.fori_loop(0, rnd_ref[b * n_tiles + j], round_body, 0)
        return carry

    lax.fori_loop(0, n_tiles, tile_body, 0)

    @pl.when(dq == 0)
    def _():
        lane = lax.broadcasted_iota(I32, (cap, LANES), 1)
        for e in range(N_EXPERTS):
            mine = (lane == e) | (lane == N_EXPERTS + e) | (lane == 2 * N_EXPERTS + e)
            col = jnp.sum(jnp.where(mine, gacc[e], 0.0), axis=1, keepdims=True)
            gate_ref[e, 0] = jnp.broadcast_to(col, (cap, LANES))


def _gather(cum, rounds, slot, h2, affn, cap):
    b, n, d = h2.shape
    n_tiles = n // SEL_TILE
    win = min(EC_WINDOW, cap)
    dn = 512
    slot_tiles = slot.reshape(b, N_EXPERTS, n_tiles, SEL_TILE).transpose(0, 2, 1, 3)
    grid_spec = pltpu.PrefetchScalarGridSpec(
        num_scalar_prefetch=2,
        grid=(b, d // dn),
        in_specs=[
            pl.BlockSpec((1, n_tiles, N_EXPERTS, SEL_TILE), lambda i, q, c, r: (i, 0, 0, 0)),
            pl.BlockSpec((1, n, dn), lambda i, q, c, r: (i, 0, q)),
            pl.BlockSpec((1, n, LANES), lambda i, q, c, r: (i, 0, 0)),
        ],
        out_specs=[pl.BlockSpec((N_EXPERTS, 1, cap, dn), lambda i, q, c, r: (0, i, 0, q)),
                   pl.BlockSpec((N_EXPERTS, 1, cap, LANES), lambda i, q, c, r: (0, i, 0, 0))],
        scratch_shapes=[pltpu.VMEM((N_EXPERTS, cap, LANES), F32)],
    )
    return pl.pallas_call(
        functools.partial(_gather_kernel, cap=cap, win=win, n_tiles=n_tiles),
        grid_spec=grid_spec,
        out_shape=[jax.ShapeDtypeStruct((N_EXPERTS, b, cap, d), BF16),
                   jax.ShapeDtypeStruct((N_EXPERTS, b, cap, LANES), F32)],
        compiler_params=_params(("parallel", "arbitrary")),
        name="ec_gather",
    )(cum, rounds, slot_tiles, h2, affn)


def _expert_up_kernel(xs_ref, wg_ref, wu_ref, h_ref, wg_bf, wu_bf):
    @pl.when(pl.program_id(2) == 0)
    def _():
        wg_bf[...] = wg_ref[0, 0].astype(BF16)
        wu_bf[...] = wu_ref[0, 0].astype(BF16)

    x = xs_ref[0]
    a = _dot(x, wg_bf[...])
    u = _dot(x, wu_bf[...])
    h_ref[0] = (a * jax.nn.sigmoid(a) * u).astype(BF16)


def _expert_down_kernel(h_ref, wd_ref, gate_ref, ys_ref, wd_bf):
    @pl.when(pl.program_id(2) == 0)
    def _():
        wd_bf[...] = wd_ref[0, 0].astype(BF16)

    y = _dot(h_ref[0], wd_bf[...])
    g = gate_ref[0]
    ys_ref[0] = (y * jnp.concatenate([g] * (y.shape[1] // LANES), axis=1)).astype(BF16)


def _experts(xs, gate, wg, wu, wd, layer):
    e, m, d = xs.shape
    ff = wg.shape[3]
    tm = min(2048, m)
    tf = 512
    tn = 1024
    sem = ("parallel", "parallel", "arbitrary")
    hidden = pl.pallas_call(
        _expert_up_kernel,
        grid=(e, ff // tf, m // tm),
        in_specs=[
            pl.BlockSpec((1, tm, d), lambda ei, f, i: (ei, i, 0)),
            pl.BlockSpec((1, 1, d, tf), lambda ei, f, i: (layer, ei, 0, f)),
            pl.BlockSpec((1, 1, d, tf), lambda ei, f, i: (layer, ei, 0, f)),
        ],
        out_specs=pl.BlockSpec((1, tm, tf), lambda ei, f, i: (ei, i, f)),
        out_shape=jax.ShapeDtypeStruct((e, m, ff), BF16),
        scratch_shapes=[pltpu.VMEM((d, tf), BF16), pltpu.VMEM((d, tf), BF16)],
        compiler_params=_params(sem),
        name="expert_up",
    )(xs, wg, wu)
    td = min(1024, m)
    return pl.pallas_call(
        _expert_down_kernel,
        grid=(e, d // tn, m // td),
        in_specs=[
            pl.BlockSpec((1, td, ff), lambda ei, j, i: (ei, i, 0)),
            pl.BlockSpec((1, 1, ff, tn), lambda ei, j, i: (layer, ei, 0, j)),
            pl.BlockSpec((1, td, LANES), lambda ei, j, i: (ei, i, 0)),
        ],
        out_specs=pl.BlockSpec((1, td, tn), lambda ei, j, i: (ei, i, j)),
        out_shape=jax.ShapeDtypeStruct((e, m, d), BF16),
        scratch_shapes=[pltpu.VMEM((ff, tn), BF16)],
        compiler_params=_params(sem),
        name="expert_down",
    )(hidden, wd, gate)


def _scatter_kernel(cum_ref, rnd_ref, slott_ref, ys_ref, x_ref, g_ref, *rest, cap, win, n_tiles, next_norm):
    if next_norm:
        nw_ref, sh_ref, sc_ref, o_ref, hn_ref = rest
    else:
        (o_ref,) = rest
    b = pl.program_id(0)
    j = pl.program_id(1)
    st = slott_ref[0]
    per_block = LANES // win
    lane = lax.broadcasted_iota(I32, (SEL_TILE, LANES), 1)
    lane_f = lane.astype(F32)
    group = lane // win

    def round_body(r, acc):
        blocks, pieces = [], []
        for k in range(N_EXPERTS // per_block):
            target = None
            for i in range(per_block):
                e = k * per_block + i
                first, start = _window(cum_ref, b, e, j, r, n_tiles, cap, win)
                col = st[:, e:e + 1]
                tgt = jnp.where(col >= first.astype(F32), col - (start - i * win).astype(F32), -1.0)
                target = tgt if target is None else jnp.where(group == i, tgt, target)
                pieces.append(ys_ref[e, 0, pl.ds(start, win), :])
            blocks.append(jnp.where(lane_f == target, 1.0, 0.0).astype(BF16))
        onehot = jnp.concatenate(blocks, axis=1)
        stacked = jnp.concatenate(pieces, axis=0)
        return acc + _dot(onehot, stacked)

    acc = lax.fori_loop(0, rnd_ref[b * n_tiles + j], round_body, jnp.zeros(o_ref.shape[1:], F32))
    x2 = x_ref[0] + g_ref[0] * acc
    o_ref[0] = x2
    if next_norm:
        hn_ref[0] = _prenorm_math(x2, nw_ref[...], sh_ref[0], sc_ref[0]).astype(BF16)


def _scatter(cum, rounds, slott, ys, x1, gate, cap, next_norm=None):
    b, n, d = x1.shape
    n_tiles = n // SEL_TILE
    win = min(EC_WINDOW, cap)
    tile = pl.BlockSpec((1, SEL_TILE, d), lambda i, t, c, r: (i, t, 0))
    vec = pl.BlockSpec((1, 1, d), lambda i, t, c, r: (i, 0, 0))
    in_specs = [
        pl.BlockSpec((1, SEL_TILE, LANES), lambda i, t, c, r: (i, t, 0)),
        pl.BlockSpec((N_EXPERTS, 1, cap, d), lambda i, t, c, r: (0, i, 0, 0), pipeline_mode=pl.Buffered(1)),
        tile, vec,
    ]
    args = [cum, rounds, slott, ys, x1, gate]
    out_specs, out_shape = tile, jax.ShapeDtypeStruct((b, n, d), F32)
    if next_norm is not None:
        in_specs += [pl.BlockSpec((1, d), lambda i, t, c, r: (0, 0)), vec, vec]
        args += [next_norm[0].reshape(1, d), next_norm[1], next_norm[2]]
        out_specs, out_shape = [tile, tile], [out_shape, jax.ShapeDtypeStruct((b, n, d), BF16)]
    grid_spec = pltpu.PrefetchScalarGridSpec(
        num_scalar_prefetch=2, grid=(b, n_tiles), in_specs=in_specs, out_specs=out_specs)
    return pl.pallas_call(
        functools.partial(_scatter_kernel, cap=cap, win=win, n_tiles=n_tiles, next_norm=next_norm is not None),
        grid_spec=grid_spec,
        out_shape=out_shape,
        compiler_params=_params(("parallel", "arbitrary")),
        name="ec_scatter",
    )(*args)


def _mod_chunks(mod_rows):
    b = mod_rows.shape[0]
    return [mod_rows[:, i * D_MODEL:(i + 1) * D_MODEL].reshape(b, 1, D_MODEL) for i in range(N_MOD)]


def _token_mixer(h, zmix, att, lw, dft):
    b, n, d = h.shape
    y_conv, y_pool = _stencil(zmix, lw["conv_w"], lw["pool_w"], lw["pool_scale"])
    y_f = _fourier(zmix, *dft)
    branches = [y.reshape(b * n, MIX_W) for y in (y_conv, att, y_f, y_pool)]
    merged = _merge(h.reshape(b * n, d), branches, lw["w_in"], lw["w_branch"], lw["layer"])
    return merged.reshape(b, n, d)


def _ffn(x1, h2, aff, affn, g2, lw, next_norm=None):
    b, n, d = x1.shape
    cap = EC_CAPACITY * n // N_EXPERTS
    slot, slott = _select(aff, cap)
    per_tile = jnp.sum((slot >= 0).reshape(b, N_EXPERTS, n // SEL_TILE, SEL_TILE), axis=-1, dtype=I32)
    cum = jnp.concatenate([jnp.zeros((b, N_EXPERTS, 1), I32), jnp.cumsum(per_tile, axis=-1)], axis=-1)
    win = min(EC_WINDOW, cap)
    span = cum[..., 1:] - (cum[..., :-1] // ROW_ALIGN) * ROW_ALIGN
    rounds = jnp.max((span + win - 1) // win, axis=1).reshape(-1)
    cum = cum.reshape(-1)
    xs, gate = _gather(cum, rounds, slot, h2, affn, cap)
    ys = _experts(xs.reshape(N_EXPERTS, b * cap, d), gate.reshape(N_EXPERTS, b * cap, LANES),
                  lw["w_gate"], lw["w_up"], lw["w_down"], lw["layer"])
    return _scatter(cum, rounds, slott, ys.reshape(N_EXPERTS, b, cap, d), x1, g2, cap, next_norm)


def kernel(x, c, ctx, c_ctx, w_ada, b_ada, norm1_w, norm2_w, w_in, conv_w, q_norm_w, k_norm_w,
           na_rpb, pool_w, pool_scale, w_branch, w_out, w_router, w_exp_gate, w_exp_up, w_exp_down):
    bsz, n, d = x.shape
    l_ctx = ctx.shape[1]
    depth = w_ada.shape[0]
    n_rows = n // GRID_W

    c16 = jnp.zeros((16, d), F32).at[:bsz].set(c).at[bsz].set(c_ctx)
    mod_all = _ada(c16, w_ada, b_ada)

    hd = np.arange(MIX_W) // HEAD_DIM
    bd = jnp.asarray(hd[:, None] == hd[None, :], dtype=BF16)
    dft_lat = _dft_mats(n, 1.0, n // 2) + _dft_mats(FOURIER_GC, (n * FOURIER_GC) ** -0.5)
    dft_ctx = _dft_mats(l_ctx, 1.0, l_ctx // 2) + _dft_mats(FOURIER_GC, (l_ctx * FOURIER_GC) ** -0.5)

    mods = [_mod_chunks(mod_all[l, :bsz]) for l in range(depth)]
    h = _prenorm(x, norm1_w[0], mods[0][0], mods[0][1])
    for l in range(depth):
        last = l == depth - 1
        lw = {
            "w_in": w_in, "conv_w": conv_w[l], "pool_w": pool_w[l],
            "pool_scale": pool_scale[l], "w_branch": w_branch,
            "w_gate": w_exp_gate, "w_up": w_exp_up, "w_down": w_exp_down, "layer": l,
        }
        w_out_bf = w_out[l].astype(BF16)
        wr_pad = jnp.zeros((d, LANES), F32).at[:, :N_EXPERTS].set(w_router[l])
        wr_hi = wr_pad.astype(BF16)
        wr_cat = jnp.concatenate([wr_hi, (wr_pad - wr_hi.astype(F32)).astype(BF16)], axis=1)
        q_w = jnp.tile(q_norm_w[l], N_HEADS).reshape(1, MIX_W)
        k_w = jnp.tile(k_norm_w[l], N_HEADS).reshape(1, MIX_W)
        tb = _na_bias_table(na_rpb[l], n_rows)
        _, _, g1, sh2, sc2, g2 = mods[l]
        mc = _mod_chunks(jnp.broadcast_to(mod_all[l, bsz:bsz + 1], (bsz, N_MOD * d)))

        hc = _prenorm(ctx, norm1_w[l], mc[0], mc[1])
        zc_mix = _in_proj(hc.reshape(bsz * l_ctx, d), w_in, l, MIX_COLS).reshape(bsz, l_ctx, MIX_COLS)

        zmix = _in_proj(h.reshape(bsz * n, d), w_in, l, MIX_COLS).reshape(bsz, n, MIX_COLS)
        att = _neighbourhood_attention(zmix, zc_mix, q_w, k_w, bd, tb)
        merged = _token_mixer(h, zmix, att, lw, dft_lat)
        x1, h2, aff, affn = _out_proj(merged, w_out_bf, x, g1, norm2_w[l], sh2, sc2, wr_cat)
        if last:
            x = _ffn(x1, h2, aff, affn, g2, lw)
        else:
            x, h = _ffn(x1, h2, aff, affn, g2, lw, (norm1_w[l + 1], mods[l + 1][0], mods[l + 1][1]))

        if not last:
            att_c = _context_attention(zc_mix, q_w, k_w, bd)
            merged_c = _token_mixer(hc, zc_mix, att_c, lw, dft_ctx)
            c1, hc2, aff_c, affn_c = _out_proj(merged_c, w_out_bf, ctx, mc[2], norm2_w[l], mc[3], mc[4], wr_cat)
            ctx = _ffn(c1, hc2, aff_c, affn_c, mc[5], lw)
    return x
```

```python
import functools

import numpy as np
import jax
import jax.numpy as jnp
from jax import lax
from jax.experimental import pallas as pl
from jax.experimental.pallas import tpu as pltpu

F32 = jnp.float32
BF16 = jnp.bfloat16
I32 = jnp.int32
HIGHEST = lax.Precision.HIGHEST

D_MODEL = 2048
GRID_W = 64
MIX_W = D_MODEL // 4
N_HEADS = 8
HEAD_DIM = MIX_W // N_HEADS
WIN_H = 8
WIN_W = 16
ATT_SCALE = HEAD_DIM ** -0.5
POOL_WINDOWS = (2, 4, 8, 16)
POOL_GC = MIX_W // len(POOL_WINDOWS)
FOURIER_GC = MIX_W // 4
N_EXPERTS = 16
EC_CAPACITY = 2
N_MOD = 6
EPS = 1e-6
NEG_INF = -1e30
MIX_COLS = 8 * MIX_W
LANES = 128
HALO = 16
SEL_TILE = 256
ROW_ALIGN = 16
EC_WINDOW = 64
ROWS_PER_PASS = 2
VMEM_LIMIT = 56 * 1024 * 1024

COL_XA, COL_GB, COL_GC, COL_Q, COL_K, COL_V, COL_F, COL_P = range(8)


def _params(sem, vmem=VMEM_LIMIT):
    return pltpu.CompilerParams(dimension_semantics=sem, vmem_limit_bytes=vmem)


def _dot(a, b):
    return jnp.dot(a, b, preferred_element_type=F32)


def _dot_nt(a, b):
    return lax.dot_general(a, b, (((1,), (1,)), ((), ())), preferred_element_type=F32)


def _ada_kernel(c_ref, w_ref, b_ref, o_ref):
    c = c_ref[...]
    s = c * jax.nn.sigmoid(c)
    o_ref[0] = jnp.dot(s, w_ref[0], precision=HIGHEST, preferred_element_type=F32) + b_ref[0]


def _ada(c16, w_ada, b_ada):
    depth, d, ncol = w_ada.shape
    tn = 1024
    return pl.pallas_call(
        _ada_kernel,
        grid=(depth, ncol // tn),
        in_specs=[
            pl.BlockSpec((16, d), lambda l, j: (0, 0)),
            pl.BlockSpec((1, d, tn), lambda l, j: (l, 0, j)),
            pl.BlockSpec((1, 1, tn), lambda l, j: (l, 0, j)),
        ],
        out_specs=pl.BlockSpec((1, 16, tn), lambda l, j: (l, 0, j)),
        out_shape=jax.ShapeDtypeStruct((depth, 16, ncol), F32),
        compiler_params=_params(("parallel", "parallel")),
        name="ada",
    )(c16, w_ada, b_ada.reshape(depth, 1, ncol))


def _prenorm_math(x, w, shift, scale):
    ms = jnp.mean(x * x, axis=-1, keepdims=True)
    y = x * lax.rsqrt(ms + EPS) * w
    return y * (1.0 + scale) + shift


def _prenorm_kernel(x_ref, w_ref, sh_ref, sc_ref, o_ref):
    o_ref[0] = _prenorm_math(x_ref[0], w_ref[...], sh_ref[0], sc_ref[0]).astype(BF16)


def _prenorm(x, w, shift, scale):
    b, n, d = x.shape
    tm = min(512, n)
    return pl.pallas_call(
        _prenorm_kernel,
        grid=(b, n // tm),
        in_specs=[
            pl.BlockSpec((1, tm, d), lambda i, j: (i, j, 0)),
            pl.BlockSpec((1, d), lambda i, j: (0, 0)),
            pl.BlockSpec((1, 1, d), lambda i, j: (i, 0, 0)),
            pl.BlockSpec((1, 1, d), lambda i, j: (i, 0, 0)),
        ],
        out_specs=pl.BlockSpec((1, tm, d), lambda i, j: (i, j, 0)),
        out_shape=jax.ShapeDtypeStruct((b, n, d), BF16),
        compiler_params=_params(("parallel", "parallel")),
        name="prenorm",
    )(x, w.reshape(1, d), shift, scale)


def _mm_kernel(a_ref, w_ref, o_ref, w_bf):
    @pl.when(pl.program_id(1) == 0)
    def _():
        w_bf[...] = w_ref[0].astype(BF16)

    o_ref[...] = _dot(a_ref[...], w_bf[...]).astype(o_ref.dtype)


def _in_proj(h2d, w, layer, ncols):
    m, k = h2d.shape
    tm = min(1024, m)
    tn = 1024
    return pl.pallas_call(
        _mm_kernel,
        grid=(ncols // tn, m // tm),
        in_specs=[
            pl.BlockSpec((tm, k), lambda j, i: (i, 0)),
            pl.BlockSpec((1, k, tn), lambda j, i: (layer, 0, j)),
        ],
        out_specs=pl.BlockSpec((tm, tn), lambda j, i: (i, j)),
        out_shape=jax.ShapeDtypeStruct((m, ncols), BF16),
        scratch_shapes=[pltpu.VMEM((k, tn), BF16)],
        compiler_params=_params(("parallel", "arbitrary")),
        name="in_proj",
    )(h2d, w)


def _stencil_kernel(xa_ref, gb_ref, gc_ref, zp_ref, xa_p, xa_n, gc_p, gc_n, zp_p, zp_n,
                    cw_ref, pw_ref, ps_ref, oc_ref, op_ref, *, tm, n_seq):
    i = pl.program_id(1)
    nt = pl.num_programs(1)
    rows = tm + 2 * HALO

    def ext(main_ref, prev_ref, next_ref):
        main = main_ref[0].astype(F32)
        prev = jnp.where(i > 0, prev_ref[0].astype(F32), 0.0)
        nxt = jnp.where(i < nt - 1, next_ref[0].astype(F32), 0.0)
        return jnp.concatenate([prev, main, nxt], axis=0)

    def shifted(a, d):
        if d == 0:
            return a[HALO:HALO + tm]
        return pltpu.roll(a, (-d) % rows, axis=0)[HALO:HALO + tm]

    u = ext(gc_ref, gc_p, gc_n) * ext(xa_ref, xa_p, xa_n)
    cw = cw_ref[...]
    y = cw[0:1] * shifted(u, -1) + cw[1:2] * shifted(u, 0) + cw[2:3] * shifted(u, 1)
    oc_ref[0] = (gb_ref[0].astype(F32) * y).astype(BF16)

    t = (i * tm + lax.broadcasted_iota(I32, (tm, 1), 0)).astype(F32)
    zp = ext(zp_ref, zp_p, zp_n)
    for g, win in enumerate(POOL_WINDOWS):
        ug = zp[:, g * POOL_GC:(g + 1) * POOL_GC]
        acc = ug + pltpu.roll(ug, 1, axis=0)
        half = 1
        while 2 * half < win:
            acc = pltpu.roll(acc, rows - half, axis=0) + pltpu.roll(acc, half, axis=0)
            half *= 2
        wsum = acc[HALO:HALO + tm]
        lo = jnp.maximum(t - (win // 2), 0.0)
        hi = jnp.minimum(t + (win - win // 2 - 1), float(n_seq - 1))
        pooled = wsum / (hi - lo + 1.0) - ug[HALO:HALO + tm]
        yg = _dot(pooled.astype(BF16), pw_ref[g])
        op_ref[0, :, g * POOL_GC:(g + 1) * POOL_GC] = (
            yg * ps_ref[:, g * POOL_GC:(g + 1) * POOL_GC]).astype(BF16)


def _stencil(zmix, conv_w, pool_w, pool_scale):
    b, n, _ = zmix.shape
    tm = min(512, n)
    hb = tm // HALO
    last = n // HALO - 1

    def main(col):
        return pl.BlockSpec((1, tm, MIX_W), lambda bi, i: (bi, i, col))

    def prev(col):
        return pl.BlockSpec((1, HALO, MIX_W), lambda bi, i: (bi, jnp.maximum(i * hb - 1, 0), col))

    def nxt(col):
        return pl.BlockSpec((1, HALO, MIX_W), lambda bi, i: (bi, jnp.minimum((i + 1) * hb, last), col))

    out_spec = pl.BlockSpec((1, tm, MIX_W), lambda bi, i: (bi, i, 0))
    return pl.pallas_call(
        functools.partial(_stencil_kernel, tm=tm, n_seq=n),
        grid=(b, n // tm),
        in_specs=[main(COL_XA), main(COL_GB), main(COL_GC), main(COL_P),
                  prev(COL_XA), nxt(COL_XA), prev(COL_GC), nxt(COL_GC), prev(COL_P), nxt(COL_P),
                  pl.BlockSpec((3, MIX_W), lambda bi, i: (0, 0)),
                  pl.BlockSpec((len(POOL_WINDOWS), POOL_GC, POOL_GC), lambda bi, i: (0, 0, 0)),
                  pl.BlockSpec((1, MIX_W), lambda bi, i: (0, 0))],
        out_specs=[out_spec, out_spec],
        out_shape=[jax.ShapeDtypeStruct((b, n, MIX_W), BF16)] * 2,
        compiler_params=_params(("parallel", "parallel")),
        name="stencil",
    )(zmix, zmix, zmix, zmix, zmix, zmix, zmix, zmix, zmix, zmix,
      conv_w, pool_w.astype(BF16), pool_scale.reshape(1, MIX_W))


def _fourier_kernel(cn_ref, sn_ref, u_ref, cc_ref, sc_ref, o_ref, even, odd):
    n = u_ref.shape[1]
    half, tk = n // 2, cn_ref.shape[0]
    rb = min(256, half)

    @pl.when(pl.program_id(1) == 0)
    def _():
        i = lax.broadcasted_iota(I32, (rb, rb), 0)
        j = lax.broadcasted_iota(I32, (rb, rb), 1)
        flip = jnp.where(i + j == rb, 1.0, 0.0).astype(BF16)
        row = lax.broadcasted_iota(I32, (rb, 1), 0)
        for a in range(half // rb):
            src = n - (a + 1) * rb
            rev = _dot(flip, u_ref[0, src:src + rb, :])
            first = (n - a * rb) % n
            rev = jnp.where(row == 0, u_ref[0, first:first + 1, :].astype(F32), rev)
            ua = u_ref[0, a * rb:(a + 1) * rb, :].astype(F32)
            fold = ua + rev if a else jnp.where(row == 0, ua, ua + rev)
            even[a * rb:(a + 1) * rb, :] = fold.astype(BF16)
            odd[a * rb:(a + 1) * rb, :] = (ua - rev).astype(BF16)

    k = pl.program_id(1) * tk + lax.broadcasted_iota(I32, (tk, 1), 0)
    sign = (1 - 2 * jnp.bitwise_and(k, 1)).astype(F32)
    p = (_dot(cn_ref[...], even[...]) + sign * u_ref[0, half:half + 1, :].astype(F32)).astype(BF16)
    q = _dot(sn_ref[...], odd[...]).astype(BF16)
    for g in range(MIX_W // FOURIER_GC):
        sl = slice(g * FOURIER_GC, (g + 1) * FOURIER_GC)
        o_ref[0, :, sl] = (_dot(p[:, sl], cc_ref[...]) - _dot(q[:, sl], sc_ref[...])).astype(BF16)


def _dft_mats(n, scale, ncols=None):
    ncols = n if ncols is None else ncols
    k = jnp.arange(n, dtype=I32)[:, None]

    def table(count, period):
        r = (k * jnp.arange(count, dtype=I32)[None, :]) % period
        ang = r.astype(F32) * np.float32(2.0 * np.pi / period)
        return jnp.cos(ang), jnp.sin(ang)

    inner = 64
    if n <= inner or n % inner or ncols % inner:
        c, s = table(ncols, n)
    else:
        ca, sa = table(ncols // inner, n // inner)
        cb, sb = table(inner, n)
        c = (ca[:, :, None] * cb[:, None, :] - sa[:, :, None] * sb[:, None, :]).reshape(n, ncols)
        s = (sa[:, :, None] * cb[:, None, :] + ca[:, :, None] * sb[:, None, :]).reshape(n, ncols)
    return (c * scale).astype(BF16), (s * scale).astype(BF16)


def _fourier(zmix, cn, sn, cc, sc):
    b, n, _ = zmix.shape
    half = n // 2
    tk = min(512, n)
    return pl.pallas_call(
        _fourier_kernel,
        grid=(b, n // tk),
        in_specs=[
            pl.BlockSpec((tk, half), lambda bi, k: (k, 0)),
            pl.BlockSpec((tk, half), lambda bi, k: (k, 0)),
            pl.BlockSpec((1, n, MIX_W), lambda bi, k: (bi, 0, COL_F)),
            pl.BlockSpec((FOURIER_GC, FOURIER_GC), lambda bi, k: (0, 0)),
            pl.BlockSpec((FOURIER_GC, FOURIER_GC), lambda bi, k: (0, 0)),
        ],
        out_specs=pl.BlockSpec((1, tk, MIX_W), lambda bi, k: (bi, k, 0)),
        out_shape=jax.ShapeDtypeStruct((b, n, MIX_W), BF16),
        scratch_shapes=[pltpu.VMEM((half, MIX_W), BF16), pltpu.VMEM((half, MIX_W), BF16)],
        compiler_params=_params(("parallel", "arbitrary")),
        name="fourier",
    )(cn, sn, zmix, cc, sc)


def _head_rms(x, w, bd):
    x2 = x * x
    hi = x2.astype(BF16)
    lo = (x2 - hi.astype(F32)).astype(BF16)
    ms = (_dot(hi, bd) + _dot(lo, bd)) * (1.0 / HEAD_DIM)
    return x * lax.rsqrt(ms + EPS) * w


def _pair_stack(q2):
    lane = lax.broadcasted_iota(I32, q2.shape, 1)
    zero = jnp.zeros_like(q2)
    return jnp.concatenate([jnp.where(lane < HEAD_DIM, q2, zero),
                            jnp.where(lane >= HEAD_DIM, q2, zero)], axis=0)


def _pair_unstack(o2, m):
    lane = lax.broadcasted_iota(I32, (m, LANES), 1)
    return jnp.where(lane < HEAD_DIM, o2[:m], o2[m:])


def _na_kernel(q_ref, k_ref, v_ref, kc_ref, vc_ref, qw_ref, kw_ref, bd_ref, tb_ref, o_ref,
               kn_scr, kcn_scr, qn_scr, *, rt, n_rows):
    rb = pl.program_id(1)
    n_seq = n_rows * GRID_W
    chunk = 512

    @pl.when(rb == 0)
    def _():
        def body(c, carry):
            sl = pl.ds(pl.multiple_of(c * chunk, chunk), chunk)
            kn_scr[sl, :] = _head_rms(k_ref[0, sl, :].astype(F32), kw_ref[...], bd_ref[...]).astype(BF16)
            return carry
        lax.fori_loop(0, n_seq // chunk, body, 0)
        kcn_scr[...] = _head_rms(kc_ref[0].astype(F32), kw_ref[...], bd_ref[...]).astype(BF16)

    qn_scr[...] = (_head_rms(q_ref[0].astype(F32), qw_ref[...], bd_ref[...]) * ATT_SCALE).astype(BF16)

    kh = min(WIN_H, n_rows)
    nloc = kh * GRID_W

    def row_body(jj, carry):
        units = []
        for i in range(ROWS_PER_PASS):
            j = jj * ROWS_PER_PASS + i
            r = rb * rt + j
            rs = jnp.clip(r - kh // 2, 0, n_rows - kh)
            q_rows = pl.ds(pl.multiple_of(j * GRID_W, GRID_W), GRID_W)
            k_rows = pl.ds(pl.multiple_of(rs * GRID_W, GRID_W), nloc)
            for p in range(N_HEADS // 2):
                units.append((q_rows, k_rows, rs - r + (WIN_H - 1), p, slice(p * LANES, (p + 1) * LANES)))
        qs = [_pair_stack(qn_scr[q_rows, cols]) for q_rows, _, _, _, cols in units]
        s_loc = [_dot_nt(q, kn_scr[k_rows, cols]) for q, (_, k_rows, _, _, cols) in zip(qs, units)]
        s_ctx = [_dot_nt(q, kcn_scr[:, cols]) for q, (_, _, _, _, cols) in zip(qs, units)]
        s_loc = [s + tb_ref[dr0, p] for s, (_, _, dr0, p, _) in zip(s_loc, units)]
        m = [jnp.maximum(jnp.max(a, axis=1, keepdims=True), jnp.max(c, axis=1, keepdims=True))
             for a, c in zip(s_loc, s_ctx)]
        p_loc = [jnp.exp(a - mx) for a, mx in zip(s_loc, m)]
        p_ctx = [jnp.exp(c - mx) for c, mx in zip(s_ctx, m)]
        denom = [jnp.sum(a, axis=1, keepdims=True) + jnp.sum(c, axis=1, keepdims=True) for a, c in zip(p_loc, p_ctx)]
        o2 = [_dot(a.astype(BF16), v_ref[0, k_rows, cols]) + _dot(c.astype(BF16), vc_ref[0, :, cols])
              for a, c, (_, k_rows, _, _, cols) in zip(p_loc, p_ctx, units)]
        for o, d, (q_rows, _, _, _, cols) in zip(o2, denom, units):
            o_ref[0, q_rows, cols] = _pair_unstack(o / d, GRID_W).astype(BF16)
        return carry

    lax.fori_loop(0, rt // ROWS_PER_PASS, row_body, 0)


def _na_bias_table(rpb, n_rows):
    kh = min(WIN_H, n_rows)
    c = np.arange(GRID_W)
    cs = np.clip(c - WIN_W // 2, 0, GRID_W - WIN_W)
    kc = np.arange(GRID_W)
    ok = (kc[None, :] >= cs[:, None]) & (kc[None, :] < cs[:, None] + WIN_W)
    edge = GRID_W - WIN_W
    padded = jnp.pad(rpb.astype(F32), ((0, 0), (0, 0), (edge, edge)))
    toeplitz = jnp.stack([padded[:, :, GRID_W - 1 - q:2 * GRID_W - 1 - q] for q in range(GRID_W)], axis=2)
    masked = jnp.where(ok[None, None], toeplitz, NEG_INF)
    t = jnp.stack([masked[:, d0:d0 + kh] for d0 in range(WIN_H)], axis=0)
    t = jnp.transpose(t, (0, 1, 3, 2, 4))
    return t.reshape(WIN_H, N_HEADS // 2, 2 * GRID_W, kh * GRID_W)


def _neighbourhood_attention(zmix, zc_mix, q_w, k_w, bd, tb):
    b, n, _ = zmix.shape
    l = zc_mix.shape[1]
    n_rows = n // GRID_W
    rt = 8
    nloc = min(WIN_H, n_rows) * GRID_W
    return pl.pallas_call(
        functools.partial(_na_kernel, rt=rt, n_rows=n_rows),
        grid=(b, n_rows // rt),
        in_specs=[
            pl.BlockSpec((1, rt * GRID_W, MIX_W), lambda bi, r: (bi, r, COL_Q)),
            pl.BlockSpec((1, n, MIX_W), lambda bi, r: (bi, 0, COL_K)),
            pl.BlockSpec((1, n, MIX_W), lambda bi, r: (bi, 0, COL_V)),
            pl.BlockSpec((1, l, MIX_W), lambda bi, r: (bi, 0, COL_K)),
            pl.BlockSpec((1, l, MIX_W), lambda bi, r: (bi, 0, COL_V)),
            pl.BlockSpec((1, MIX_W), lambda bi, r: (0, 0)),
            pl.BlockSpec((1, MIX_W), lambda bi, r: (0, 0)),
            pl.BlockSpec((MIX_W, MIX_W), lambda bi, r: (0, 0)),
            pl.BlockSpec((WIN_H, N_HEADS // 2, 2 * GRID_W, nloc), lambda bi, r: (0, 0, 0, 0),
                         pipeline_mode=pl.Buffered(1)),
        ],
        out_specs=pl.BlockSpec((1, rt * GRID_W, MIX_W), lambda bi, r: (bi, r, 0)),
        out_shape=jax.ShapeDtypeStruct((b, n, MIX_W), BF16),
        scratch_shapes=[pltpu.VMEM((n, MIX_W), BF16), pltpu.VMEM((l, MIX_W), BF16),
                        pltpu.VMEM((rt * GRID_W, MIX_W), BF16)],
        compiler_params=_params(("parallel", "arbitrary")),
        name="nbr_attention",
    )(zmix, zmix, zmix, zc_mix, zc_mix, q_w, k_w, bd, tb)


def _ctx_attn_kernel(q_ref, k_ref, v_ref, qw_ref, kw_ref, bd_ref, o_ref):
    l = q_ref.shape[1]
    qn = (_head_rms(q_ref[0].astype(F32), qw_ref[...], bd_ref[...]) * ATT_SCALE).astype(BF16)
    kn = _head_rms(k_ref[0].astype(F32), kw_ref[...], bd_ref[...]).astype(BF16)
    for p in range(N_HEADS // 2):
        cols = slice(p * LANES, (p + 1) * LANES)
        qs = _pair_stack(qn[:, cols])
        s = _dot_nt(qs, kn[:, cols])
        m = jnp.max(s, axis=1, keepdims=True)
        e = jnp.exp(s - m)
        o2 = _dot(e.astype(BF16), v_ref[0, :, cols]) / jnp.sum(e, axis=1, keepdims=True)
        o_ref[0, :, cols] = _pair_unstack(o2, l).astype(BF16)


def _context_attention(zc_mix, q_w, k_w, bd):
    b, l, _ = zc_mix.shape
    vec = pl.BlockSpec((1, MIX_W), lambda bi: (0, 0))
    return pl.pallas_call(
        _ctx_attn_kernel,
        grid=(b,),
        in_specs=[
            pl.BlockSpec((1, l, MIX_W), lambda bi: (bi, 0, COL_Q)),
            pl.BlockSpec((1, l, MIX_W), lambda bi: (bi, 0, COL_K)),
            pl.BlockSpec((1, l, MIX_W), lambda bi: (bi, 0, COL_V)),
            vec, vec,
            pl.BlockSpec((MIX_W, MIX_W), lambda bi: (0, 0)),
        ],
        out_specs=pl.BlockSpec((1, l, MIX_W), lambda bi: (bi, 0, 0)),
        out_shape=jax.ShapeDtypeStruct((b, l, MIX_W), BF16),
        compiler_params=_params(("parallel",)),
        name="ctx_attention",
    )(zc_mix, zc_mix, zc_mix, q_w, k_w, bd)


def _merge_kernel(h_ref, y0, y1, y2, y3, g0, g1, g2, g3, wb_ref, o_ref, wg_bf, wb_bf):
    @pl.when(pl.program_id(1) == 0)
    def _():
        for i, g in enumerate((g0, g1, g2, g3)):
            wg_bf[i] = g[0].astype(BF16)
        wb_bf[...] = wb_ref[0].astype(BF16)

    h = h_ref[...]
    acc = None
    for i, y in enumerate((y0, y1, y2, y3)):
        gate = jax.nn.sigmoid(_dot(h, wg_bf[i]))
        term = gate * _dot(y[...], wb_bf[i])
        acc = term if acc is None else acc + term
    o_ref[...] = acc.astype(BF16)


def _merge(h2d, branches, w_in, w_branch, layer):
    m, d = h2d.shape
    tm = min(1024, m)
    tn = 256
    gate0 = MIX_COLS // tn

    def gspec(i):
        return pl.BlockSpec((1, d, tn), lambda n, r: (layer, 0, gate0 + i * (d // tn) + n))

    yspec = pl.BlockSpec((tm, MIX_W), lambda n, r: (r, 0))
    return pl.pallas_call(
        _merge_kernel,
        grid=(d // tn, m // tm),
        in_specs=[pl.BlockSpec((tm, d), lambda n, r: (r, 0)), yspec, yspec, yspec, yspec,
                  gspec(0), gspec(1), gspec(2), gspec(3),
                  pl.BlockSpec((1, 4, MIX_W, tn), lambda n, r: (layer, 0, 0, n))],
        out_specs=pl.BlockSpec((tm, tn), lambda n, r: (r, n)),
        out_shape=jax.ShapeDtypeStruct((m, d), BF16),
        scratch_shapes=[pltpu.VMEM((4, d, tn), BF16), pltpu.VMEM((4, MIX_W, tn), BF16)],
        compiler_params=_params(("parallel", "arbitrary")),
        name="merge",
    )(h2d, *branches, w_in, w_in, w_in, w_in, w_branch)


def _out_proj_kernel(m_ref, w_ref, x_ref, g_ref, nw_ref, sh_ref, sc_ref, wrc_ref,
                     x1_ref, h2_ref, aff_ref, affn_ref):
    x1 = x_ref[0] + g_ref[0] * _dot(m_ref[0], w_ref[...])
    x1_ref[0] = x1
    h2 = _prenorm_math(x1, nw_ref[...], sh_ref[0], sc_ref[0])
    h_hi = h2.astype(BF16)
    h2_ref[0] = h_hi
    h_lo = (h2 - h_hi.astype(F32)).astype(BF16)
    both = _dot(h_hi, wrc_ref[...])
    logits = both[:, :LANES] + (both[:, LANES:] + _dot(h_lo, wrc_ref[:, :LANES]))
    lane = lax.broadcasted_iota(I32, logits.shape, 1)
    logits = jnp.where(lane < N_EXPERTS, logits, NEG_INF)
    e = jnp.exp(logits - jnp.max(logits, axis=1, keepdims=True))
    aff = e / jnp.sum(e, axis=1, keepdims=True)
    aff_ref[0] = aff.T[:N_EXPERTS]
    hi = aff.astype(BF16).astype(F32)
    r1 = aff - hi
    mid = r1.astype(BF16).astype(F32)
    lo = (r1 - mid).astype(BF16).astype(F32)
    packed = hi + pltpu.roll(mid, N_EXPERTS, axis=1) + pltpu.roll(lo, 2 * N_EXPERTS, axis=1)
    affn_ref[0] = packed.astype(BF16)


def _out_proj(merged, w_out_bf, x, gate, norm_w, shift, scale, wr_cat):
    b, n, d = x.shape
    tm = min(512, n)
    vec = pl.BlockSpec((1, 1, d), lambda i, j: (i, 0, 0))
    tile = pl.BlockSpec((1, tm, d), lambda i, j: (i, j, 0))
    wr_spec = pl.BlockSpec((d, 2 * LANES), lambda i, j: (0, 0))
    return pl.pallas_call(
        _out_proj_kernel,
        grid=(b, n // tm),
        in_specs=[tile, pl.BlockSpec((d, d), lambda i, j: (0, 0), pipeline_mode=pl.Buffered(1)), tile, vec,
                  pl.BlockSpec((1, d), lambda i, j: (0, 0)), vec, vec, wr_spec],
        out_specs=[tile, tile,
                   pl.BlockSpec((1, N_EXPERTS, tm), lambda i, j: (i, 0, j)),
                   pl.BlockSpec((1, tm, LANES), lambda i, j: (i, j, 0))],
        out_shape=[jax.ShapeDtypeStruct((b, n, d), F32), jax.ShapeDtypeStruct((b, n, d), BF16),
                   jax.ShapeDtypeStruct((b, N_EXPERTS, n), F32), jax.ShapeDtypeStruct((b, n, LANES), BF16)],
        compiler_params=_params(("parallel", "parallel")),
        name="out_proj",
    )(merged, w_out_bf, x, gate, norm_w.reshape(1, d), shift, scale, wr_cat)


def _lane_cumsum(mask_f, ut):
    rows, n = mask_f.shape
    carry = jnp.zeros((rows, 1), F32)
    parts = []
    for j in range(n // LANES):
        c = _dot(mask_f[:, j * LANES:(j + 1) * LANES].astype(BF16), ut) + carry
        parts.append(c)
        carry = c[:, LANES - 1:LANES]
    return jnp.concatenate(parts, axis=1)


def _select_kernel(aff_ref, slot_ref, slott_ref, *, cap):
    aff = aff_ref[0]
    e, n = aff.shape
    bits = lax.bitcast_convert_type(aff, I32)

    def body(_, carry):
        lo, hi = carry
        mid = lo + ((hi - lo) >> 1)
        cnt = jnp.sum(jnp.where(bits >= mid, 1.0, 0.0), axis=1, keepdims=True)
        ge = cnt >= float(cap)
        return jnp.where(ge, mid, lo), jnp.where(ge, hi, mid)

    lo, _ = lax.fori_loop(0, 31, body, (jnp.zeros((e, 1), I32), jnp.full((e, 1), 0x7F800000, I32)))
    ri = lax.broadcasted_iota(I32, (LANES, LANES), 0)
    ci = lax.broadcasted_iota(I32, (LANES, LANES), 1)
    ut = jnp.where(ri <= ci, 1.0, 0.0).astype(BF16)
    gt = bits > lo
    eq = bits == lo
    need = float(cap) - jnp.sum(jnp.where(gt, 1.0, 0.0), axis=1, keepdims=True)
    ceq = _lane_cumsum(jnp.where(eq, 1.0, 0.0), ut)
    sel = gt | (eq & (ceq <= need))
    csel = _lane_cumsum(jnp.where(sel, 1.0, 0.0), ut)
    slot = jnp.where(sel, csel - 1.0, -1.0)
    slot_ref[0] = slot.astype(I32)
    padded = jnp.concatenate([slot, jnp.full((LANES - e, n), -1.0, F32)], axis=0)
    slott_ref[0] = padded.T


def _select(aff, cap):
    b, e, n = aff.shape
    return pl.pallas_call(
        functools.partial(_select_kernel, cap=cap),
        grid=(b,),
        in_specs=[pl.BlockSpec((1, e, n), lambda i: (i, 0, 0))],
        out_specs=[pl.BlockSpec((1, e, n), lambda i: (i, 0, 0)),
                   pl.BlockSpec((1, n, LANES), lambda i: (i, 0, 0))],
        out_shape=[jax.ShapeDtypeStruct((b, e, n), I32), jax.ShapeDtypeStruct((b, n, LANES), F32)],
        compiler_params=_params(("parallel",)),
        name="select",
    )(aff)


def _window(cum_ref, b, e, j, r, n_tiles, cap, win):
    c_lo = cum_ref[(b * N_EXPERTS + e) * (n_tiles + 1) + j]
    first = (c_lo // ROW_ALIGN) * ROW_ALIGN + r * win
    start = pl.multiple_of(jnp.minimum(first, cap - win), ROW_ALIGN)
    return first, start


def _gather_kernel(cum_ref, rnd_ref, slot_ref, h_ref, affn_ref, xs_ref, gate_ref, gacc, *, cap, win, n_tiles):
    b = pl.program_id(0)
    dq = pl.program_id(1)
    xs_ref[...] = jnp.zeros_like(xs_ref)

    @pl.when(dq == 0)
    def _():
        gacc[...] = jnp.zeros_like(gacc)

    rows = lax.broadcasted_iota(I32, (win, SEL_TILE), 0)

    def tile_body(j, carry):
        toks = pl.ds(pl.multiple_of(j * SEL_TILE, SEL_TILE), SEL_TILE)
        slots = slot_ref[0, j]
        h_tile = h_ref[0, toks, :]

        def round_body(r, carry2):
            starts, pieces = [], []
            for e in range(N_EXPERTS):
                first, start = _window(cum_ref, b, e, j, r, n_tiles, cap, win)
                s_row = slots[e:e + 1, :]
                hit = (rows + start == s_row) & (s_row >= first)
                pieces.append(jnp.where(hit, 1.0, 0.0).astype(BF16))
                starts.append(start)
            onehot = jnp.concatenate(pieces, axis=0)
            res = _dot(onehot, h_tile)
            for e in range(N_EXPERTS):
                sl = pl.ds(starts[e], win)
                xs_ref[e, 0, sl, :] = (xs_ref[e, 0, sl, :].astype(F32) + res[e * win:(e + 1) * win]).astype(BF16)

            @pl.when(dq == 0)
            def _():
                g = _dot(onehot, affn_ref[0, toks, :])
                for e in range(N_EXPERTS):
                    gacc[e, pl.ds(starts[e], win), :] += g[e * win:(e + 1) * win]
            return carry2

        lax.fori_loop(0, rnd_ref[b * n_tiles + j], round_body, 0)
        return carry

    lax.fori_loop(0, n_tiles, tile_body, 0)

    @pl.when(dq == 0)
    def _():
        lane = lax.broadcasted_iota(I32, (cap, LANES), 1)
        for e in range(N_EXPERTS):
            mine = (lane == e) | (lane == N_EXPERTS + e) | (lane == 2 * N_EXPERTS + e)
            col = jnp.sum(jnp.where(mine, gacc[e], 0.0), axis=1, keepdims=True)
            gate_ref[e, 0] = jnp.broadcast_to(col, (cap, LANES))


def _gather(cum, rounds, slot, h2, affn, cap):
    b, n, d = h2.shape
    n_tiles = n // SEL_TILE
    win = min(EC_WINDOW, cap)
    dn = 512
    slot_tiles = slot.reshape(b, N_EXPERTS, n_tiles, SEL_TILE).transpose(0, 2, 1, 3)
    grid_spec = pltpu.PrefetchScalarGridSpec(
        num_scalar_prefetch=2,
        grid=(b, d // dn),
        in_specs=[
            pl.BlockSpec((1, n_tiles, N_EXPERTS, SEL_TILE), lambda i, q, c, r: (i, 0, 0, 0)),
            pl.BlockSpec((1, n, dn), lambda i, q, c, r: (i, 0, q)),
            pl.BlockSpec((1, n, LANES), lambda i, q, c, r: (i, 0, 0)),
        ],
        out_specs=[pl.BlockSpec((N_EXPERTS, 1, cap, dn), lambda i, q, c, r: (0, i, 0, q)),
                   pl.BlockSpec((N_EXPERTS, 1, cap, LANES), lambda i, q, c, r: (0, i, 0, 0))],
        scratch_shapes=[pltpu.VMEM((N_EXPERTS, cap, LANES), F32)],
    )
    return pl.pallas_call(
        functools.partial(_gather_kernel, cap=cap, win=win, n_tiles=n_tiles),
        grid_spec=grid_spec,
        out_shape=[jax.ShapeDtypeStruct((N_EXPERTS, b, cap, d), BF16),
                   jax.ShapeDtypeStruct((N_EXPERTS, b, cap, LANES), F32)],
        compiler_params=_params(("parallel", "arbitrary")),
        name="ec_gather",
    )(cum, rounds, slot_tiles, h2, affn)


def _expert_up_kernel(xs_ref, wg_ref, wu_ref, h_ref, wg_bf, wu_bf):
    @pl.when(pl.program_id(2) == 0)
    def _():
        wg_bf[...] = wg_ref[0, 0].astype(BF16)
        wu_bf[...] = wu_ref[0, 0].astype(BF16)

    x = xs_ref[0]
    a = _dot(x, wg_bf[...])
    u = _dot(x, wu_bf[...])
    h_ref[0] = (a * jax.nn.sigmoid(a) * u).astype(BF16)


def _expert_down_kernel(h_ref, wd_ref, gate_ref, ys_ref, wd_bf):
    @pl.when(pl.program_id(2) == 0)
    def _():
        wd_bf[...] = wd_ref[0, 0].astype(BF16)

    y = _dot(h_ref[0], wd_bf[...])
    g = gate_ref[0]
    ys_ref[0] = (y * jnp.concatenate([g] * (y.shape[1] // LANES), axis=1)).astype(BF16)


def _experts(xs, gate, wg, wu, wd, layer):
    e, m, d = xs.shape
    ff = wg.shape[3]
    tm = min(2048, m)
    tf = 512
    tn = 1024
    sem = ("parallel", "parallel", "arbitrary")
    hidden = pl.pallas_call(
        _expert_up_kernel,
        grid=(e, ff // tf, m // tm),
        in_specs=[
            pl.BlockSpec((1, tm, d), lambda ei, f, i: (ei, i, 0)),
            pl.BlockSpec((1, 1, d, tf), lambda ei, f, i: (layer, ei, 0, f)),
            pl.BlockSpec((1, 1, d, tf), lambda ei, f, i: (layer, ei, 0, f)),
        ],
        out_specs=pl.BlockSpec((1, tm, tf), lambda ei, f, i: (ei, i, f)),
        out_shape=jax.ShapeDtypeStruct((e, m, ff), BF16),
        scratch_shapes=[pltpu.VMEM((d, tf), BF16), pltpu.VMEM((d, tf), BF16)],
        compiler_params=_params(sem),
        name="expert_up",
    )(xs, wg, wu)
    td = min(1024, m)
    return pl.pallas_call(
        _expert_down_kernel,
        grid=(e, d // tn, m // td),
        in_specs=[
            pl.BlockSpec((1, td, ff), lambda ei, j, i: (ei, i, 0)),
            pl.BlockSpec((1, 1, ff, tn), lambda ei, j, i: (layer, ei, 0, j)),
            pl.BlockSpec((1, td, LANES), lambda ei, j, i: (ei, i, 0)),
        ],
        out_specs=pl.BlockSpec((1, td, tn), lambda ei, j, i: (ei, i, j)),
        out_shape=jax.ShapeDtypeStruct((e, m, d), BF16),
        scratch_shapes=[pltpu.VMEM((ff, tn), BF16)],
        compiler_params=_params(sem),
        name="expert_down",
    )(hidden, wd, gate)


def _scatter_kernel(cum_ref, rnd_ref, slott_ref, ys_ref, x_ref, g_ref, *rest, cap, win, n_tiles, next_norm):
    if next_norm:
        nw_ref, sh_ref, sc_ref, o_ref, hn_ref = rest
    else:
        (o_ref,) = rest
    b = pl.program_id(0)
    j = pl.program_id(1)
    st = slott_ref[0]
    per_block = LANES // win
    lane = lax.broadcasted_iota(I32, (SEL_TILE, LANES), 1)
    lane_f = lane.astype(F32)
    group = lane // win

    def round_body(r, acc):
        blocks, pieces = [], []
        for k in range(N_EXPERTS // per_block):
            target = None
            for i in range(per_block):
                e = k * per_block + i
                first, start = _window(cum_ref, b, e, j, r, n_tiles, cap, win)
                col = st[:, e:e + 1]
                tgt = jnp.where(col >= first.astype(F32), col - (start - i * win).astype(F32), -1.0)
                target = tgt if target is None else jnp.where(group == i, tgt, target)
                pieces.append(ys_ref[e, 0, pl.ds(start, win), :])
            blocks.append(jnp.where(lane_f == target, 1.0, 0.0).astype(BF16))
        onehot = jnp.concatenate(blocks, axis=1)
        stacked = jnp.concatenate(pieces, axis=0)
        return acc + _dot(onehot, stacked)

    acc = lax.fori_loop(0, rnd_ref[b * n_tiles + j], round_body, jnp.zeros(o_ref.shape[1:], F32))
    x2 = x_ref[0] + g_ref[0] * acc
    o_ref[0] = x2
    if next_norm:
        hn_ref[0] = _prenorm_math(x2, nw_ref[...], sh_ref[0], sc_ref[0]).astype(BF16)


def _scatter(cum, rounds, slott, ys, x1, gate, cap, next_norm=None):
    b, n, d = x1.shape
    n_tiles = n // SEL_TILE
    win = min(EC_WINDOW, cap)
    tile = pl.BlockSpec((1, SEL_TILE, d), lambda i, t, c, r: (i, t, 0))
    vec = pl.BlockSpec((1, 1, d), lambda i, t, c, r: (i, 0, 0))
    in_specs = [
        pl.BlockSpec((1, SEL_TILE, LANES), lambda i, t, c, r: (i, t, 0)),
        pl.BlockSpec((N_EXPERTS, 1, cap, d), lambda i, t, c, r: (0, i, 0, 0), pipeline_mode=pl.Buffered(1)),
        tile, vec,
    ]
    args = [cum, rounds, slott, ys, x1, gate]
    out_specs, out_shape = tile, jax.ShapeDtypeStruct((b, n, d), F32)
    if next_norm is not None:
        in_specs += [pl.BlockSpec((1, d), lambda i, t, c, r: (0, 0)), vec, vec]
        args += [next_norm[0].reshape(1, d), next_norm[1], next_norm[2]]
        out_specs, out_shape = [tile, tile], [out_shape, jax.ShapeDtypeStruct((b, n, d), BF16)]
    grid_spec = pltpu.PrefetchScalarGridSpec(
        num_scalar_prefetch=2, grid=(b, n_tiles), in_specs=in_specs, out_specs=out_specs)
    return pl.pallas_call(
        functools.partial(_scatter_kernel, cap=cap, win=win, n_tiles=n_tiles, next_norm=next_norm is not None),
        grid_spec=grid_spec,
        out_shape=out_shape,
        compiler_params=_params(("parallel", "arbitrary")),
        name="ec_scatter",
    )(*args)


def _mod_chunks(mod_rows):
    b = mod_rows.shape[0]
    return [mod_rows[:, i * D_MODEL:(i + 1) * D_MODEL].reshape(b, 1, D_MODEL) for i in range(N_MOD)]


def _token_mixer(h, zmix, att, lw, dft):
    b, n, d = h.shape
    y_conv, y_pool = _stencil(zmix, lw["conv_w"], lw["pool_w"], lw["pool_scale"])
    y_f = _fourier(zmix, *dft)
    branches = [y.reshape(b * n, MIX_W) for y in (y_conv, att, y_f, y_pool)]
    merged = _merge(h.reshape(b * n, d), branches, lw["w_in"], lw["w_branch"], lw["layer"])
    return merged.reshape(b, n, d)


def _ffn(x1, h2, aff, affn, g2, lw, next_norm=None):
    b, n, d = x1.shape
    cap = EC_CAPACITY * n // N_EXPERTS
    slot, slott = _select(aff, cap)
    per_tile = jnp.sum((slot >= 0).reshape(b, N_EXPERTS, n // SEL_TILE, SEL_TILE), axis=-1, dtype=I32)
    cum = jnp.concatenate([jnp.zeros((b, N_EXPERTS, 1), I32), jnp.cumsum(per_tile, axis=-1)], axis=-1)
    win = min(EC_WINDOW, cap)
    span = cum[..., 1:] - (cum[..., :-1] // ROW_ALIGN) * ROW_ALIGN
    rounds = jnp.max((span + win - 1) // win, axis=1).reshape(-1)
    cum = cum.reshape(-1)
    xs, gate = _gather(cum, rounds, slot, h2, affn, cap)
    ys = _experts(xs.reshape(N_EXPERTS, b * cap, d), gate.reshape(N_EXPERTS, b * cap, LANES),
                  lw["w_gate"], lw["w_up"], lw["w_down"], lw["layer"])
    return _scatter(cum, rounds, slott, ys.reshape(N_EXPERTS, b, cap, d), x1, g2, cap, next_norm)


def kernel(x, c, ctx, c_ctx, w_ada, b_ada, norm1_w, norm2_w, w_in, conv_w, q_norm_w, k_norm_w,
           na_rpb, pool_w, pool_scale, w_branch, w_out, w_router, w_exp_gate, w_exp_up, w_exp_down):
    bsz, n, d = x.shape
    l_ctx = ctx.shape[1]
    depth = w_ada.shape[0]
    n_rows = n // GRID_W

    c16 = jnp.zeros((16, d), F32).at[:bsz].set(c).at[bsz].set(c_ctx)
    mod_all = _ada(c16, w_ada, b_ada)

    hd = np.arange(MIX_W) // HEAD_DIM
    bd = jnp.asarray(hd[:, None] == hd[None, :], dtype=BF16)
    dft_lat = _dft_mats(n, 1.0, n // 2) + _dft_mats(FOURIER_GC, (n * FOURIER_GC) ** -0.5)
    dft_ctx = _dft_mats(l_ctx, 1.0, l_ctx // 2) + _dft_mats(FOURIER_GC, (l_ctx * FOURIER_GC) ** -0.5)

    mods = [_mod_chunks(mod_all[l, :bsz]) for l in range(depth)]
    h = _prenorm(x, norm1_w[0], mods[0][0], mods[0][1])
    for l in range(depth):
        last = l == depth - 1
        lw = {
            "w_in": w_in, "conv_w": conv_w[l], "pool_w": pool_w[l],
            "pool_scale": pool_scale[l], "w_branch": w_branch,
            "w_gate": w_exp_gate, "w_up": w_exp_up, "w_down": w_exp_down, "layer": l,
        }
        w_out_bf = w_out[l].astype(BF16)
        wr_pad = jnp.zeros((d, LANES), F32).at[:, :N_EXPERTS].set(w_router[l])
        wr_hi = wr_pad.astype(BF16)
        wr_cat = jnp.concatenate([wr_hi, (wr_pad - wr_hi.astype(F32)).astype(BF16)], axis=1)
        q_w = jnp.tile(q_norm_w[l], N_HEADS).reshape(1, MIX_W)
        k_w = jnp.tile(k_norm_w[l], N_HEADS).reshape(1, MIX_W)
        tb = _na_bias_table(na_rpb[l], n_rows)
        _, _, g1, sh2, sc2, g2 = mods[l]
        mc = _mod_chunks(jnp.broadcast_to(mod_all[l, bsz:bsz + 1], (bsz, N_MOD * d)))

        hc = _prenorm(ctx, norm1_w[l], mc[0], mc[1])
        zc_mix = _in_proj(hc.reshape(bsz * l_ctx, d), w_in, l, MIX_COLS).reshape(bsz, l_ctx, MIX_COLS)

        zmix = _in_proj(h.reshape(bsz * n, d), w_in, l, MIX_COLS).reshape(bsz, n, MIX_COLS)
        att = _neighbourhood_attention(zmix, zc_mix, q_w, k_w, bd, tb)
        merged = _token_mixer(h, zmix, att, lw, dft_lat)
        x1, h2, aff, affn = _out_proj(merged, w_out_bf, x, g1, norm2_w[l], sh2, sc2, wr_cat)
        if last:
            x = _ffn(x1, h2, aff, affn, g2, lw)
        else:
            x, h = _ffn(x1, h2, aff, affn, g2, lw, (norm1_w[l + 1], mods[l + 1][0], mods[l + 1][1]))

        if not last:
            att_c = _context_attention(zc_mix, q_w, k_w, bd)
            merged_c = _token_mixer(hc, zc_mix, att_c, lw, dft_ctx)
            c1, hc2, aff_c, affn_c = _out_proj(merged_c, w_out_bf, ctx, mc[2], norm2_w[l], mc[3], mc[4], wr_cat)
            ctx = _ffn(c1, hc2, aff_c, affn_c, mc[5], lw)
    return x
```

```python
import functools

import numpy as np
import jax
import jax.numpy as jnp
from jax import lax
from jax.experimental import pallas as pl
from jax.experimental.pallas import tpu as pltpu

F32 = jnp.float32
BF16 = jnp.bfloat16
I32 = jnp.int32
HIGHEST = lax.Precision.HIGHEST

D_MODEL = 2048
GRID_W = 64
MIX_W = D_MODEL // 4
N_HEADS = 8
HEAD_DIM = MIX_W // N_HEADS
WIN_H = 8
WIN_W = 16
ATT_SCALE = HEAD_DIM ** -0.5
POOL_WINDOWS = (2, 4, 8, 16)
POOL_GC = MIX_W // len(POOL_WINDOWS)
FOURIER_GC = MIX_W // 4
N_EXPERTS = 16
EC_CAPACITY = 2
N_MOD = 6
EPS = 1e-6
NEG_INF = -1e30
MIX_COLS = 8 * MIX_W
LANES = 128
HALO = 16
SEL_TILE = 256
ROW_ALIGN = 16
EC_WINDOW = 64
ROWS_PER_PASS = 2
SCATTER_CHUNKS = 4
GATHER_CHUNKS = 4
VMEM_LIMIT = 56 * 1024 * 1024

COL_XA, COL_GB, COL_GC, COL_Q, COL_K, COL_V, COL_F, COL_P = range(8)


def _params(sem, vmem=VMEM_LIMIT):
    return pltpu.CompilerParams(dimension_semantics=sem, vmem_limit_bytes=vmem)


def _dot(a, b):
    return jnp.dot(a, b, preferred_element_type=F32)


def _dot_nt(a, b):
    return lax.dot_general(a, b, (((1,), (1,)), ((), ())), preferred_element_type=F32)


def _ada_kernel(c_ref, w_ref, b_ref, o_ref):
    c = c_ref[...]
    s = c * jax.nn.sigmoid(c)
    o_ref[0] = jnp.dot(s, w_ref[0], precision=HIGHEST, preferred_element_type=F32) + b_ref[0]


def _ada(c16, w_ada, b_ada):
    depth, d, ncol = w_ada.shape
    tn = 1024
    return pl.pallas_call(
        _ada_kernel,
        grid=(depth, ncol // tn),
        in_specs=[
            pl.BlockSpec((16, d), lambda l, j: (0, 0)),
            pl.BlockSpec((1, d, tn), lambda l, j: (l, 0, j)),
            pl.BlockSpec((1, 1, tn), lambda l, j: (l, 0, j)),
        ],
        out_specs=pl.BlockSpec((1, 16, tn), lambda l, j: (l, 0, j)),
        out_shape=jax.ShapeDtypeStruct((depth, 16, ncol), F32),
        compiler_params=_params(("parallel", "parallel")),
        name="ada",
    )(c16, w_ada, b_ada.reshape(depth, 1, ncol))


def _prenorm_math(x, w, shift, scale):
    ms = jnp.mean(x * x, axis=-1, keepdims=True)
    y = x * lax.rsqrt(ms + EPS) * w
    return y * (1.0 + scale) + shift


def _prenorm_kernel(x_ref, w_ref, sh_ref, sc_ref, o_ref):
    o_ref[0] = _prenorm_math(x_ref[0], w_ref[...], sh_ref[0], sc_ref[0]).astype(BF16)


def _prenorm(x, w, shift, scale):
    b, n, d = x.shape
    tm = min(512, n)
    return pl.pallas_call(
        _prenorm_kernel,
        grid=(b, n // tm),
        in_specs=[
            pl.BlockSpec((1, tm, d), lambda i, j: (i, j, 0)),
            pl.BlockSpec((1, d), lambda i, j: (0, 0)),
            pl.BlockSpec((1, 1, d), lambda i, j: (i, 0, 0)),
            pl.BlockSpec((1, 1, d), lambda i, j: (i, 0, 0)),
        ],
        out_specs=pl.BlockSpec((1, tm, d), lambda i, j: (i, j, 0)),
        out_shape=jax.ShapeDtypeStruct((b, n, d), BF16),
        compiler_params=_params(("parallel", "parallel")),
        name="prenorm",
    )(x, w.reshape(1, d), shift, scale)


def _mm_kernel(a_ref, w_ref, o_ref, w_bf):
    @pl.when(pl.program_id(1) == 0)
    def _():
        w_bf[...] = w_ref[0].astype(BF16)

    o_ref[...] = _dot(a_ref[...], w_bf[...]).astype(o_ref.dtype)


def _in_proj(h2d, w, layer, ncols):
    m, k = h2d.shape
    tm = min(1024, m)
    tn = 1024
    return pl.pallas_call(
        _mm_kernel,
        grid=(ncols // tn, m // tm),
        in_specs=[
            pl.BlockSpec((tm, k), lambda j, i: (i, 0)),
            pl.BlockSpec((1, k, tn), lambda j, i: (layer, 0, j)),
        ],
        out_specs=pl.BlockSpec((tm, tn), lambda j, i: (i, j)),
        out_shape=jax.ShapeDtypeStruct((m, ncols), BF16),
        scratch_shapes=[pltpu.VMEM((k, tn), BF16)],
        compiler_params=_params(("parallel", "arbitrary")),
        name="in_proj",
    )(h2d, w)


def _stencil_kernel(xa_ref, gb_ref, gc_ref, zp_ref, xa_p, xa_n, gc_p, gc_n, zp_p, zp_n,
                    cw_ref, pw_ref, ps_ref, oc_ref, op_ref, *, tm, n_seq):
    i = pl.program_id(1)
    nt = pl.num_programs(1)
    rows = tm + 2 * HALO

    def ext(main_ref, prev_ref, next_ref):
        main = main_ref[0].astype(F32)
        prev = jnp.where(i > 0, prev_ref[0].astype(F32), 0.0)
        nxt = jnp.where(i < nt - 1, next_ref[0].astype(F32), 0.0)
        return jnp.concatenate([prev, main, nxt], axis=0)

    def shifted(a, d):
        if d == 0:
            return a[HALO:HALO + tm]
        return pltpu.roll(a, (-d) % rows, axis=0)[HALO:HALO + tm]

    u = ext(gc_ref, gc_p, gc_n) * ext(xa_ref, xa_p, xa_n)
    cw = cw_ref[...]
    y = cw[0:1] * shifted(u, -1) + cw[1:2] * shifted(u, 0) + cw[2:3] * shifted(u, 1)
    oc_ref[0] = (gb_ref[0].astype(F32) * y).astype(BF16)

    t = (i * tm + lax.broadcasted_iota(I32, (tm, 1), 0)).astype(F32)
    zp = ext(zp_ref, zp_p, zp_n)
    for g, win in enumerate(POOL_WINDOWS):
        ug = zp[:, g * POOL_GC:(g + 1) * POOL_GC]
        acc = ug + pltpu.roll(ug, 1, axis=0)
        half = 1
        while 2 * half < win:
            acc = pltpu.roll(acc, rows - half, axis=0) + pltpu.roll(acc, half, axis=0)
            half *= 2
        wsum = acc[HALO:HALO + tm]
        lo = jnp.maximum(t - (win // 2), 0.0)
        hi = jnp.minimum(t + (win - win // 2 - 1), float(n_seq - 1))
        pooled = wsum / (hi - lo + 1.0) - ug[HALO:HALO + tm]
        yg = _dot(pooled.astype(BF16), pw_ref[g])
        op_ref[0, :, g * POOL_GC:(g + 1) * POOL_GC] = (
            yg * ps_ref[:, g * POOL_GC:(g + 1) * POOL_GC]).astype(BF16)


def _stencil(zmix, conv_w, pool_w, pool_scale):
    b, n, _ = zmix.shape
    tm = min(512, n)
    hb = tm // HALO
    last = n // HALO - 1

    def main(col):
        return pl.BlockSpec((1, tm, MIX_W), lambda bi, i: (bi, i, col))

    def prev(col):
        return pl.BlockSpec((1, HALO, MIX_W), lambda bi, i: (bi, jnp.maximum(i * hb - 1, 0), col))

    def nxt(col):
        return pl.BlockSpec((1, HALO, MIX_W), lambda bi, i: (bi, jnp.minimum((i + 1) * hb, last), col))

    out_spec = pl.BlockSpec((1, tm, MIX_W), lambda bi, i: (bi, i, 0))
    return pl.pallas_call(
        functools.partial(_stencil_kernel, tm=tm, n_seq=n),
        grid=(b, n // tm),
        in_specs=[main(COL_XA), main(COL_GB), main(COL_GC), main(COL_P),
                  prev(COL_XA), nxt(COL_XA), prev(COL_GC), nxt(COL_GC), prev(COL_P), nxt(COL_P),
                  pl.BlockSpec((3, MIX_W), lambda bi, i: (0, 0)),
                  pl.BlockSpec((len(POOL_WINDOWS), POOL_GC, POOL_GC), lambda bi, i: (0, 0, 0)),
                  pl.BlockSpec((1, MIX_W), lambda bi, i: (0, 0))],
        out_specs=[out_spec, out_spec],
        out_shape=[jax.ShapeDtypeStruct((b, n, MIX_W), BF16)] * 2,
        compiler_params=_params(("parallel", "parallel")),
        name="stencil",
    )(zmix, zmix, zmix, zmix, zmix, zmix, zmix, zmix, zmix, zmix,
      conv_w, pool_w.astype(BF16), pool_scale.reshape(1, MIX_W))


def _fourier_kernel(cn_ref, sn_ref, u_ref, cc_ref, sc_ref, o_ref, even, odd):
    n = u_ref.shape[1]
    half, tk = n // 2, cn_ref.shape[0]
    rb = min(256, half)

    @pl.when(pl.program_id(1) == 0)
    def _():
        i = lax.broadcasted_iota(I32, (rb, rb), 0)
        j = lax.broadcasted_iota(I32, (rb, rb), 1)
        flip = jnp.where(i + j == rb, 1.0, 0.0).astype(BF16)
        row = lax.broadcasted_iota(I32, (rb, 1), 0)
        for a in range(half // rb):
            src = n - (a + 1) * rb
            rev = _dot(flip, u_ref[0, src:src + rb, :])
            first = (n - a * rb) % n
            rev = jnp.where(row == 0, u_ref[0, first:first + 1, :].astype(F32), rev)
            ua = u_ref[0, a * rb:(a + 1) * rb, :].astype(F32)
            fold = ua + rev if a else jnp.where(row == 0, ua, ua + rev)
            even[a * rb:(a + 1) * rb, :] = fold.astype(BF16)
            odd[a * rb:(a + 1) * rb, :] = (ua - rev).astype(BF16)

    k = pl.program_id(1) * tk + lax.broadcasted_iota(I32, (tk, 1), 0)
    sign = (1 - 2 * jnp.bitwise_and(k, 1)).astype(F32)
    p = (_dot(cn_ref[...], even[...]) + sign * u_ref[0, half:half + 1, :].astype(F32)).astype(BF16)
    q = _dot(sn_ref[...], odd[...]).astype(BF16)
    for g in range(MIX_W // FOURIER_GC):
        sl = slice(g * FOURIER_GC, (g + 1) * FOURIER_GC)
        o_ref[0, :, sl] = (_dot(p[:, sl], cc_ref[...]) - _dot(q[:, sl], sc_ref[...])).astype(BF16)


def _dft_mats(n, scale, ncols=None):
    ncols = n if ncols is None else ncols
    k = jnp.arange(n, dtype=I32)[:, None]

    def table(count, period):
        r = (k * jnp.arange(count, dtype=I32)[None, :]) % period
        ang = r.astype(F32) * np.float32(2.0 * np.pi / period)
        return jnp.cos(ang), jnp.sin(ang)

    inner = 64
    if n <= inner or n % inner or ncols % inner:
        c, s = table(ncols, n)
    else:
        ca, sa = table(ncols // inner, n // inner)
        cb, sb = table(inner, n)
        c = (ca[:, :, None] * cb[:, None, :] - sa[:, :, None] * sb[:, None, :]).reshape(n, ncols)
        s = (sa[:, :, None] * cb[:, None, :] + ca[:, :, None] * sb[:, None, :]).reshape(n, ncols)
    return (c * scale).astype(BF16), (s * scale).astype(BF16)


def _fourier(zmix, cn, sn, cc, sc):
    b, n, _ = zmix.shape
    half = n // 2
    tk = min(512, n)
    return pl.pallas_call(
        _fourier_kernel,
        grid=(b, n // tk),
        in_specs=[
            pl.BlockSpec((tk, half), lambda bi, k: (k, 0)),
            pl.BlockSpec((tk, half), lambda bi, k: (k, 0)),
            pl.BlockSpec((1, n, MIX_W), lambda bi, k: (bi, 0, COL_F)),
            pl.BlockSpec((FOURIER_GC, FOURIER_GC), lambda bi, k: (0, 0)),
            pl.BlockSpec((FOURIER_GC, FOURIER_GC), lambda bi, k: (0, 0)),
        ],
        out_specs=pl.BlockSpec((1, tk, MIX_W), lambda bi, k: (bi, k, 0)),
        out_shape=jax.ShapeDtypeStruct((b, n, MIX_W), BF16),
        scratch_shapes=[pltpu.VMEM((half, MIX_W), BF16), pltpu.VMEM((half, MIX_W), BF16)],
        compiler_params=_params(("parallel", "arbitrary")),
        name="fourier",
    )(cn, sn, zmix, cc, sc)


def _head_rms(x, w, bd):
    x2 = x * x
    hi = x2.astype(BF16)
    lo = (x2 - hi.astype(F32)).astype(BF16)
    ms = (_dot(hi, bd) + _dot(lo, bd)) * (1.0 / HEAD_DIM)
    return x * lax.rsqrt(ms + EPS) * w


def _pair_stack(q2):
    lane = lax.broadcasted_iota(I32, q2.shape, 1)
    zero = jnp.zeros_like(q2)
    return jnp.concatenate([jnp.where(lane < HEAD_DIM, q2, zero),
                            jnp.where(lane >= HEAD_DIM, q2, zero)], axis=0)


def _pair_unstack(o2, m):
    lane = lax.broadcasted_iota(I32, (m, LANES), 1)
    return jnp.where(lane < HEAD_DIM, o2[:m], o2[m:])


def _na_kernel(q_ref, k_ref, v_ref, kc_ref, vc_ref, qw_ref, kw_ref, bd_ref, tb_ref, o_ref,
               kn_scr, kcn_scr, qn_scr, *, rt, n_rows):
    rb = pl.program_id(1)
    n_seq = n_rows * GRID_W
    chunk = 512

    @pl.when(rb == 0)
    def _():
        def body(c, carry):
            sl = pl.ds(pl.multiple_of(c * chunk, chunk), chunk)
            kn_scr[sl, :] = _head_rms(k_ref[0, sl, :].astype(F32), kw_ref[...], bd_ref[...]).astype(BF16)
            return carry
        lax.fori_loop(0, n_seq // chunk, body, 0)
        kcn_scr[...] = _head_rms(kc_ref[0].astype(F32), kw_ref[...], bd_ref[...]).astype(BF16)

    qn_scr[...] = (_head_rms(q_ref[0].astype(F32), qw_ref[...], bd_ref[...]) * ATT_SCALE).astype(BF16)

    kh = min(WIN_H, n_rows)
    nloc = kh * GRID_W

    def row_body(jj, carry):
        units = []
        for i in range(ROWS_PER_PASS):
            j = jj * ROWS_PER_PASS + i
            r = rb * rt + j
            rs = jnp.clip(r - kh // 2, 0, n_rows - kh)
            q_rows = pl.ds(pl.multiple_of(j * GRID_W, GRID_W), GRID_W)
            k_rows = pl.ds(pl.multiple_of(rs * GRID_W, GRID_W), nloc)
            for p in range(N_HEADS // 2):
                units.append((q_rows, k_rows, rs - r + (WIN_H - 1), p, slice(p * LANES, (p + 1) * LANES)))
        qs = [_pair_stack(qn_scr[q_rows, cols]) for q_rows, _, _, _, cols in units]
        s_loc = [_dot_nt(q, kn_scr[k_rows, cols]) for q, (_, k_rows, _, _, cols) in zip(qs, units)]
        s_ctx = [_dot_nt(q, kcn_scr[:, cols]) for q, (_, _, _, _, cols) in zip(qs, units)]
        s_loc = [s + tb_ref[dr0, p] for s, (_, _, dr0, p, _) in zip(s_loc, units)]
        m = [jnp.maximum(jnp.max(a, axis=1, keepdims=True), jnp.max(c, axis=1, keepdims=True))
             for a, c in zip(s_loc, s_ctx)]
        p_loc = [jnp.exp(a - mx) for a, mx in zip(s_loc, m)]
        p_ctx = [jnp.exp(c - mx) for c, mx in zip(s_ctx, m)]
        denom = [jnp.sum(a, axis=1, keepdims=True) + jnp.sum(c, axis=1, keepdims=True) for a, c in zip(p_loc, p_ctx)]
        o2 = [_dot(a.astype(BF16), v_ref[0, k_rows, cols]) + _dot(c.astype(BF16), vc_ref[0, :, cols])
              for a, c, (_, k_rows, _, _, cols) in zip(p_loc, p_ctx, units)]
        for o, d, (q_rows, _, _, _, cols) in zip(o2, denom, units):
            o_ref[0, q_rows, cols] = _pair_unstack(o / d, GRID_W).astype(BF16)
        return carry

    lax.fori_loop(0, rt // ROWS_PER_PASS, row_body, 0)


def _na_bias_table(rpb, n_rows):
    kh = min(WIN_H, n_rows)
    c = np.arange(GRID_W)
    cs = np.clip(c - WIN_W // 2, 0, GRID_W - WIN_W)
    kc = np.arange(GRID_W)
    ok = (kc[None, :] >= cs[:, None]) & (kc[None, :] < cs[:, None] + WIN_W)
    edge = GRID_W - WIN_W
    padded = jnp.pad(rpb.astype(F32), ((0, 0), (0, 0), (edge, edge)))
    toeplitz = jnp.stack([padded[:, :, GRID_W - 1 - q:2 * GRID_W - 1 - q] for q in range(GRID_W)], axis=2)
    masked = jnp.where(ok[None, None], toeplitz, NEG_INF)
    t = jnp.stack([masked[:, d0:d0 + kh] for d0 in range(WIN_H)], axis=0)
    t = jnp.transpose(t, (0, 1, 3, 2, 4))
    return t.reshape(WIN_H, N_HEADS // 2, 2 * GRID_W, kh * GRID_W)


def _neighbourhood_attention(zmix, zc_mix, q_w, k_w, bd, tb):
    b, n, _ = zmix.shape
    l = zc_mix.shape[1]
    n_rows = n // GRID_W
    rt = 8
    nloc = min(WIN_H, n_rows) * GRID_W
    return pl.pallas_call(
        functools.partial(_na_kernel, rt=rt, n_rows=n_rows),
        grid=(b, n_rows // rt),
        in_specs=[
            pl.BlockSpec((1, rt * GRID_W, MIX_W), lambda bi, r: (bi, r, COL_Q)),
            pl.BlockSpec((1, n, MIX_W), lambda bi, r: (bi, 0, COL_K)),
            pl.BlockSpec((1, n, MIX_W), lambda bi, r: (bi, 0, COL_V)),
            pl.BlockSpec((1, l, MIX_W), lambda bi, r: (bi, 0, COL_K)),
            pl.BlockSpec((1, l, MIX_W), lambda bi, r: (bi, 0, COL_V)),
            pl.BlockSpec((1, MIX_W), lambda bi, r: (0, 0)),
            pl.BlockSpec((1, MIX_W), lambda bi, r: (0, 0)),
            pl.BlockSpec((MIX_W, MIX_W), lambda bi, r: (0, 0)),
            pl.BlockSpec((WIN_H, N_HEADS // 2, 2 * GRID_W, nloc), lambda bi, r: (0, 0, 0, 0),
                         pipeline_mode=pl.Buffered(1)),
        ],
        out_specs=pl.BlockSpec((1, rt * GRID_W, MIX_W), lambda bi, r: (bi, r, 0)),
        out_shape=jax.ShapeDtypeStruct((b, n, MIX_W), BF16),
        scratch_shapes=[pltpu.VMEM((n, MIX_W), BF16), pltpu.VMEM((l, MIX_W), BF16),
                        pltpu.VMEM((rt * GRID_W, MIX_W), BF16)],
        compiler_params=_params(("parallel", "arbitrary")),
        name="nbr_attention",
    )(zmix, zmix, zmix, zc_mix, zc_mix, q_w, k_w, bd, tb)


def _ctx_attn_kernel(q_ref, k_ref, v_ref, qw_ref, kw_ref, bd_ref, o_ref):
    l = q_ref.shape[1]
    qn = (_head_rms(q_ref[0].astype(F32), qw_ref[...], bd_ref[...]) * ATT_SCALE).astype(BF16)
    kn = _head_rms(k_ref[0].astype(F32), kw_ref[...], bd_ref[...]).astype(BF16)
    for p in range(N_HEADS // 2):
        cols = slice(p * LANES, (p + 1) * LANES)
        qs = _pair_stack(qn[:, cols])
        s = _dot_nt(qs, kn[:, cols])
        m = jnp.max(s, axis=1, keepdims=True)
        e = jnp.exp(s - m)
        o2 = _dot(e.astype(BF16), v_ref[0, :, cols]) / jnp.sum(e, axis=1, keepdims=True)
        o_ref[0, :, cols] = _pair_unstack(o2, l).astype(BF16)


def _context_attention(zc_mix, q_w, k_w, bd):
    b, l, _ = zc_mix.shape
    vec = pl.BlockSpec((1, MIX_W), lambda bi: (0, 0))
    return pl.pallas_call(
        _ctx_attn_kernel,
        grid=(b,),
        in_specs=[
            pl.BlockSpec((1, l, MIX_W), lambda bi: (bi, 0, COL_Q)),
            pl.BlockSpec((1, l, MIX_W), lambda bi: (bi, 0, COL_K)),
            pl.BlockSpec((1, l, MIX_W), lambda bi: (bi, 0, COL_V)),
            vec, vec,
            pl.BlockSpec((MIX_W, MIX_W), lambda bi: (0, 0)),
        ],
        out_specs=pl.BlockSpec((1, l, MIX_W), lambda bi: (bi, 0, 0)),
        out_shape=jax.ShapeDtypeStruct((b, l, MIX_W), BF16),
        compiler_params=_params(("parallel",)),
        name="ctx_attention",
    )(zc_mix, zc_mix, zc_mix, q_w, k_w, bd)


def _merge_kernel(h_ref, y0, y1, y2, y3, g0, g1, g2, g3, wb_ref, o_ref, wg_bf, wb_bf):
    @pl.when(pl.program_id(1) == 0)
    def _():
        for i, g in enumerate((g0, g1, g2, g3)):
            wg_bf[i] = g[0].astype(BF16)
        wb_bf[...] = wb_ref[0].astype(BF16)

    h = h_ref[...]
    acc = None
    for i, y in enumerate((y0, y1, y2, y3)):
        gate = jax.nn.sigmoid(_dot(h, wg_bf[i]))
        term = gate * _dot(y[...], wb_bf[i])
        acc = term if acc is None else acc + term
    o_ref[...] = acc.astype(BF16)


def _merge(h2d, branches, w_in, w_branch, layer):
    m, d = h2d.shape
    tm = min(1024, m)
    tn = 256
    gate0 = MIX_COLS // tn

    def gspec(i):
        return pl.BlockSpec((1, d, tn), lambda n, r: (layer, 0, gate0 + i * (d // tn) + n))

    yspec = pl.BlockSpec((tm, MIX_W), lambda n, r: (r, 0))
    return pl.pallas_call(
        _merge_kernel,
        grid=(d // tn, m // tm),
        in_specs=[pl.BlockSpec((tm, d), lambda n, r: (r, 0)), yspec, yspec, yspec, yspec,
                  gspec(0), gspec(1), gspec(2), gspec(3),
                  pl.BlockSpec((1, 4, MIX_W, tn), lambda n, r: (layer, 0, 0, n))],
        out_specs=pl.BlockSpec((tm, tn), lambda n, r: (r, n)),
        out_shape=jax.ShapeDtypeStruct((m, d), BF16),
        scratch_shapes=[pltpu.VMEM((4, d, tn), BF16), pltpu.VMEM((4, MIX_W, tn), BF16)],
        compiler_params=_params(("parallel", "arbitrary")),
        name="merge",
    )(h2d, *branches, w_in, w_in, w_in, w_in, w_branch)


def _out_proj_kernel(m_ref, w_ref, x_ref, g_ref, nw_ref, sh_ref, sc_ref, wrc_ref,
                     x1_ref, h2_ref, aff_ref, affn_ref):
    x1 = x_ref[0] + g_ref[0] * _dot(m_ref[0], w_ref[...])
    x1_ref[0] = x1
    h2 = _prenorm_math(x1, nw_ref[...], sh_ref[0], sc_ref[0])
    h_hi = h2.astype(BF16)
    h2_ref[0] = h_hi
    h_lo = (h2 - h_hi.astype(F32)).astype(BF16)
    both = _dot(h_hi, wrc_ref[...])
    logits = both[:, :LANES] + (both[:, LANES:] + _dot(h_lo, wrc_ref[:, :LANES]))
    lane = lax.broadcasted_iota(I32, logits.shape, 1)
    logits = jnp.where(lane < N_EXPERTS, logits, NEG_INF)
    e = jnp.exp(logits - jnp.max(logits, axis=1, keepdims=True))
    aff = e / jnp.sum(e, axis=1, keepdims=True)
    aff_ref[0] = aff.T[:N_EXPERTS]
    hi = aff.astype(BF16).astype(F32)
    r1 = aff - hi
    mid = r1.astype(BF16).astype(F32)
    lo = (r1 - mid).astype(BF16).astype(F32)
    packed = hi + pltpu.roll(mid, N_EXPERTS, axis=1) + pltpu.roll(lo, 2 * N_EXPERTS, axis=1)
    affn_ref[0] = packed.astype(BF16)


def _out_proj(merged, w_out_bf, x, gate, norm_w, shift, scale, wr_cat):
    b, n, d = x.shape
    tm = min(512, n)
    vec = pl.BlockSpec((1, 1, d), lambda i, j: (i, 0, 0))
    tile = pl.BlockSpec((1, tm, d), lambda i, j: (i, j, 0))
    wr_spec = pl.BlockSpec((d, 2 * LANES), lambda i, j: (0, 0))
    return pl.pallas_call(
        _out_proj_kernel,
        grid=(b, n // tm),
        in_specs=[tile, pl.BlockSpec((d, d), lambda i, j: (0, 0), pipeline_mode=pl.Buffered(1)), tile, vec,
                  pl.BlockSpec((1, d), lambda i, j: (0, 0)), vec, vec, wr_spec],
        out_specs=[tile, tile,
                   pl.BlockSpec((1, N_EXPERTS, tm), lambda i, j: (i, 0, j)),
                   pl.BlockSpec((1, tm, LANES), lambda i, j: (i, j, 0))],
        out_shape=[jax.ShapeDtypeStruct((b, n, d), F32), jax.ShapeDtypeStruct((b, n, d), BF16),
                   jax.ShapeDtypeStruct((b, N_EXPERTS, n), F32), jax.ShapeDtypeStruct((b, n, LANES), BF16)],
        compiler_params=_params(("parallel", "parallel")),
        name="out_proj",
    )(merged, w_out_bf, x, gate, norm_w.reshape(1, d), shift, scale, wr_cat)


def _lane_cumsum(mask_f, ut):
    rows, n = mask_f.shape
    carry = jnp.zeros((rows, 1), F32)
    parts = []
    for j in range(n // LANES):
        c = _dot(mask_f[:, j * LANES:(j + 1) * LANES].astype(BF16), ut) + carry
        parts.append(c)
        carry = c[:, LANES - 1:LANES]
    return jnp.concatenate(parts, axis=1)


def _select_kernel(aff_ref, slot_ref, slott_ref, *, cap):
    aff = aff_ref[0]
    e, n = aff.shape
    bits = lax.bitcast_convert_type(aff, I32)

    def body(_, carry):
        lo, hi = carry
        mid = lo + ((hi - lo) >> 1)
        cnt = jnp.sum(jnp.where(bits >= mid, 1.0, 0.0), axis=1, keepdims=True)
        ge = cnt >= float(cap)
        return jnp.where(ge, mid, lo), jnp.where(ge, hi, mid)

    lo, _ = lax.fori_loop(0, 31, body, (jnp.zeros((e, 1), I32), jnp.full((e, 1), 0x7F800000, I32)))
    ri = lax.broadcasted_iota(I32, (LANES, LANES), 0)
    ci = lax.broadcasted_iota(I32, (LANES, LANES), 1)
    ut = jnp.where(ri <= ci, 1.0, 0.0).astype(BF16)
    gt = bits > lo
    eq = bits == lo
    need = float(cap) - jnp.sum(jnp.where(gt, 1.0, 0.0), axis=1, keepdims=True)
    ceq = _lane_cumsum(jnp.where(eq, 1.0, 0.0), ut)
    sel = gt | (eq & (ceq <= need))
    csel = _lane_cumsum(jnp.where(sel, 1.0, 0.0), ut)
    slot = jnp.where(sel, csel - 1.0, -1.0)
    slot_ref[0] = slot.astype(I32)
    padded = jnp.concatenate([slot, jnp.full((LANES - e, n), -1.0, F32)], axis=0)
    slott_ref[0] = padded.T


def _select(aff, cap):
    b, e, n = aff.shape
    return pl.pallas_call(
        functools.partial(_select_kernel, cap=cap),
        grid=(b,),
        in_specs=[pl.BlockSpec((1, e, n), lambda i: (i, 0, 0))],
        out_specs=[pl.BlockSpec((1, e, n), lambda i: (i, 0, 0)),
                   pl.BlockSpec((1, n, LANES), lambda i: (i, 0, 0))],
        out_shape=[jax.ShapeDtypeStruct((b, e, n), I32), jax.ShapeDtypeStruct((b, n, LANES), F32)],
        compiler_params=_params(("parallel",)),
        name="select",
    )(aff)


def _window(cum_ref, b, e, j, r, n_tiles, cap, win):
    c_lo = cum_ref[(b * N_EXPERTS + e) * (n_tiles + 1) + j]
    first = (c_lo // ROW_ALIGN) * ROW_ALIGN + r * win
    start = pl.multiple_of(jnp.minimum(first, cap - win), ROW_ALIGN)
    return first, start


def _gather_kernel(cum_ref, rnd_ref, slot_ref, h_ref, affn_ref, xs_ref, gate_ref, gacc, *, cap, win, n_tiles):
    b = pl.program_id(0)
    dq = pl.program_id(1)
    xs_ref[...] = jnp.zeros_like(xs_ref)

    @pl.when(dq == 0)
    def _():
        gacc[...] = jnp.zeros_like(gacc)

    rows = lax.broadcasted_iota(I32, (win, SEL_TILE), 0)

    def tile_body(j, carry):
        toks = pl.ds(pl.multiple_of(j * SEL_TILE, SEL_TILE), SEL_TILE)
        slots = slot_ref[0, j]
        h_tile = h_ref[0, toks, :]

        def round_body(r, carry2):
            starts, pieces = [], []
            for e in range(N_EXPERTS):
                first, start = _window(cum_ref, b, e, j, r, n_tiles, cap, win)
                s_row = slots[e:e + 1, :]
                hit = (rows + start == s_row) & (s_row >= first)
                pieces.append(jnp.where(hit, 1.0, 0.0).astype(BF16))
                starts.append(start)
            ge = N_EXPERTS // GATHER_CHUNKS
            onehots = [jnp.concatenate(pieces[lo:lo + ge], axis=0) for lo in range(0, N_EXPERTS, ge)]
            for c, onehot in enumerate(onehots):
                res = _dot(onehot, h_tile)
                for i in range(ge):
                    e = c * ge + i
                    sl = pl.ds(starts[e], win)
                    xs_ref[e, 0, sl, :] = (xs_ref[e, 0, sl, :].astype(F32) + res[i * win:(i + 1) * win]).astype(BF16)

            @pl.when(dq == 0)
            def _():
                for c, onehot in enumerate(onehots):
                    g = _dot(onehot, affn_ref[0, toks, :])
                    for i in range(ge):
                        gacc[c * ge + i, pl.ds(starts[c * ge + i], win), :] += g[i * win:(i + 1) * win]
            return carry2

        lax.fori_loop(0, rnd_ref[b * n_tiles + j], round_body, 0)
        return carry

    lax.fori_loop(0, n_tiles, tile_body, 0)

    @pl.when(dq == 0)
    def _():
        lane = lax.broadcasted_iota(I32, (cap, LANES), 1)
        for e in range(N_EXPERTS):
            mine = (lane == e) | (lane == N_EXPERTS + e) | (lane == 2 * N_EXPERTS + e)
            col = jnp.sum(jnp.where(mine, gacc[e], 0.0), axis=1, keepdims=True)
            gate_ref[e, 0] = jnp.broadcast_to(col, (cap, LANES))


def _gather(cum, rounds, slot, h2, affn, cap):
    b, n, d = h2.shape
    n_tiles = n // SEL_TILE
    win = min(EC_WINDOW, cap)
    dn = 512
    slot_tiles = slot.reshape(b, N_EXPERTS, n_tiles, SEL_TILE).transpose(0, 2, 1, 3)
    grid_spec = pltpu.PrefetchScalarGridSpec(
        num_scalar_prefetch=2,
        grid=(b, d // dn),
        in_specs=[
            pl.BlockSpec((1, n_tiles, N_EXPERTS, SEL_TILE), lambda i, q, c, r: (i, 0, 0, 0)),
            pl.BlockSpec((1, n, dn), lambda i, q, c, r: (i, 0, q)),
            pl.BlockSpec((1, n, LANES), lambda i, q, c, r: (i, 0, 0)),
        ],
        out_specs=[pl.BlockSpec((N_EXPERTS, 1, cap, dn), lambda i, q, c, r: (0, i, 0, q)),
                   pl.BlockSpec((N_EXPERTS, 1, cap, LANES), lambda i, q, c, r: (0, i, 0, 0))],
        scratch_shapes=[pltpu.VMEM((N_EXPERTS, cap, LANES), F32)],
    )
    return pl.pallas_call(
        functools.partial(_gather_kernel, cap=cap, win=win, n_tiles=n_tiles),
        grid_spec=grid_spec,
        out_shape=[jax.ShapeDtypeStruct((N_EXPERTS, b, cap, d), BF16),
                   jax.ShapeDtypeStruct((N_EXPERTS, b, cap, LANES), F32)],
        compiler_params=_params(("parallel", "arbitrary")),
        name="ec_gather",
    )(cum, rounds, slot_tiles, h2, affn)


def _expert_up_kernel(*refs, n_main, extra):
    if extra:
        xs_ref, xe_ref, wg_ref, wu_ref, h_ref, he_ref, wg_bf, wu_bf = refs
    else:
        xs_ref, wg_ref, wu_ref, h_ref, wg_bf, wu_bf = refs
    i = pl.program_id(2)

    @pl.when(i == 0)
    def _():
        wg_bf[...] = wg_ref[0, 0].astype(BF16)
        wu_bf[...] = wu_ref[0, 0].astype(BF16)

    def swiglu(x_ref, o_ref):
        x = x_ref[0]
        a = _dot(x, wg_bf[...])
        u = _dot(x, wu_bf[...])
        o_ref[0] = (a * jax.nn.sigmoid(a) * u).astype(BF16)

    if extra:
        pl.when(i < n_main)(lambda: swiglu(xs_ref, h_ref))
        pl.when(i == n_main)(lambda: swiglu(xe_ref, he_ref))
    else:
        swiglu(xs_ref, h_ref)


def _expert_down_kernel(*refs, n_main, extra):
    if extra:
        h_ref, he_ref, wd_ref, gate_ref, ge_ref, ys_ref, ye_ref, wd_bf = refs
    else:
        h_ref, wd_ref, gate_ref, ys_ref, wd_bf = refs
    i = pl.program_id(2)

    @pl.when(i == 0)
    def _():
        wd_bf[...] = wd_ref[0, 0].astype(BF16)

    def down(x_ref, g_ref, o_ref):
        y = _dot(x_ref[0], wd_bf[...])
        g = g_ref[0]
        o_ref[0] = (y * jnp.concatenate([g] * (y.shape[1] // LANES), axis=1)).astype(BF16)

    if extra:
        pl.when(i < n_main)(lambda: down(h_ref, gate_ref, ys_ref))
        pl.when(i == n_main)(lambda: down(he_ref, ge_ref, ye_ref))
    else:
        down(h_ref, gate_ref, ys_ref)


def _experts(xs, gate, wg, wu, wd, layer, extra=None):
    e, m, d = xs.shape
    ff = wg.shape[3]
    tf = 512
    tn = 1024
    sem = ("parallel", "parallel", "arbitrary")
    has_extra = extra is not None
    me = extra[0].shape[1] if has_extra else 0

    def specs(tm, width_in, tile_out, width_out):
        n_main = m // tm
        ins = [pl.BlockSpec((1, tm, width_in), lambda ei, j, i: (ei, jnp.minimum(i, n_main - 1), 0))]
        outs = [pl.BlockSpec((1, tm, tile_out), lambda ei, j, i: (ei, jnp.minimum(i, n_main - 1), j))]
        shapes = [jax.ShapeDtypeStruct((e, m, width_out), BF16)]
        if has_extra:
            ins.append(pl.BlockSpec((1, me, width_in), lambda ei, j, i: (ei, 0, 0)))
            outs.append(pl.BlockSpec((1, me, tile_out), lambda ei, j, i: (ei, 0, j)))
            shapes.append(jax.ShapeDtypeStruct((e, me, width_out), BF16))
        return n_main, ins, outs, shapes

    tm = min(1024, m)
    n_main, ins, outs, shapes = specs(tm, d, tf, ff)
    wspec = pl.BlockSpec((1, 1, d, tf), lambda ei, f, i: (layer, ei, 0, f))
    up_out = pl.pallas_call(
        functools.partial(_expert_up_kernel, n_main=n_main, extra=has_extra),
        grid=(e, ff // tf, n_main + has_extra),
        in_specs=ins + [wspec, wspec],
        out_specs=outs,
        out_shape=shapes,
        scratch_shapes=[pltpu.VMEM((d, tf), BF16), pltpu.VMEM((d, tf), BF16)],
        compiler_params=_params(sem),
        name="expert_up",
    )(*([xs] + ([extra[0]] if has_extra else []) + [wg, wu]))

    td = min(1024, m)
    n_main, ins, outs, shapes = specs(td, ff, tn, d)
    gspecs = [pl.BlockSpec((1, td, LANES), lambda ei, j, i: (ei, jnp.minimum(i, n_main - 1), 0))]
    gargs = [gate]
    if has_extra:
        gspecs.append(pl.BlockSpec((1, me, LANES), lambda ei, j, i: (ei, 0, 0)))
        gargs.append(extra[1])
    down_out = pl.pallas_call(
        functools.partial(_expert_down_kernel, n_main=n_main, extra=has_extra),
        grid=(e, d // tn, n_main + has_extra),
        in_specs=ins + [pl.BlockSpec((1, 1, ff, tn), lambda ei, j, i: (layer, ei, 0, j))] + gspecs,
        out_specs=outs,
        out_shape=shapes,
        scratch_shapes=[pltpu.VMEM((ff, tn), BF16)],
        compiler_params=_params(sem),
        name="expert_down",
    )(*(list(up_out) + [wd] + gargs))
    return tuple(down_out) if has_extra else down_out[0]


def _scatter_kernel(cum_ref, rnd_ref, slott_ref, ys_ref, x_ref, g_ref, *rest, cap, win, n_tiles, next_norm):
    if next_norm:
        nw_ref, sh_ref, sc_ref, o_ref, hn_ref = rest
    else:
        (o_ref,) = rest
    b = pl.program_id(0)
    j = pl.program_id(1)
    st = slott_ref[0]
    per_block = LANES // win
    lane = lax.broadcasted_iota(I32, (SEL_TILE, LANES), 1)
    lane_f = lane.astype(F32)
    group = lane // win

    def round_body(r, acc):
        blocks, pieces = [], []
        for k in range(N_EXPERTS // per_block):
            target = None
            for i in range(per_block):
                e = k * per_block + i
                first, start = _window(cum_ref, b, e, j, r, n_tiles, cap, win)
                col = st[:, e:e + 1]
                tgt = jnp.where(col >= first.astype(F32), col - (start - i * win).astype(F32), -1.0)
                target = tgt if target is None else jnp.where(group == i, tgt, target)
                pieces.append(ys_ref[e, 0, pl.ds(start, win), :])
            blocks.append(jnp.where(lane_f == target, 1.0, 0.0).astype(BF16))
        nb = max(1, len(blocks) // SCATTER_CHUNKS)
        for lo in range(0, len(blocks), nb):
            onehot = jnp.concatenate(blocks[lo:lo + nb], axis=1)
            stacked = jnp.concatenate(pieces[lo * per_block:(lo + nb) * per_block], axis=0)
            acc = acc + _dot(onehot, stacked)
        return acc

    acc = lax.fori_loop(0, rnd_ref[b * n_tiles + j], round_body, jnp.zeros(o_ref.shape[1:], F32))
    x2 = x_ref[0] + g_ref[0] * acc
    o_ref[0] = x2
    if next_norm:
        hn_ref[0] = _prenorm_math(x2, nw_ref[...], sh_ref[0], sc_ref[0]).astype(BF16)


def _scatter(cum, rounds, slott, ys, x1, gate, cap, next_norm=None):
    b, n, d = x1.shape
    n_tiles = n // SEL_TILE
    win = min(EC_WINDOW, cap)
    tile = pl.BlockSpec((1, SEL_TILE, d), lambda i, t, c, r: (i, t, 0))
    vec = pl.BlockSpec((1, 1, d), lambda i, t, c, r: (i, 0, 0))
    in_specs = [
        pl.BlockSpec((1, SEL_TILE, LANES), lambda i, t, c, r: (i, t, 0)),
        pl.BlockSpec((N_EXPERTS, 1, cap, d), lambda i, t, c, r: (0, i, 0, 0), pipeline_mode=pl.Buffered(1)),
        tile, vec,
    ]
    args = [cum, rounds, slott, ys, x1, gate]
    out_specs, out_shape = tile, jax.ShapeDtypeStruct((b, n, d), F32)
    if next_norm is not None:
        in_specs += [pl.BlockSpec((1, d), lambda i, t, c, r: (0, 0)), vec, vec]
        args += [next_norm[0].reshape(1, d), next_norm[1], next_norm[2]]
        out_specs, out_shape = [tile, tile], [out_shape, jax.ShapeDtypeStruct((b, n, d), BF16)]
    grid_spec = pltpu.PrefetchScalarGridSpec(
        num_scalar_prefetch=2, grid=(b, n_tiles), in_specs=in_specs, out_specs=out_specs)
    return pl.pallas_call(
        functools.partial(_scatter_kernel, cap=cap, win=win, n_tiles=n_tiles, next_norm=next_norm is not None),
        grid_spec=grid_spec,
        out_shape=out_shape,
        compiler_params=_params(("parallel", "arbitrary")),
        name="ec_scatter",
    )(*args)


def _mod_chunks(mod_rows):
    b = mod_rows.shape[0]
    return [mod_rows[:, i * D_MODEL:(i + 1) * D_MODEL].reshape(b, 1, D_MODEL) for i in range(N_MOD)]


def _token_mixer(h, zmix, att, lw, dft):
    b, n, d = h.shape
    y_conv, y_pool = _stencil(zmix, lw["conv_w"], lw["pool_w"], lw["pool_scale"])
    y_f = _fourier(zmix, *dft)
    branches = [y.reshape(b * n, MIX_W) for y in (y_conv, att, y_f, y_pool)]
    merged = _merge(h.reshape(b * n, d), branches, lw["w_in"], lw["w_branch"], lw["layer"])
    return merged.reshape(b, n, d)


def _route(h2, aff, affn):
    b, n, d = h2.shape
    cap = EC_CAPACITY * n // N_EXPERTS
    slot, slott = _select(aff, cap)
    per_tile = jnp.sum((slot >= 0).reshape(b, N_EXPERTS, n // SEL_TILE, SEL_TILE), axis=-1, dtype=I32)
    cum = jnp.concatenate([jnp.zeros((b, N_EXPERTS, 1), I32), jnp.cumsum(per_tile, axis=-1)], axis=-1)
    win = min(EC_WINDOW, cap)
    span = cum[..., 1:] - (cum[..., :-1] // ROW_ALIGN) * ROW_ALIGN
    rounds = jnp.max((span + win - 1) // win, axis=1).reshape(-1)
    cum = cum.reshape(-1)
    xs, gate = _gather(cum, rounds, slot, h2, affn, cap)
    tables = (cum, rounds, slott, cap)
    return tables, (xs.reshape(N_EXPERTS, b * cap, d), gate.reshape(N_EXPERTS, b * cap, LANES))


def _combine(tables, ys, x1, g2, next_norm=None):
    cum, rounds, slott, cap = tables
    b, _, d = x1.shape
    return _scatter(cum, rounds, slott, ys.reshape(N_EXPERTS, b, cap, d), x1, g2, cap, next_norm)


def kernel(x, c, ctx, c_ctx, w_ada, b_ada, norm1_w, norm2_w, w_in, conv_w, q_norm_w, k_norm_w,
           na_rpb, pool_w, pool_scale, w_branch, w_out, w_router, w_exp_gate, w_exp_up, w_exp_down):
    bsz, n, d = x.shape
    l_ctx = ctx.shape[1]
    depth = w_ada.shape[0]
    n_rows = n // GRID_W

    c16 = jnp.zeros((16, d), F32).at[:bsz].set(c).at[bsz].set(c_ctx)
    mod_all = _ada(c16, w_ada, b_ada)

    hd = np.arange(MIX_W) // HEAD_DIM
    bd = jnp.asarray(hd[:, None] == hd[None, :], dtype=BF16)
    dft_lat = _dft_mats(n, 1.0, n // 2) + _dft_mats(FOURIER_GC, (n * FOURIER_GC) ** -0.5)
    dft_ctx = _dft_mats(l_ctx, 1.0, l_ctx // 2) + _dft_mats(FOURIER_GC, (l_ctx * FOURIER_GC) ** -0.5)

    mods = [_mod_chunks(mod_all[l, :bsz]) for l in range(depth)]
    h = _prenorm(x, norm1_w[0], mods[0][0], mods[0][1])
    for l in range(depth):
        last = l == depth - 1
        lw = {
            "w_in": w_in, "conv_w": conv_w[l], "pool_w": pool_w[l],
            "pool_scale": pool_scale[l], "w_branch": w_branch,
            "w_gate": w_exp_gate, "w_up": w_exp_up, "w_down": w_exp_down, "layer": l,
        }
        w_out_bf = w_out[l].astype(BF16)
        wr_pad = jnp.zeros((d, LANES), F32).at[:, :N_EXPERTS].set(w_router[l])
        wr_hi = wr_pad.astype(BF16)
        wr_cat = jnp.concatenate([wr_hi, (wr_pad - wr_hi.astype(F32)).astype(BF16)], axis=1)
        q_w = jnp.tile(q_norm_w[l], N_HEADS).reshape(1, MIX_W)
        k_w = jnp.tile(k_norm_w[l], N_HEADS).reshape(1, MIX_W)
        tb = _na_bias_table(na_rpb[l], n_rows)
        _, _, g1, sh2, sc2, g2 = mods[l]
        mc = _mod_chunks(jnp.broadcast_to(mod_all[l, bsz:bsz + 1], (bsz, N_MOD * d)))

        hc = _prenorm(ctx, norm1_w[l], mc[0], mc[1])
        zc_mix = _in_proj(hc.reshape(bsz * l_ctx, d), w_in, l, MIX_COLS).reshape(bsz, l_ctx, MIX_COLS)

        zmix = _in_proj(h.reshape(bsz * n, d), w_in, l, MIX_COLS).reshape(bsz, n, MIX_COLS)
        att = _neighbourhood_attention(zmix, zc_mix, q_w, k_w, bd, tb)
        merged = _token_mixer(h, zmix, att, lw, dft_lat)
        x1, h2, aff, affn = _out_proj(merged, w_out_bf, x, g1, norm2_w[l], sh2, sc2, wr_cat)
        tables, routed = _route(h2, aff, affn)
        expert_w = (w_exp_gate, w_exp_up, w_exp_down, l)
        if last:
            x = _combine(tables, _experts(*routed, *expert_w), x1, g2)
        else:
            att_c = _context_attention(zc_mix, q_w, k_w, bd)
            merged_c = _token_mixer(hc, zc_mix, att_c, lw, dft_ctx)
            c1, hc2, aff_c, affn_c = _out_proj(merged_c, w_out_bf, ctx, mc[2], norm2_w[l], mc[3], mc[4], wr_cat)
            tables_c, routed_c = _route(hc2, aff_c, affn_c)
            ys, ys_c = _experts(*routed, *expert_w, extra=routed_c)
            x, h = _combine(tables, ys, x1, g2, (norm1_w[l + 1], mods[l + 1][0], mods[l + 1][1]))
            ctx = _combine(tables_c, ys_c, c1, mc[5])
    return x
```

```python
import functools

import numpy as np
import jax
import jax.numpy as jnp
from jax import lax
from jax.experimental import pallas as pl
from jax.experimental.pallas import tpu as pltpu

F32 = jnp.float32
BF16 = jnp.bfloat16
I32 = jnp.int32
HIGHEST = lax.Precision.HIGHEST

D_MODEL = 2048
GRID_W = 64
MIX_W = D_MODEL // 4
N_HEADS = 8
HEAD_DIM = MIX_W // N_HEADS
WIN_H = 8
WIN_W = 16
ATT_SCALE = HEAD_DIM ** -0.5
POOL_WINDOWS = (2, 4, 8, 16)
POOL_GC = MIX_W // len(POOL_WINDOWS)
FOURIER_GC = MIX_W // 4
N_EXPERTS = 16
EC_CAPACITY = 2
N_MOD = 6
EPS = 1e-6
NEG_INF = -1e30
MIX_COLS = 8 * MIX_W
LANES = 128
HALO = 16
SEL_TILE = 256
ROW_ALIGN = 16
EC_WINDOW = 64
ROWS_PER_PASS = 2
SCATTER_CHUNKS = 4
GATHER_CHUNKS = 4
VMEM_LIMIT = 56 * 1024 * 1024

COL_XA, COL_GB, COL_GC, COL_Q, COL_K, COL_V, COL_F, COL_P = range(8)


def _params(sem, vmem=VMEM_LIMIT):
    return pltpu.CompilerParams(dimension_semantics=sem, vmem_limit_bytes=vmem)


def _dot(a, b):
    return jnp.dot(a, b, preferred_element_type=F32)


def _dot_nt(a, b):
    return lax.dot_general(a, b, (((1,), (1,)), ((), ())), preferred_element_type=F32)


def _ada_kernel(c_ref, w_ref, b_ref, o_ref):
    c = c_ref[...]
    s = c * jax.nn.sigmoid(c)
    o_ref[0] = jnp.dot(s, w_ref[0], precision=HIGHEST, preferred_element_type=F32) + b_ref[0]


def _ada(c16, w_ada, b_ada):
    depth, d, ncol = w_ada.shape
    tn = 1024
    return pl.pallas_call(
        _ada_kernel,
        grid=(depth, ncol // tn),
        in_specs=[
            pl.BlockSpec((16, d), lambda l, j: (0, 0)),
            pl.BlockSpec((1, d, tn), lambda l, j: (l, 0, j)),
            pl.BlockSpec((1, 1, tn), lambda l, j: (l, 0, j)),
        ],
        out_specs=pl.BlockSpec((1, 16, tn), lambda l, j: (l, 0, j)),
        out_shape=jax.ShapeDtypeStruct((depth, 16, ncol), F32),
        compiler_params=_params(("parallel", "parallel")),
        name="ada",
    )(c16, w_ada, b_ada.reshape(depth, 1, ncol))


def _prenorm_math(x, w, shift, scale):
    ms = jnp.mean(x * x, axis=-1, keepdims=True)
    y = x * lax.rsqrt(ms + EPS) * w
    return y * (1.0 + scale) + shift


def _prenorm_kernel(x_ref, w_ref, sh_ref, sc_ref, o_ref):
    o_ref[0] = _prenorm_math(x_ref[0], w_ref[...], sh_ref[0], sc_ref[0]).astype(BF16)


def _prenorm(x, w, shift, scale):
    b, n, d = x.shape
    tm = min(512, n)
    return pl.pallas_call(
        _prenorm_kernel,
        grid=(b, n // tm),
        in_specs=[
            pl.BlockSpec((1, tm, d), lambda i, j: (i, j, 0)),
            pl.BlockSpec((1, d), lambda i, j: (0, 0)),
            pl.BlockSpec((1, 1, d), lambda i, j: (i, 0, 0)),
            pl.BlockSpec((1, 1, d), lambda i, j: (i, 0, 0)),
        ],
        out_specs=pl.BlockSpec((1, tm, d), lambda i, j: (i, j, 0)),
        out_shape=jax.ShapeDtypeStruct((b, n, d), BF16),
        compiler_params=_params(("parallel", "parallel")),
        name="prenorm",
    )(x, w.reshape(1, d), shift, scale)


def _mm_kernel(a_ref, w_ref, o_ref, w_bf):
    @pl.when(pl.program_id(1) == 0)
    def _():
        w_bf[...] = w_ref[0].astype(BF16)

    o_ref[...] = _dot(a_ref[...], w_bf[...]).astype(o_ref.dtype)


def _in_proj(h2d, w, layer, ncols):
    m, k = h2d.shape
    tm = min(1024, m)
    tn = 1024
    return pl.pallas_call(
        _mm_kernel,
        grid=(ncols // tn, m // tm),
        in_specs=[
            pl.BlockSpec((tm, k), lambda j, i: (i, 0)),
            pl.BlockSpec((1, k, tn), lambda j, i: (layer, 0, j)),
        ],
        out_specs=pl.BlockSpec((tm, tn), lambda j, i: (i, j)),
        out_shape=jax.ShapeDtypeStruct((m, ncols), BF16),
        scratch_shapes=[pltpu.VMEM((k, tn), BF16)],
        compiler_params=_params(("parallel", "arbitrary")),
        name="in_proj",
    )(h2d, w)


def _stencil_kernel(xa_ref, gb_ref, gc_ref, zp_ref, xa_p, xa_n, gc_p, gc_n, zp_p, zp_n,
                    cw_ref, pw_ref, ps_ref, oc_ref, op_ref, *, tm, n_seq):
    i = pl.program_id(1)
    nt = pl.num_programs(1)
    rows = tm + 2 * HALO

    def ext(main_ref, prev_ref, next_ref):
        main = main_ref[0].astype(F32)
        prev = jnp.where(i > 0, prev_ref[0].astype(F32), 0.0)
        nxt = jnp.where(i < nt - 1, next_ref[0].astype(F32), 0.0)
        return jnp.concatenate([prev, main, nxt], axis=0)

    def shifted(a, d):
        if d == 0:
            return a[HALO:HALO + tm]
        return pltpu.roll(a, (-d) % rows, axis=0)[HALO:HALO + tm]

    u = ext(gc_ref, gc_p, gc_n) * ext(xa_ref, xa_p, xa_n)
    cw = cw_ref[...]
    y = cw[0:1] * shifted(u, -1) + cw[1:2] * shifted(u, 0) + cw[2:3] * shifted(u, 1)
    oc_ref[0] = (gb_ref[0].astype(F32) * y).astype(BF16)

    t = (i * tm + lax.broadcasted_iota(I32, (tm, 1), 0)).astype(F32)
    zp = ext(zp_ref, zp_p, zp_n)
    for g, win in enumerate(POOL_WINDOWS):
        ug = zp[:, g * POOL_GC:(g + 1) * POOL_GC]
        acc = ug + pltpu.roll(ug, 1, axis=0)
        half = 1
        while 2 * half < win:
            acc = pltpu.roll(acc, rows - half, axis=0) + pltpu.roll(acc, half, axis=0)
            half *= 2
        wsum = acc[HALO:HALO + tm]
        lo = jnp.maximum(t - (win // 2), 0.0)
        hi = jnp.minimum(t + (win - win // 2 - 1), float(n_seq - 1))
        pooled = wsum / (hi - lo + 1.0) - ug[HALO:HALO + tm]
        yg = _dot(pooled.astype(BF16), pw_ref[g])
        op_ref[0, :, g * POOL_GC:(g + 1) * POOL_GC] = (
            yg * ps_ref[:, g * POOL_GC:(g + 1) * POOL_GC]).astype(BF16)


def _stencil(zmix, conv_w, pool_w, pool_scale):
    b, n, _ = zmix.shape
    tm = min(512, n)
    hb = tm // HALO
    last = n // HALO - 1

    def main(col):
        return pl.BlockSpec((1, tm, MIX_W), lambda bi, i: (bi, i, col))

    def prev(col):
        return pl.BlockSpec((1, HALO, MIX_W), lambda bi, i: (bi, jnp.maximum(i * hb - 1, 0), col))

    def nxt(col):
        return pl.BlockSpec((1, HALO, MIX_W), lambda bi, i: (bi, jnp.minimum((i + 1) * hb, last), col))

    out_spec = pl.BlockSpec((1, tm, MIX_W), lambda bi, i: (bi, i, 0))
    return pl.pallas_call(
        functools.partial(_stencil_kernel, tm=tm, n_seq=n),
        grid=(b, n // tm),
        in_specs=[main(COL_XA), main(COL_GB), main(COL_GC), main(COL_P),
                  prev(COL_XA), nxt(COL_XA), prev(COL_GC), nxt(COL_GC), prev(COL_P), nxt(COL_P),
                  pl.BlockSpec((3, MIX_W), lambda bi, i: (0, 0)),
                  pl.BlockSpec((len(POOL_WINDOWS), POOL_GC, POOL_GC), lambda bi, i: (0, 0, 0)),
                  pl.BlockSpec((1, MIX_W), lambda bi, i: (0, 0))],
        out_specs=[out_spec, out_spec],
        out_shape=[jax.ShapeDtypeStruct((b, n, MIX_W), BF16)] * 2,
        compiler_params=_params(("parallel", "parallel")),
        name="stencil",
    )(zmix, zmix, zmix, zmix, zmix, zmix, zmix, zmix, zmix, zmix,
      conv_w, pool_w.astype(BF16), pool_scale.reshape(1, MIX_W))


def _fourier_kernel(cn_ref, sn_ref, u_ref, cc_ref, sc_ref, o_ref, even, odd):
    n = u_ref.shape[1]
    half, tk = n // 2, cn_ref.shape[0]
    rb = min(256, half)

    @pl.when(pl.program_id(1) == 0)
    def _():
        i = lax.broadcasted_iota(I32, (rb, rb), 0)
        j = lax.broadcasted_iota(I32, (rb, rb), 1)
        flip = jnp.where(i + j == rb, 1.0, 0.0).astype(BF16)
        row = lax.broadcasted_iota(I32, (rb, 1), 0)
        for a in range(half // rb):
            src = n - (a + 1) * rb
            rev = _dot(flip, u_ref[0, src:src + rb, :])
            first = (n - a * rb) % n
            rev = jnp.where(row == 0, u_ref[0, first:first + 1, :].astype(F32), rev)
            ua = u_ref[0, a * rb:(a + 1) * rb, :].astype(F32)
            fold = ua + rev if a else jnp.where(row == 0, ua, ua + rev)
            even[a * rb:(a + 1) * rb, :] = fold.astype(BF16)
            odd[a * rb:(a + 1) * rb, :] = (ua - rev).astype(BF16)

    k = pl.program_id(1) * tk + lax.broadcasted_iota(I32, (tk, 1), 0)
    sign = (1 - 2 * jnp.bitwise_and(k, 1)).astype(F32)
    p = (_dot(cn_ref[...], even[...]) + sign * u_ref[0, half:half + 1, :].astype(F32)).astype(BF16)
    q = _dot(sn_ref[...], odd[...]).astype(BF16)
    for g in range(MIX_W // FOURIER_GC):
        sl = slice(g * FOURIER_GC, (g + 1) * FOURIER_GC)
        o_ref[0, :, sl] = (_dot(p[:, sl], cc_ref[...]) - _dot(q[:, sl], sc_ref[...])).astype(BF16)


def _dft_mats(n, scale, ncols=None):
    ncols = n if ncols is None else ncols
    k = jnp.arange(n, dtype=I32)[:, None]

    def table(count, period):
        r = (k * jnp.arange(count, dtype=I32)[None, :]) % period
        ang = r.astype(F32) * np.float32(2.0 * np.pi / period)
        return jnp.cos(ang), jnp.sin(ang)

    inner = 64
    if n <= inner or n % inner or ncols % inner:
        c, s = table(ncols, n)
    else:
        ca, sa = table(ncols // inner, n // inner)
        cb, sb = table(inner, n)
        c = (ca[:, :, None] * cb[:, None, :] - sa[:, :, None] * sb[:, None, :]).reshape(n, ncols)
        s = (sa[:, :, None] * cb[:, None, :] + ca[:, :, None] * sb[:, None, :]).reshape(n, ncols)
    return (c * scale).astype(BF16), (s * scale).astype(BF16)


def _fourier(zmix, cn, sn, cc, sc):
    b, n, _ = zmix.shape
    half = n // 2
    tk = min(512, n)
    return pl.pallas_call(
        _fourier_kernel,
        grid=(b, n // tk),
        in_specs=[
            pl.BlockSpec((tk, half), lambda bi, k: (k, 0)),
            pl.BlockSpec((tk, half), lambda bi, k: (k, 0)),
            pl.BlockSpec((1, n, MIX_W), lambda bi, k: (bi, 0, COL_F)),
            pl.BlockSpec((FOURIER_GC, FOURIER_GC), lambda bi, k: (0, 0)),
            pl.BlockSpec((FOURIER_GC, FOURIER_GC), lambda bi, k: (0, 0)),
        ],
        out_specs=pl.BlockSpec((1, tk, MIX_W), lambda bi, k: (bi, k, 0)),
        out_shape=jax.ShapeDtypeStruct((b, n, MIX_W), BF16),
        scratch_shapes=[pltpu.VMEM((half, MIX_W), BF16), pltpu.VMEM((half, MIX_W), BF16)],
        compiler_params=_params(("parallel", "arbitrary")),
        name="fourier",
    )(cn, sn, zmix, cc, sc)


def _head_rms(x, w, bd):
    x2 = x * x
    hi = x2.astype(BF16)
    lo = (x2 - hi.astype(F32)).astype(BF16)
    ms = (_dot(hi, bd) + _dot(lo, bd)) * (1.0 / HEAD_DIM)
    return x * lax.rsqrt(ms + EPS) * w


def _pair_stack(q2):
    lane = lax.broadcasted_iota(I32, q2.shape, 1)
    zero = jnp.zeros_like(q2)
    return jnp.concatenate([jnp.where(lane < HEAD_DIM, q2, zero),
                            jnp.where(lane >= HEAD_DIM, q2, zero)], axis=0)


def _pair_unstack(o2, m):
    lane = lax.broadcasted_iota(I32, (m, LANES), 1)
    return jnp.where(lane < HEAD_DIM, o2[:m], o2[m:])


def _na_kernel(q_ref, k_ref, v_ref, kc_ref, vc_ref, qw_ref, kw_ref, bd_ref, tb_ref, o_ref,
               kn_scr, kcn_scr, qn_scr, *, rt, n_rows):
    rb = pl.program_id(1)
    n_seq = n_rows * GRID_W
    chunk = 512

    @pl.when(rb == 0)
    def _():
        def body(c, carry):
            sl = pl.ds(pl.multiple_of(c * chunk, chunk), chunk)
            kn_scr[sl, :] = _head_rms(k_ref[0, sl, :].astype(F32), kw_ref[...], bd_ref[...]).astype(BF16)
            return carry
        lax.fori_loop(0, n_seq // chunk, body, 0)
        kcn_scr[...] = _head_rms(kc_ref[0].astype(F32), kw_ref[...], bd_ref[...]).astype(BF16)

    qn_scr[...] = (_head_rms(q_ref[0].astype(F32), qw_ref[...], bd_ref[...]) * ATT_SCALE).astype(BF16)

    kh = min(WIN_H, n_rows)
    nloc = kh * GRID_W

    def row_body(jj, carry):
        units = []
        for i in range(ROWS_PER_PASS):
            j = jj * ROWS_PER_PASS + i
            r = rb * rt + j
            rs = jnp.clip(r - kh // 2, 0, n_rows - kh)
            q_rows = pl.ds(pl.multiple_of(j * GRID_W, GRID_W), GRID_W)
            k_rows = pl.ds(pl.multiple_of(rs * GRID_W, GRID_W), nloc)
            for p in range(N_HEADS // 2):
                units.append((q_rows, k_rows, rs - r + (WIN_H - 1), p, slice(p * LANES, (p + 1) * LANES)))
        qs = [_pair_stack(qn_scr[q_rows, cols]) for q_rows, _, _, _, cols in units]
        s_loc = [_dot_nt(q, kn_scr[k_rows, cols]) for q, (_, k_rows, _, _, cols) in zip(qs, units)]
        s_ctx = [_dot_nt(q, kcn_scr[:, cols]) for q, (_, _, _, _, cols) in zip(qs, units)]
        s_loc = [s + tb_ref[dr0, p] for s, (_, _, dr0, p, _) in zip(s_loc, units)]
        m = [jnp.maximum(jnp.max(a, axis=1, keepdims=True), jnp.max(c, axis=1, keepdims=True))
             for a, c in zip(s_loc, s_ctx)]
        p_loc = [jnp.exp(a - mx) for a, mx in zip(s_loc, m)]
        p_ctx = [jnp.exp(c - mx) for c, mx in zip(s_ctx, m)]
        denom = [jnp.sum(a, axis=1, keepdims=True) + jnp.sum(c, axis=1, keepdims=True) for a, c in zip(p_loc, p_ctx)]
        o2 = [_dot(a.astype(BF16), v_ref[0, k_rows, cols]) + _dot(c.astype(BF16), vc_ref[0, :, cols])
              for a, c, (_, k_rows, _, _, cols) in zip(p_loc, p_ctx, units)]
        for o, d, (q_rows, _, _, _, cols) in zip(o2, denom, units):
            o_ref[0, q_rows, cols] = _pair_unstack(o / d, GRID_W).astype(BF16)
        return carry

    lax.fori_loop(0, rt // ROWS_PER_PASS, row_body, 0)


def _na_bias_table(rpb, n_rows):
    kh = min(WIN_H, n_rows)
    c = np.arange(GRID_W)
    cs = np.clip(c - WIN_W // 2, 0, GRID_W - WIN_W)
    kc = np.arange(GRID_W)
    ok = (kc[None, :] >= cs[:, None]) & (kc[None, :] < cs[:, None] + WIN_W)
    edge = GRID_W - WIN_W
    padded = jnp.pad(rpb.astype(F32), ((0, 0), (0, 0), (edge, edge)))
    toeplitz = jnp.stack([padded[:, :, GRID_W - 1 - q:2 * GRID_W - 1 - q] for q in range(GRID_W)], axis=2)
    masked = jnp.where(ok[None, None], toeplitz, NEG_INF)
    t = jnp.stack([masked[:, d0:d0 + kh] for d0 in range(WIN_H)], axis=0)
    t = jnp.transpose(t, (0, 1, 3, 2, 4))
    return t.reshape(WIN_H, N_HEADS // 2, 2 * GRID_W, kh * GRID_W)


def _neighbourhood_attention(zmix, zc_mix, q_w, k_w, bd, tb):
    b, n, _ = zmix.shape
    l = zc_mix.shape[1]
    n_rows = n // GRID_W
    rt = 8
    nloc = min(WIN_H, n_rows) * GRID_W
    return pl.pallas_call(
        functools.partial(_na_kernel, rt=rt, n_rows=n_rows),
        grid=(b, n_rows // rt),
        in_specs=[
            pl.BlockSpec((1, rt * GRID_W, MIX_W), lambda bi, r: (bi, r, COL_Q)),
            pl.BlockSpec((1, n, MIX_W), lambda bi, r: (bi, 0, COL_K)),
            pl.BlockSpec((1, n, MIX_W), lambda bi, r: (bi, 0, COL_V)),
            pl.BlockSpec((1, l, MIX_W), lambda bi, r: (bi, 0, COL_K)),
            pl.BlockSpec((1, l, MIX_W), lambda bi, r: (bi, 0, COL_V)),
            pl.BlockSpec((1, MIX_W), lambda bi, r: (0, 0)),
            pl.BlockSpec((1, MIX_W), lambda bi, r: (0, 0)),
            pl.BlockSpec((MIX_W, MIX_W), lambda bi, r: (0, 0)),
            pl.BlockSpec((WIN_H, N_HEADS // 2, 2 * GRID_W, nloc), lambda bi, r: (0, 0, 0, 0),
                         pipeline_mode=pl.Buffered(1)),
        ],
        out_specs=pl.BlockSpec((1, rt * GRID_W, MIX_W), lambda bi, r: (bi, r, 0)),
        out_shape=jax.ShapeDtypeStruct((b, n, MIX_W), BF16),
        scratch_shapes=[pltpu.VMEM((n, MIX_W), BF16), pltpu.VMEM((l, MIX_W), BF16),
                        pltpu.VMEM((rt * GRID_W, MIX_W), BF16)],
        compiler_params=_params(("parallel", "arbitrary")),
        name="nbr_attention",
    )(zmix, zmix, zmix, zc_mix, zc_mix, q_w, k_w, bd, tb)


def _ctx_attn_kernel(q_ref, k_ref, v_ref, qw_ref, kw_ref, bd_ref, o_ref):
    l = q_ref.shape[1]
    qn = (_head_rms(q_ref[0].astype(F32), qw_ref[...], bd_ref[...]) * ATT_SCALE).astype(BF16)
    kn = _head_rms(k_ref[0].astype(F32), kw_ref[...], bd_ref[...]).astype(BF16)
    for p in range(N_HEADS // 2):
        cols = slice(p * LANES, (p + 1) * LANES)
        qs = _pair_stack(qn[:, cols])
        s = _dot_nt(qs, kn[:, cols])
        m = jnp.max(s, axis=1, keepdims=True)
        e = jnp.exp(s - m)
        o2 = _dot(e.astype(BF16), v_ref[0, :, cols]) / jnp.sum(e, axis=1, keepdims=True)
        o_ref[0, :, cols] = _pair_unstack(o2, l).astype(BF16)


def _context_attention(zc_mix, q_w, k_w, bd):
    b, l, _ = zc_mix.shape
    vec = pl.BlockSpec((1, MIX_W), lambda bi: (0, 0))
    return pl.pallas_call(
        _ctx_attn_kernel,
        grid=(b,),
        in_specs=[
            pl.BlockSpec((1, l, MIX_W), lambda bi: (bi, 0, COL_Q)),
            pl.BlockSpec((1, l, MIX_W), lambda bi: (bi, 0, COL_K)),
            pl.BlockSpec((1, l, MIX_W), lambda bi: (bi, 0, COL_V)),
            vec, vec,
            pl.BlockSpec((MIX_W, MIX_W), lambda bi: (0, 0)),
        ],
        out_specs=pl.BlockSpec((1, l, MIX_W), lambda bi: (bi, 0, 0)),
        out_shape=jax.ShapeDtypeStruct((b, l, MIX_W), BF16),
        compiler_params=_params(("parallel",)),
        name="ctx_attention",
    )(zc_mix, zc_mix, zc_mix, q_w, k_w, bd)


def _merge_kernel(h_ref, y0, y1, y2, y3, g0, g1, g2, g3, wb_ref, o_ref, wg_bf, wb_bf):
    @pl.when(pl.program_id(1) == 0)
    def _():
        for i, g in enumerate((g0, g1, g2, g3)):
            wg_bf[i] = g[0].astype(BF16)
        wb_bf[...] = wb_ref[0].astype(BF16)

    h = h_ref[...]
    acc = None
    for i, y in enumerate((y0, y1, y2, y3)):
        gate = jax.nn.sigmoid(_dot(h, wg_bf[i]))
        term = gate * _dot(y[...], wb_bf[i])
        acc = term if acc is None else acc + term
    o_ref[...] = acc.astype(BF16)


def _merge(h2d, branches, w_in, w_branch, layer):
    m, d = h2d.shape
    tm = min(1024, m)
    tn = 256
    gate0 = MIX_COLS // tn

    def gspec(i):
        return pl.BlockSpec((1, d, tn), lambda n, r: (layer, 0, gate0 + i * (d // tn) + n))

    yspec = pl.BlockSpec((tm, MIX_W), lambda n, r: (r, 0))
    return pl.pallas_call(
        _merge_kernel,
        grid=(d // tn, m // tm),
        in_specs=[pl.BlockSpec((tm, d), lambda n, r: (r, 0)), yspec, yspec, yspec, yspec,
                  gspec(0), gspec(1), gspec(2), gspec(3),
                  pl.BlockSpec((1, 4, MIX_W, tn), lambda n, r: (layer, 0, 0, n))],
        out_specs=pl.BlockSpec((tm, tn), lambda n, r: (r, n)),
        out_shape=jax.ShapeDtypeStruct((m, d), BF16),
        scratch_shapes=[pltpu.VMEM((4, d, tn), BF16), pltpu.VMEM((4, MIX_W, tn), BF16)],
        compiler_params=_params(("parallel", "arbitrary")),
        name="merge",
    )(h2d, *branches, w_in, w_in, w_in, w_in, w_branch)


def _out_proj_kernel(m_ref, w_ref, x_ref, g_ref, nw_ref, sh_ref, sc_ref, wrc_ref,
                     x1_ref, h2_ref, aff_ref, affn_ref):
    x1 = x_ref[0] + g_ref[0] * _dot(m_ref[0], w_ref[...])
    x1_ref[0] = x1
    h2 = _prenorm_math(x1, nw_ref[...], sh_ref[0], sc_ref[0])
    h_hi = h2.astype(BF16)
    h2_ref[0] = h_hi
    h_lo = (h2 - h_hi.astype(F32)).astype(BF16)
    both = _dot(h_hi, wrc_ref[...])
    logits = both[:, :LANES] + (both[:, LANES:] + _dot(h_lo, wrc_ref[:, :LANES]))
    lane = lax.broadcasted_iota(I32, logits.shape, 1)
    logits = jnp.where(lane < N_EXPERTS, logits, NEG_INF)
    e = jnp.exp(logits - jnp.max(logits, axis=1, keepdims=True))
    aff = e / jnp.sum(e, axis=1, keepdims=True)
    aff_ref[0] = aff.T[:N_EXPERTS]
    hi = aff.astype(BF16).astype(F32)
    r1 = aff - hi
    mid = r1.astype(BF16).astype(F32)
    lo = (r1 - mid).astype(BF16).astype(F32)
    packed = hi + pltpu.roll(mid, N_EXPERTS, axis=1) + pltpu.roll(lo, 2 * N_EXPERTS, axis=1)
    affn_ref[0] = packed.astype(BF16)


def _out_proj(merged, w_out_bf, x, gate, norm_w, shift, scale, wr_cat):
    b, n, d = x.shape
    tm = min(512, n)
    vec = pl.BlockSpec((1, 1, d), lambda i, j: (i, 0, 0))
    tile = pl.BlockSpec((1, tm, d), lambda i, j: (i, j, 0))
    wr_spec = pl.BlockSpec((d, 2 * LANES), lambda i, j: (0, 0))
    return pl.pallas_call(
        _out_proj_kernel,
        grid=(b, n // tm),
        in_specs=[tile, pl.BlockSpec((d, d), lambda i, j: (0, 0), pipeline_mode=pl.Buffered(1)), tile, vec,
                  pl.BlockSpec((1, d), lambda i, j: (0, 0)), vec, vec, wr_spec],
        out_specs=[tile, tile,
                   pl.BlockSpec((1, N_EXPERTS, tm), lambda i, j: (i, 0, j)),
                   pl.BlockSpec((1, tm, LANES), lambda i, j: (i, j, 0))],
        out_shape=[jax.ShapeDtypeStruct((b, n, d), F32), jax.ShapeDtypeStruct((b, n, d), BF16),
                   jax.ShapeDtypeStruct((b, N_EXPERTS, n), F32), jax.ShapeDtypeStruct((b, n, LANES), BF16)],
        compiler_params=_params(("parallel", "parallel")),
        name="out_proj",
    )(merged, w_out_bf, x, gate, norm_w.reshape(1, d), shift, scale, wr_cat)


def _lane_cumsum(mask_f, ut):
    rows, n = mask_f.shape
    carry = jnp.zeros((rows, 1), F32)
    parts = []
    for j in range(n // LANES):
        c = _dot(mask_f[:, j * LANES:(j + 1) * LANES].astype(BF16), ut) + carry
        parts.append(c)
        carry = c[:, LANES - 1:LANES]
    return jnp.concatenate(parts, axis=1)


def _select_kernel(aff_ref, slot_ref, slott_ref, *, cap):
    aff = aff_ref[0]
    e, n = aff.shape
    bits = lax.bitcast_convert_type(aff, I32)

    def body(_, carry):
        lo, hi = carry
        mid = lo + ((hi - lo) >> 1)
        cnt = jnp.sum(jnp.where(bits >= mid, 1.0, 0.0), axis=1, keepdims=True)
        ge = cnt >= float(cap)
        return jnp.where(ge, mid, lo), jnp.where(ge, hi, mid)

    lo, _ = lax.fori_loop(0, 31, body, (jnp.zeros((e, 1), I32), jnp.full((e, 1), 0x7F800000, I32)))
    ri = lax.broadcasted_iota(I32, (LANES, LANES), 0)
    ci = lax.broadcasted_iota(I32, (LANES, LANES), 1)
    ut = jnp.where(ri <= ci, 1.0, 0.0).astype(BF16)
    gt = bits > lo
    eq = bits == lo
    need = float(cap) - jnp.sum(jnp.where(gt, 1.0, 0.0), axis=1, keepdims=True)
    ceq = _lane_cumsum(jnp.where(eq, 1.0, 0.0), ut)
    sel = gt | (eq & (ceq <= need))
    csel = _lane_cumsum(jnp.where(sel, 1.0, 0.0), ut)
    slot = jnp.where(sel, csel - 1.0, -1.0)
    slot_ref[0] = slot.astype(I32)
    padded = jnp.concatenate([slot, jnp.full((LANES - e, n), -1.0, F32)], axis=0)
    slott_ref[0] = padded.T


def _select(aff, cap):
    b, e, n = aff.shape
    return pl.pallas_call(
        functools.partial(_select_kernel, cap=cap),
        grid=(b,),
        in_specs=[pl.BlockSpec((1, e, n), lambda i: (i, 0, 0))],
        out_specs=[pl.BlockSpec((1, e, n), lambda i: (i, 0, 0)),
                   pl.BlockSpec((1, n, LANES), lambda i: (i, 0, 0))],
        out_shape=[jax.ShapeDtypeStruct((b, e, n), I32), jax.ShapeDtypeStruct((b, n, LANES), F32)],
        compiler_params=_params(("parallel",)),
        name="select",
    )(aff)


def _window(cum_ref, b, e, j, r, n_tiles, cap, win):
    c_lo = cum_ref[(b * N_EXPERTS + e) * (n_tiles + 1) + j]
    first = (c_lo // ROW_ALIGN) * ROW_ALIGN + r * win
    start = pl.multiple_of(jnp.minimum(first, cap - win), ROW_ALIGN)
    return first, start


def _gather_kernel(cum_ref, rnd_ref, slot_ref, h_ref, affn_ref, xs_ref, gate_ref, gacc, *, cap, win, n_tiles):
    b = pl.program_id(0)
    dq = pl.program_id(1)
    xs_ref[...] = jnp.zeros_like(xs_ref)

    @pl.when(dq == 0)
    def _():
        gacc[...] = jnp.zeros_like(gacc)

    rows = lax.broadcasted_iota(I32, (win, SEL_TILE), 0)

    def tile_body(j, carry):
        toks = pl.ds(pl.multiple_of(j * SEL_TILE, SEL_TILE), SEL_TILE)
        slots = slot_ref[0, j]
        h_tile = h_ref[0, toks, :]

        def round_body(r, carry2):
            starts, pieces = [], []
            for e in range(N_EXPERTS):
                first, start = _window(cum_ref, b, e, j, r, n_tiles, cap, win)
                s_row = slots[e:e + 1, :]
                hit = (rows + start == s_row) & (s_row >= first)
                pieces.append(jnp.where(hit, 1.0, 0.0).astype(BF16))
                starts.append(start)
            ge = N_EXPERTS // GATHER_CHUNKS
            onehots = [jnp.concatenate(pieces[lo:lo + ge], axis=0) for lo in range(0, N_EXPERTS, ge)]
            for c, onehot in enumerate(onehots):
                res = _dot(onehot, h_tile)
                for i in range(ge):
                    e = c * ge + i
                    sl = pl.ds(starts[e], win)
                    xs_ref[e, 0, sl, :] = (xs_ref[e, 0, sl, :].astype(F32) + res[i * win:(i + 1) * win]).astype(BF16)

            @pl.when(dq == 0)
            def _():
                for c, onehot in enumerate(onehots):
                    g = _dot(onehot, affn_ref[0, toks, :])
                    for i in range(ge):
                        gacc[c * ge + i, pl.ds(starts[c * ge + i], win), :] += g[i * win:(i + 1) * win]
            return carry2

        lax.fori_loop(0, rnd_ref[b * n_tiles + j], round_body, 0)
        return carry

    lax.fori_loop(0, n_tiles, tile_body, 0)

    @pl.when(dq == 0)
    def _():
        lane = lax.broadcasted_iota(I32, (cap, LANES), 1)
        for e in range(N_EXPERTS):
            mine = (lane == e) | (lane == N_EXPERTS + e) | (lane == 2 * N_EXPERTS + e)
            col = jnp.sum(jnp.where(mine, gacc[e], 0.0), axis=1, keepdims=True)
            gate_ref[e, 0] = jnp.broadcast_to(col, (cap, LANES))


def _gather(cum, rounds, slot, h2, affn, cap):
    b, n, d = h2.shape
    n_tiles = n // SEL_TILE
    win = min(EC_WINDOW, cap)
    dn = 512
    slot_tiles = slot.reshape(b, N_EXPERTS, n_tiles, SEL_TILE).transpose(0, 2, 1, 3)
    grid_spec = pltpu.PrefetchScalarGridSpec(
        num_scalar_prefetch=2,
        grid=(b, d // dn),
        in_specs=[
            pl.BlockSpec((1, n_tiles, N_EXPERTS, SEL_TILE), lambda i, q, c, r: (i, 0, 0, 0)),
            pl.BlockSpec((1, n, dn), lambda i, q, c, r: (i, 0, q)),
            pl.BlockSpec((1, n, LANES), lambda i, q, c, r: (i, 0, 0)),
        ],
        out_specs=[pl.BlockSpec((N_EXPERTS, 1, cap, dn), lambda i, q, c, r: (0, i, 0, q)),
                   pl.BlockSpec((N_EXPERTS, 1, cap, LANES), lambda i, q, c, r: (0, i, 0, 0))],
        scratch_shapes=[pltpu.VMEM((N_EXPERTS, cap, LANES), F32)],
    )
    return pl.pallas_call(
        functools.partial(_gather_kernel, cap=cap, win=win, n_tiles=n_tiles),
        grid_spec=grid_spec,
        out_shape=[jax.ShapeDtypeStruct((N_EXPERTS, b, cap, d), BF16),
                   jax.ShapeDtypeStruct((N_EXPERTS, b, cap, LANES), F32)],
        compiler_params=_params(("parallel", "arbitrary")),
        name="ec_gather",
    )(cum, rounds, slot_tiles, h2, affn)


def _expert_up_kernel(xs_ref, wg_ref, wu_ref, h_ref, wg_bf, wu_bf):
    @pl.when(pl.program_id(2) == 0)
    def _():
        wg_bf[...] = wg_ref[0, 0].astype(BF16)
        wu_bf[...] = wu_ref[0, 0].astype(BF16)

    x = xs_ref[0]
    a = _dot(x, wg_bf[...])
    u = _dot(x, wu_bf[...])
    h_ref[0] = (a * jax.nn.sigmoid(a) * u).astype(BF16)


def _expert_down_kernel(h_ref, wd_ref, gate_ref, ys_ref, wd_bf):
    @pl.when(pl.program_id(2) == 0)
    def _():
        wd_bf[...] = wd_ref[0, 0].astype(BF16)

    y = _dot(h_ref[0], wd_bf[...])
    g = gate_ref[0]
    ys_ref[0] = (y * jnp.concatenate([g] * (y.shape[1] // LANES), axis=1)).astype(BF16)


def _experts(xs, gate, wg, wu, wd, layer):
    e, m, d = xs.shape
    ff = wg.shape[3]
    tm = min(1024, m)
    tf = 512
    tn = 1024
    sem = ("parallel", "parallel", "arbitrary")
    hidden = pl.pallas_call(
        _expert_up_kernel,
        grid=(e, ff // tf, m // tm),
        in_specs=[
            pl.BlockSpec((1, tm, d), lambda ei, f, i: (ei, i, 0)),
            pl.BlockSpec((1, 1, d, tf), lambda ei, f, i: (layer, ei, 0, f)),
            pl.BlockSpec((1, 1, d, tf), lambda ei, f, i: (layer, ei, 0, f)),
        ],
        out_specs=pl.BlockSpec((1, tm, tf), lambda ei, f, i: (ei, i, f)),
        out_shape=jax.ShapeDtypeStruct((e, m, ff), BF16),
        scratch_shapes=[pltpu.VMEM((d, tf), BF16), pltpu.VMEM((d, tf), BF16)],
        compiler_params=_params(sem),
        name="expert_up",
    )(xs, wg, wu)
    return pl.pallas_call(
        _expert_down_kernel,
        grid=(e, d // tn, m // tm),
        in_specs=[
            pl.BlockSpec((1, tm, ff), lambda ei, j, i: (ei, i, 0)),
            pl.BlockSpec((1, 1, ff, tn), lambda ei, j, i: (layer, ei, 0, j)),
            pl.BlockSpec((1, tm, LANES), lambda ei, j, i: (ei, i, 0)),
        ],
        out_specs=pl.BlockSpec((1, tm, tn), lambda ei, j, i: (ei, i, j)),
        out_shape=jax.ShapeDtypeStruct((e, m, d), BF16),
        scratch_shapes=[pltpu.VMEM((ff, tn), BF16)],
        compiler_params=_params(sem),
        name="expert_down",
    )(hidden, wd, gate)


def _scatter_kernel(cum_ref, rnd_ref, slott_ref, ys_ref, x_ref, g_ref, *rest, cap, win, n_tiles, next_norm):
    if next_norm:
        nw_ref, sh_ref, sc_ref, o_ref, hn_ref = rest
    else:
        (o_ref,) = rest
    b = pl.program_id(0)
    j = pl.program_id(1)
    st = slott_ref[0]
    per_block = LANES // win
    lane = lax.broadcasted_iota(I32, (SEL_TILE, LANES), 1)
    lane_f = lane.astype(F32)
    group = lane // win

    def round_body(r, acc):
        blocks, pieces = [], []
        for k in range(N_EXPERTS // per_block):
            target = None
            for i in range(per_block):
                e = k * per_block + i
                first, start = _window(cum_ref, b, e, j, r, n_tiles, cap, win)
                col = st[:, e:e + 1]
                tgt = jnp.where(col >= first.astype(F32), col - (start - i * win).astype(F32), -1.0)
                target = tgt if target is None else jnp.where(group == i, tgt, target)
                pieces.append(ys_ref[e, 0, pl.ds(start, win), :])
            blocks.append(jnp.where(lane_f == target, 1.0, 0.0).astype(BF16))
        nb = max(1, len(blocks) // SCATTER_CHUNKS)
        for lo in range(0, len(blocks), nb):
            onehot = jnp.concatenate(blocks[lo:lo + nb], axis=1)
            stacked = jnp.concatenate(pieces[lo * per_block:(lo + nb) * per_block], axis=0)
            acc = acc + _dot(onehot, stacked)
        return acc

    acc = lax.fori_loop(0, rnd_ref[b * n_tiles + j], round_body, jnp.zeros(o_ref.shape[1:], F32))
    x2 = x_ref[0] + g_ref[0] * acc
    o_ref[0] = x2
    if next_norm:
        hn_ref[0] = _prenorm_math(x2, nw_ref[...], sh_ref[0], sc_ref[0]).astype(BF16)


def _scatter(cum, rounds, slott, ys, x1, gate, cap, next_norm=None):
    b, n, d = x1.shape
    n_tiles = n // SEL_TILE
    win = min(EC_WINDOW, cap)
    tile = pl.BlockSpec((1, SEL_TILE, d), lambda i, t, c, r: (i, t, 0))
    vec = pl.BlockSpec((1, 1, d), lambda i, t, c, r: (i, 0, 0))
    in_specs = [
        pl.BlockSpec((1, SEL_TILE, LANES), lambda i, t, c, r: (i, t, 0)),
        pl.BlockSpec((N_EXPERTS, 1, cap, d), lambda i, t, c, r: (0, i, 0, 0), pipeline_mode=pl.Buffered(1)),
        tile, vec,
    ]
    args = [cum, rounds, slott, ys, x1, gate]
    out_specs, out_shape = tile, jax.ShapeDtypeStruct((b, n, d), F32)
    if next_norm is not None:
        in_specs += [pl.BlockSpec((1, d), lambda i, t, c, r: (0, 0)), vec, vec]
        args += [next_norm[0].reshape(1, d), next_norm[1], next_norm[2]]
        out_specs, out_shape = [tile, tile], [out_shape, jax.ShapeDtypeStruct((b, n, d), BF16)]
    grid_spec = pltpu.PrefetchScalarGridSpec(
        num_scalar_prefetch=2, grid=(b, n_tiles), in_specs=in_specs, out_specs=out_specs)
    return pl.pallas_call(
        functools.partial(_scatter_kernel, cap=cap, win=win, n_tiles=n_tiles, next_norm=next_norm is not None),
        grid_spec=grid_spec,
        out_shape=out_shape,
        compiler_params=_params(("parallel", "arbitrary")),
        name="ec_scatter",
    )(*args)


def _mod_chunks(mod_rows):
    b = mod_rows.shape[0]
    return [mod_rows[:, i * D_MODEL:(i + 1) * D_MODEL].reshape(b, 1, D_MODEL) for i in range(N_MOD)]


def _token_mixer(h, zmix, att, lw, dft):
    b, n, d = h.shape
    y_conv, y_pool = _stencil(zmix, lw["conv_w"], lw["pool_w"], lw["pool_scale"])
    y_f = _fourier(zmix, *dft)
    branches = [y.reshape(b * n, MIX_W) for y in (y_conv, att, y_f, y_pool)]
    merged = _merge(h.reshape(b * n, d), branches, lw["w_in"], lw["w_branch"], lw["layer"])
    return merged.reshape(b, n, d)


def _route(h2, aff, affn):
    b, n, d = h2.shape
    cap = EC_CAPACITY * n // N_EXPERTS
    slot, slott = _select(aff, cap)
    per_tile = jnp.sum((slot >= 0).reshape(b, N_EXPERTS, n // SEL_TILE, SEL_TILE), axis=-1, dtype=I32)
    cum = jnp.concatenate([jnp.zeros((b, N_EXPERTS, 1), I32), jnp.cumsum(per_tile, axis=-1)], axis=-1)
    win = min(EC_WINDOW, cap)
    span = cum[..., 1:] - (cum[..., :-1] // ROW_ALIGN) * ROW_ALIGN
    rounds = jnp.max((span + win - 1) // win, axis=1).reshape(-1)
    cum = cum.reshape(-1)
    xs, gate = _gather(cum, rounds, slot, h2, affn, cap)
    tables = (cum, rounds, slott, cap)
    return tables, (xs.reshape(N_EXPERTS, b * cap, d), gate.reshape(N_EXPERTS, b * cap, LANES))


def _combine(tables, ys, x1, g2, next_norm=None):
    cum, rounds, slott, cap = tables
    b, _, d = x1.shape
    return _scatter(cum, rounds, slott, ys.reshape(N_EXPERTS, b, cap, d), x1, g2, cap, next_norm)


def kernel(x, c, ctx, c_ctx, w_ada, b_ada, norm1_w, norm2_w, w_in, conv_w, q_norm_w, k_norm_w,
           na_rpb, pool_w, pool_scale, w_branch, w_out, w_router, w_exp_gate, w_exp_up, w_exp_down):
    bsz, n, d = x.shape
    l_ctx = ctx.shape[1]
    depth = w_ada.shape[0]
    n_rows = n // GRID_W

    c16 = jnp.zeros((16, d), F32).at[:bsz].set(c).at[bsz].set(c_ctx)
    mod_all = _ada(c16, w_ada, b_ada)

    hd = np.arange(MIX_W) // HEAD_DIM
    bd = jnp.asarray(hd[:, None] == hd[None, :], dtype=BF16)
    dft_lat = _dft_mats(n, 1.0, n // 2) + _dft_mats(FOURIER_GC, (n * FOURIER_GC) ** -0.5)
    dft_ctx = _dft_mats(l_ctx, 1.0, l_ctx // 2) + _dft_mats(FOURIER_GC, (l_ctx * FOURIER_GC) ** -0.5)

    mods = [_mod_chunks(mod_all[l, :bsz]) for l in range(depth)]
    h = _prenorm(x, norm1_w[0], mods[0][0], mods[0][1])
    for l in range(depth):
        last = l == depth - 1
        lw = {
            "w_in": w_in, "conv_w": conv_w[l], "pool_w": pool_w[l],
            "pool_scale": pool_scale[l], "w_branch": w_branch, "layer": l,
        }
        w_out_bf = w_out[l].astype(BF16)
        wr_pad = jnp.zeros((d, LANES), F32).at[:, :N_EXPERTS].set(w_router[l])
        wr_hi = wr_pad.astype(BF16)
        wr_cat = jnp.concatenate([wr_hi, (wr_pad - wr_hi.astype(F32)).astype(BF16)], axis=1)
        q_w = jnp.tile(q_norm_w[l], N_HEADS).reshape(1, MIX_W)
        k_w = jnp.tile(k_norm_w[l], N_HEADS).reshape(1, MIX_W)
        tb = _na_bias_table(na_rpb[l], n_rows)
        _, _, g1, sh2, sc2, g2 = mods[l]
        mc = _mod_chunks(jnp.broadcast_to(mod_all[l, bsz:bsz + 1], (bsz, N_MOD * d)))

        hc = _prenorm(ctx, norm1_w[l], mc[0], mc[1])
        zc_mix = _in_proj(hc.reshape(bsz * l_ctx, d), w_in, l, MIX_COLS).reshape(bsz, l_ctx, MIX_COLS)

        zmix = _in_proj(h.reshape(bsz * n, d), w_in, l, MIX_COLS).reshape(bsz, n, MIX_COLS)
        att = _neighbourhood_attention(zmix, zc_mix, q_w, k_w, bd, tb)
        merged = _token_mixer(h, zmix, att, lw, dft_lat)
        x1, h2, aff, affn = _out_proj(merged, w_out_bf, x, g1, norm2_w[l], sh2, sc2, wr_cat)
        tables, routed = _route(h2, aff, affn)
        expert_w = (w_exp_gate, w_exp_up, w_exp_down, l)
        if last:
            x = _combine(tables, _experts(*routed, *expert_w), x1, g2)
        else:
            att_c = _context_attention(zc_mix, q_w, k_w, bd)
            merged_c = _token_mixer(hc, zc_mix, att_c, lw, dft_ctx)
            c1, hc2, aff_c, affn_c = _out_proj(merged_c, w_out_bf, ctx, mc[2], norm2_w[l], mc[3], mc[4], wr_cat)
            tables_c, routed_c = _route(hc2, aff_c, affn_c)
            x, h = _combine(tables, _experts(*routed, *expert_w), x1, g2,
                            (norm1_w[l + 1], mods[l + 1][0], mods[l + 1][1]))
            ctx = _combine(tables_c, _experts(*routed_c, *expert_w), c1, mc[5])
    return x
```

```python
import functools

import numpy as np
import jax
import jax.numpy as jnp
from jax import lax
from jax.experimental import pallas as pl
from jax.experimental.pallas import tpu as pltpu

F32 = jnp.float32
BF16 = jnp.bfloat16
I32 = jnp.int32

D_MODEL = 2048
GRID_W = 64
MIX_W = D_MODEL // 4
N_HEADS = 8
HEAD_DIM = MIX_W // N_HEADS
WIN_H = 8
WIN_W = 16
ATT_SCALE = HEAD_DIM ** -0.5
POOL_WINDOWS = (2, 4, 8, 16)
POOL_GC = MIX_W // len(POOL_WINDOWS)
FOURIER_GC = MIX_W // 4
N_EXPERTS = 16
EC_CAPACITY = 2
N_MOD = 6
EPS = 1e-6
NEG_INF = -1e30
MIX_COLS = 8 * MIX_W
LANES = 128
HALO = 16
SEL_TILE = 256
ROW_ALIGN = 16
EC_WINDOW = 64
ROWS_PER_PASS = 2
SCATTER_CHUNKS = 4
GATHER_CHUNKS = 4
VMEM_LIMIT = 56 * 1024 * 1024

COL_XA, COL_GB, COL_GC, COL_Q, COL_K, COL_V, COL_F, COL_P = range(8)


def _params(sem, vmem=VMEM_LIMIT):
    return pltpu.CompilerParams(dimension_semantics=sem, vmem_limit_bytes=vmem)


def _dot(a, b):
    return jnp.dot(a, b, preferred_element_type=F32)


def _dot_nt(a, b):
    return lax.dot_general(a, b, (((1,), (1,)), ((), ())), preferred_element_type=F32)


def _ada_kernel(c_ref, w_ref, b_ref, o_ref):
    c = c_ref[...]
    s = c * jax.nn.sigmoid(c)
    w = w_ref[0]
    s_hi, w_hi = s.astype(BF16), w.astype(BF16)
    s_lo = (s - s_hi.astype(F32)).astype(BF16)
    w_lo = (w - w_hi.astype(F32)).astype(BF16)
    o_ref[0] = _dot(s_hi, w_hi) + (_dot(s_lo, w_hi) + _dot(s_hi, w_lo)) + b_ref[0]


def _ada(c16, w_ada, b_ada):
    depth, d, ncol = w_ada.shape
    tn = 1024
    return pl.pallas_call(
        _ada_kernel,
        grid=(depth, ncol // tn),
        in_specs=[
            pl.BlockSpec((16, d), lambda l, j: (0, 0)),
            pl.BlockSpec((1, d, tn), lambda l, j: (l, 0, j)),
            pl.BlockSpec((1, 1, tn), lambda l, j: (l, 0, j)),
        ],
        out_specs=pl.BlockSpec((1, 16, tn), lambda l, j: (l, 0, j)),
        out_shape=jax.ShapeDtypeStruct((depth, 16, ncol), F32),
        compiler_params=_params(("parallel", "parallel")),
        name="ada",
    )(c16, w_ada, b_ada.reshape(depth, 1, ncol))


def _prenorm_math(x, w, shift, scale):
    ms = jnp.mean(x * x, axis=-1, keepdims=True)
    y = x * lax.rsqrt(ms + EPS) * w
    return y * (1.0 + scale) + shift


def _prenorm_kernel(x_ref, w_ref, sh_ref, sc_ref, o_ref):
    o_ref[0] = _prenorm_math(x_ref[0], w_ref[...], sh_ref[0], sc_ref[0]).astype(BF16)


def _prenorm(x, w, shift, scale):
    b, n, d = x.shape
    tm = min(1024, n)
    return pl.pallas_call(
        _prenorm_kernel,
        grid=(b, n // tm),
        in_specs=[
            pl.BlockSpec((1, tm, d), lambda i, j: (i, j, 0)),
            pl.BlockSpec((1, d), lambda i, j: (0, 0)),
            pl.BlockSpec((1, 1, d), lambda i, j: (i, 0, 0)),
            pl.BlockSpec((1, 1, d), lambda i, j: (i, 0, 0)),
        ],
        out_specs=pl.BlockSpec((1, tm, d), lambda i, j: (i, j, 0)),
        out_shape=jax.ShapeDtypeStruct((b, n, d), BF16),
        compiler_params=_params(("parallel", "parallel")),
        name="prenorm",
    )(x, w.reshape(1, d), shift, scale)


def _mm_kernel(a_ref, w_ref, o_ref, w_bf):
    @pl.when(pl.program_id(1) == 0)
    def _():
        w_bf[...] = w_ref[0].astype(BF16)

    o_ref[...] = _dot(a_ref[...], w_bf[...]).astype(o_ref.dtype)


def _in_proj(h2d, w, layer, ncols):
    m, k = h2d.shape
    tm = min(1024, m)
    tn = 1024
    return pl.pallas_call(
        _mm_kernel,
        grid=(ncols // tn, m // tm),
        in_specs=[
            pl.BlockSpec((tm, k), lambda j, i: (i, 0)),
            pl.BlockSpec((1, k, tn), lambda j, i: (layer, 0, j)),
        ],
        out_specs=pl.BlockSpec((tm, tn), lambda j, i: (i, j)),
        out_shape=jax.ShapeDtypeStruct((m, ncols), BF16),
        scratch_shapes=[pltpu.VMEM((k, tn), BF16)],
        compiler_params=_params(("parallel", "arbitrary")),
        name="in_proj",
    )(h2d, w)


def _stencil_kernel(xa_ref, gb_ref, gc_ref, zp_ref, xa_p, xa_n, gc_p, gc_n, zp_p, zp_n,
                    cw_ref, pw_ref, ps_ref, oc_ref, op_ref, *, tm, n_seq):
    i = pl.program_id(1)
    nt = pl.num_programs(1)
    rows = tm + 2 * HALO

    def ext(main_ref, prev_ref, next_ref):
        main = main_ref[0].astype(F32)
        prev = jnp.where(i > 0, prev_ref[0].astype(F32), 0.0)
        nxt = jnp.where(i < nt - 1, next_ref[0].astype(F32), 0.0)
        return jnp.concatenate([prev, main, nxt], axis=0)

    def shifted(a, d):
        if d == 0:
            return a[HALO:HALO + tm]
        return pltpu.roll(a, (-d) % rows, axis=0)[HALO:HALO + tm]

    u = ext(gc_ref, gc_p, gc_n) * ext(xa_ref, xa_p, xa_n)
    cw = cw_ref[...]
    y = cw[0:1] * shifted(u, -1) + cw[1:2] * shifted(u, 0) + cw[2:3] * shifted(u, 1)
    oc_ref[0] = (gb_ref[0].astype(F32) * y).astype(BF16)

    t = (i * tm + lax.broadcasted_iota(I32, (tm, 1), 0)).astype(F32)
    zp = ext(zp_ref, zp_p, zp_n)
    for g, win in enumerate(POOL_WINDOWS):
        ug = zp[:, g * POOL_GC:(g + 1) * POOL_GC]
        acc = ug + pltpu.roll(ug, 1, axis=0)
        half = 1
        while 2 * half < win:
            acc = pltpu.roll(acc, rows - half, axis=0) + pltpu.roll(acc, half, axis=0)
            half *= 2
        wsum = acc[HALO:HALO + tm]
        lo = jnp.maximum(t - (win // 2), 0.0)
        hi = jnp.minimum(t + (win - win // 2 - 1), float(n_seq - 1))
        pooled = wsum / (hi - lo + 1.0) - ug[HALO:HALO + tm]
        yg = _dot(pooled.astype(BF16), pw_ref[g])
        op_ref[0, :, g * POOL_GC:(g + 1) * POOL_GC] = (
            yg * ps_ref[:, g * POOL_GC:(g + 1) * POOL_GC]).astype(BF16)


def _stencil(zmix, conv_w, pool_w, pool_scale):
    b, n, _ = zmix.shape
    tm = min(512, n)
    hb = tm // HALO
    last = n // HALO - 1

    def main(col):
        return pl.BlockSpec((1, tm, MIX_W), lambda bi, i: (bi, i, col))

    def prev(col):
        return pl.BlockSpec((1, HALO, MIX_W), lambda bi, i: (bi, jnp.maximum(i * hb - 1, 0), col))

    def nxt(col):
        return pl.BlockSpec((1, HALO, MIX_W), lambda bi, i: (bi, jnp.minimum((i + 1) * hb, last), col))

    out_spec = pl.BlockSpec((1, tm, MIX_W), lambda bi, i: (bi, i, 0))
    return pl.pallas_call(
        functools.partial(_stencil_kernel, tm=tm, n_seq=n),
        grid=(b, n // tm),
        in_specs=[main(COL_XA), main(COL_GB), main(COL_GC), main(COL_P),
                  prev(COL_XA), nxt(COL_XA), prev(COL_GC), nxt(COL_GC), prev(COL_P), nxt(COL_P),
                  pl.BlockSpec((3, MIX_W), lambda bi, i: (0, 0)),
                  pl.BlockSpec((len(POOL_WINDOWS), POOL_GC, POOL_GC), lambda bi, i: (0, 0, 0)),
                  pl.BlockSpec((1, MIX_W), lambda bi, i: (0, 0))],
        out_specs=[out_spec, out_spec],
        out_shape=[jax.ShapeDtypeStruct((b, n, MIX_W), BF16)] * 2,
        compiler_params=_params(("parallel", "parallel")),
        name="stencil",
    )(zmix, zmix, zmix, zmix, zmix, zmix, zmix, zmix, zmix, zmix,
      conv_w, pool_w.astype(BF16), pool_scale.reshape(1, MIX_W))


def _fourier_kernel(cn_ref, sn_ref, u_ref, cc_ref, sc_ref, o_ref, even, odd):
    n = u_ref.shape[1]
    half, tk = n // 2, cn_ref.shape[0]
    rb = min(256, half)

    @pl.when(pl.program_id(1) == 0)
    def _():
        i = lax.broadcasted_iota(I32, (rb, rb), 0)
        j = lax.broadcasted_iota(I32, (rb, rb), 1)
        flip = jnp.where(i + j == rb, 1.0, 0.0).astype(BF16)
        row = lax.broadcasted_iota(I32, (rb, 1), 0)
        for a in range(half // rb):
            src = n - (a + 1) * rb
            rev = _dot(flip, u_ref[0, src:src + rb, :])
            first = (n - a * rb) % n
            rev = jnp.where(row == 0, u_ref[0, first:first + 1, :].astype(F32), rev)
            ua = u_ref[0, a * rb:(a + 1) * rb, :].astype(F32)
            fold = ua + rev if a else jnp.where(row == 0, ua, ua + rev)
            even[a * rb:(a + 1) * rb, :] = fold.astype(BF16)
            odd[a * rb:(a + 1) * rb, :] = (ua - rev).astype(BF16)

    k = pl.program_id(1) * tk + lax.broadcasted_iota(I32, (tk, 1), 0)
    sign = (1 - 2 * jnp.bitwise_and(k, 1)).astype(F32)
    p = (_dot(cn_ref[...], even[...]) + sign * u_ref[0, half:half + 1, :].astype(F32)).astype(BF16)
    q = _dot(sn_ref[...], odd[...]).astype(BF16)
    for g in range(MIX_W // FOURIER_GC):
        sl = slice(g * FOURIER_GC, (g + 1) * FOURIER_GC)
        o_ref[0, :, sl] = (_dot(p[:, sl], cc_ref[...]) - _dot(q[:, sl], sc_ref[...])).astype(BF16)


def _dft_mats(n, scale, ncols=None):
    ncols = n if ncols is None else ncols
    k = jnp.arange(n, dtype=I32)[:, None]

    def table(count, period):
        r = (k * jnp.arange(count, dtype=I32)[None, :]) % period
        ang = r.astype(F32) * np.float32(2.0 * np.pi / period)
        return jnp.cos(ang), jnp.sin(ang)

    inner = 64
    if n <= inner or n % inner or ncols % inner:
        c, s = table(ncols, n)
    else:
        ca, sa = table(ncols // inner, n // inner)
        cb, sb = table(inner, n)
        c = (ca[:, :, None] * cb[:, None, :] - sa[:, :, None] * sb[:, None, :]).reshape(n, ncols)
        s = (sa[:, :, None] * cb[:, None, :] + ca[:, :, None] * sb[:, None, :]).reshape(n, ncols)
    return (c * scale).astype(BF16), (s * scale).astype(BF16)


def _fourier(zmix, cn, sn, cc, sc):
    b, n, _ = zmix.shape
    half = n // 2
    tk = min(512, n)
    return pl.pallas_call(
        _fourier_kernel,
        grid=(b, n // tk),
        in_specs=[
            pl.BlockSpec((tk, half), lambda bi, k: (k, 0)),
            pl.BlockSpec((tk, half), lambda bi, k: (k, 0)),
            pl.BlockSpec((1, n, MIX_W), lambda bi, k: (bi, 0, COL_F)),
            pl.BlockSpec((FOURIER_GC, FOURIER_GC), lambda bi, k: (0, 0)),
            pl.BlockSpec((FOURIER_GC, FOURIER_GC), lambda bi, k: (0, 0)),
        ],
        out_specs=pl.BlockSpec((1, tk, MIX_W), lambda bi, k: (bi, k, 0)),
        out_shape=jax.ShapeDtypeStruct((b, n, MIX_W), BF16),
        scratch_shapes=[pltpu.VMEM((half, MIX_W), BF16), pltpu.VMEM((half, MIX_W), BF16)],
        compiler_params=_params(("parallel", "arbitrary")),
        name="fourier",
    )(cn, sn, zmix, cc, sc)


def _head_rms(x, w, bd):
    x2 = x * x
    hi = x2.astype(BF16)
    lo = (x2 - hi.astype(F32)).astype(BF16)
    ms = (_dot(hi, bd) + _dot(lo, bd)) * (1.0 / HEAD_DIM)
    return x * lax.rsqrt(ms + EPS) * w


def _pair_stack(q2):
    lane = lax.broadcasted_iota(I32, q2.shape, 1)
    zero = jnp.zeros_like(q2)
    return jnp.concatenate([jnp.where(lane < HEAD_DIM, q2, zero),
                            jnp.where(lane >= HEAD_DIM, q2, zero)], axis=0)


def _pair_unstack(o2, m):
    lane = lax.broadcasted_iota(I32, (m, LANES), 1)
    return jnp.where(lane < HEAD_DIM, o2[:m], o2[m:])


def _na_kernel(q_ref, k_ref, v_ref, kc_ref, vc_ref, qw_ref, kw_ref, bd_ref, tb_ref, o_ref,
               kn_scr, kcn_scr, qn_scr, *, rt, n_rows):
    rb = pl.program_id(1)
    n_seq = n_rows * GRID_W
    chunk = 512

    @pl.when(rb == 0)
    def _():
        def body(c, carry):
            sl = pl.ds(pl.multiple_of(c * chunk, chunk), chunk)
            kn_scr[sl, :] = _head_rms(k_ref[0, sl, :].astype(F32), kw_ref[...], bd_ref[...]).astype(BF16)
            return carry
        lax.fori_loop(0, n_seq // chunk, body, 0)
        kcn_scr[...] = _head_rms(kc_ref[0].astype(F32), kw_ref[...], bd_ref[...]).astype(BF16)

    qn_scr[...] = (_head_rms(q_ref[0].astype(F32), qw_ref[...], bd_ref[...]) * ATT_SCALE).astype(BF16)

    kh = min(WIN_H, n_rows)
    nloc = kh * GRID_W

    def row_body(jj, carry):
        units = []
        for i in range(ROWS_PER_PASS):
            j = jj * ROWS_PER_PASS + i
            r = rb * rt + j
            rs = jnp.clip(r - kh // 2, 0, n_rows - kh)
            q_rows = pl.ds(pl.multiple_of(j * GRID_W, GRID_W), GRID_W)
            k_rows = pl.ds(pl.multiple_of(rs * GRID_W, GRID_W), nloc)
            for p in range(N_HEADS // 2):
                units.append((q_rows, k_rows, rs - r + (WIN_H - 1), p, slice(p * LANES, (p + 1) * LANES)))
        qs = [_pair_stack(qn_scr[q_rows, cols]) for q_rows, _, _, _, cols in units]
        s_loc = [_dot_nt(q, kn_scr[k_rows, cols]) for q, (_, k_rows, _, _, cols) in zip(qs, units)]
        s_ctx = [_dot_nt(q, kcn_scr[:, cols]) for q, (_, _, _, _, cols) in zip(qs, units)]
        s_loc = [s + tb_ref[dr0, p] for s, (_, _, dr0, p, _) in zip(s_loc, units)]
        m = [jnp.maximum(jnp.max(a, axis=1, keepdims=True), jnp.max(c, axis=1, keepdims=True))
             for a, c in zip(s_loc, s_ctx)]
        p_loc = [jnp.exp(a - mx) for a, mx in zip(s_loc, m)]
        p_ctx = [jnp.exp(c - mx) for c, mx in zip(s_ctx, m)]
        denom = [jnp.sum(a, axis=1, keepdims=True) + jnp.sum(c, axis=1, keepdims=True) for a, c in zip(p_loc, p_ctx)]
        o2 = [_dot(a.astype(BF16), v_ref[0, k_rows, cols]) + _dot(c.astype(BF16), vc_ref[0, :, cols])
              for a, c, (_, k_rows, _, _, cols) in zip(p_loc, p_ctx, units)]
        for o, d, (q_rows, _, _, _, cols) in zip(o2, denom, units):
            o_ref[0, q_rows, cols] = _pair_unstack(o / d, GRID_W).astype(BF16)
        return carry

    lax.fori_loop(0, rt // ROWS_PER_PASS, row_body, 0)


def _na_bias_table(rpb, n_rows):
    kh = min(WIN_H, n_rows)
    c = np.arange(GRID_W)
    cs = np.clip(c - WIN_W // 2, 0, GRID_W - WIN_W)
    kc = np.arange(GRID_W)
    ok = (kc[None, :] >= cs[:, None]) & (kc[None, :] < cs[:, None] + WIN_W)
    edge = GRID_W - WIN_W
    padded = jnp.pad(rpb.astype(F32), ((0, 0), (0, 0), (edge, edge)))
    toeplitz = jnp.stack([padded[:, :, GRID_W - 1 - q:2 * GRID_W - 1 - q] for q in range(GRID_W)], axis=2)
    masked = jnp.where(ok[None, None], toeplitz, NEG_INF)
    t = jnp.stack([masked[:, d0:d0 + kh] for d0 in range(WIN_H)], axis=0)
    t = jnp.transpose(t, (0, 1, 3, 2, 4))
    return t.reshape(WIN_H, N_HEADS // 2, 2 * GRID_W, kh * GRID_W)


def _neighbourhood_attention(zmix, zc_mix, q_w, k_w, bd, tb):
    b, n, _ = zmix.shape
    l = zc_mix.shape[1]
    n_rows = n // GRID_W
    rt = 8
    nloc = min(WIN_H, n_rows) * GRID_W
    return pl.pallas_call(
        functools.partial(_na_kernel, rt=rt, n_rows=n_rows),
        grid=(b, n_rows // rt),
        in_specs=[
            pl.BlockSpec((1, rt * GRID_W, MIX_W), lambda bi, r: (bi, r, COL_Q)),
            pl.BlockSpec((1, n, MIX_W), lambda bi, r: (bi, 0, COL_K)),
            pl.BlockSpec((1, n, MIX_W), lambda bi, r: (bi, 0, COL_V)),
            pl.BlockSpec((1, l, MIX_W), lambda bi, r: (bi, 0, COL_K)),
            pl.BlockSpec((1, l, MIX_W), lambda bi, r: (bi, 0, COL_V)),
            pl.BlockSpec((1, MIX_W), lambda bi, r: (0, 0)),
            pl.BlockSpec((1, MIX_W), lambda bi, r: (0, 0)),
            pl.BlockSpec((MIX_W, MIX_W), lambda bi, r: (0, 0)),
            pl.BlockSpec((WIN_H, N_HEADS // 2, 2 * GRID_W, nloc), lambda bi, r: (0, 0, 0, 0),
                         pipeline_mode=pl.Buffered(1)),
        ],
        out_specs=pl.BlockSpec((1, rt * GRID_W, MIX_W), lambda bi, r: (bi, r, 0)),
        out_shape=jax.ShapeDtypeStruct((b, n, MIX_W), BF16),
        scratch_shapes=[pltpu.VMEM((n, MIX_W), BF16), pltpu.VMEM((l, MIX_W), BF16),
                        pltpu.VMEM((rt * GRID_W, MIX_W), BF16)],
        compiler_params=_params(("parallel", "arbitrary")),
        name="nbr_attention",
    )(zmix, zmix, zmix, zc_mix, zc_mix, q_w, k_w, bd, tb)


def _ctx_attn_kernel(q_ref, k_ref, v_ref, qw_ref, kw_ref, bd_ref, o_ref):
    l = q_ref.shape[1]
    qn = (_head_rms(q_ref[0].astype(F32), qw_ref[...], bd_ref[...]) * ATT_SCALE).astype(BF16)
    kn = _head_rms(k_ref[0].astype(F32), kw_ref[...], bd_ref[...]).astype(BF16)
    for p in range(N_HEADS // 2):
        cols = slice(p * LANES, (p + 1) * LANES)
        qs = _pair_stack(qn[:, cols])
        s = _dot_nt(qs, kn[:, cols])
        m = jnp.max(s, axis=1, keepdims=True)
        e = jnp.exp(s - m)
        o2 = _dot(e.astype(BF16), v_ref[0, :, cols]) / jnp.sum(e, axis=1, keepdims=True)
        o_ref[0, :, cols] = _pair_unstack(o2, l).astype(BF16)


def _context_attention(zc_mix, q_w, k_w, bd):
    b, l, _ = zc_mix.shape
    vec = pl.BlockSpec((1, MIX_W), lambda bi: (0, 0))
    return pl.pallas_call(
        _ctx_attn_kernel,
        grid=(b,),
        in_specs=[
            pl.BlockSpec((1, l, MIX_W), lambda bi: (bi, 0, COL_Q)),
            pl.BlockSpec((1, l, MIX_W), lambda bi: (bi, 0, COL_K)),
            pl.BlockSpec((1, l, MIX_W), lambda bi: (bi, 0, COL_V)),
            vec, vec,
            pl.BlockSpec((MIX_W, MIX_W), lambda bi: (0, 0)),
        ],
        out_specs=pl.BlockSpec((1, l, MIX_W), lambda bi: (bi, 0, 0)),
        out_shape=jax.ShapeDtypeStruct((b, l, MIX_W), BF16),
        compiler_params=_params(("parallel",)),
        name="ctx_attention",
    )(zc_mix, zc_mix, zc_mix, q_w, k_w, bd)


def _merge_kernel(h_ref, y0, y1, y2, y3, g0, g1, g2, g3, wb_ref, o_ref, wg_bf, wb_bf):
    @pl.when(pl.program_id(1) == 0)
    def _():
        for i, g in enumerate((g0, g1, g2, g3)):
            wg_bf[i] = g[0].astype(BF16)
        wb_bf[...] = wb_ref[0].astype(BF16)

    h = h_ref[...]
    acc = None
    for i, y in enumerate((y0, y1, y2, y3)):
        gate = jax.nn.sigmoid(_dot(h, wg_bf[i]))
        term = gate * _dot(y[...], wb_bf[i])
        acc = term if acc is None else acc + term
    o_ref[...] = acc.astype(BF16)


def _merge(h2d, branches, w_in, w_branch, layer):
    m, d = h2d.shape
    tm = min(1024, m)
    tn = 256
    gate0 = MIX_COLS // tn

    def gspec(i):
        return pl.BlockSpec((1, d, tn), lambda n, r: (layer, 0, gate0 + i * (d // tn) + n))

    yspec = pl.BlockSpec((tm, MIX_W), lambda n, r: (r, 0))
    return pl.pallas_call(
        _merge_kernel,
        grid=(d // tn, m // tm),
        in_specs=[pl.BlockSpec((tm, d), lambda n, r: (r, 0)), yspec, yspec, yspec, yspec,
                  gspec(0), gspec(1), gspec(2), gspec(3),
                  pl.BlockSpec((1, 4, MIX_W, tn), lambda n, r: (layer, 0, 0, n))],
        out_specs=pl.BlockSpec((tm, tn), lambda n, r: (r, n)),
        out_shape=jax.ShapeDtypeStruct((m, d), BF16),
        scratch_shapes=[pltpu.VMEM((4, d, tn), BF16), pltpu.VMEM((4, MIX_W, tn), BF16)],
        compiler_params=_params(("parallel", "arbitrary")),
        name="merge",
    )(h2d, *branches, w_in, w_in, w_in, w_in, w_branch)


def _out_proj_kernel(m_ref, w_ref, x_ref, g_ref, nw_ref, sh_ref, sc_ref, wrc_ref,
                     x1_ref, h2_ref, aff_ref, affn_ref):
    x1 = x_ref[0] + g_ref[0] * _dot(m_ref[0], w_ref[...])
    x1_ref[0] = x1
    h2 = _prenorm_math(x1, nw_ref[...], sh_ref[0], sc_ref[0])
    h_hi = h2.astype(BF16)
    h2_ref[0] = h_hi
    h_lo = (h2 - h_hi.astype(F32)).astype(BF16)
    both = _dot(h_hi, wrc_ref[...])
    logits = both[:, :LANES] + (both[:, LANES:] + _dot(h_lo, wrc_ref[:, :LANES]))
    lane = lax.broadcasted_iota(I32, logits.shape, 1)
    logits = jnp.where(lane < N_EXPERTS, logits, NEG_INF)
    e = jnp.exp(logits - jnp.max(logits, axis=1, keepdims=True))
    aff = e / jnp.sum(e, axis=1, keepdims=True)
    aff_ref[0] = aff.T[:N_EXPERTS]
    hi = aff.astype(BF16).astype(F32)
    r1 = aff - hi
    mid = r1.astype(BF16).astype(F32)
    lo = (r1 - mid).astype(BF16).astype(F32)
    packed = hi + pltpu.roll(mid, N_EXPERTS, axis=1) + pltpu.roll(lo, 2 * N_EXPERTS, axis=1)
    affn_ref[0] = packed.astype(BF16)


def _out_proj(merged, w_out_bf, x, gate, norm_w, shift, scale, wr_cat):
    b, n, d = x.shape
    tm = min(512, n)
    vec = pl.BlockSpec((1, 1, d), lambda i, j: (i, 0, 0))
    tile = pl.BlockSpec((1, tm, d), lambda i, j: (i, j, 0))
    wr_spec = pl.BlockSpec((d, 2 * LANES), lambda i, j: (0, 0))
    return pl.pallas_call(
        _out_proj_kernel,
        grid=(b, n // tm),
        in_specs=[tile, pl.BlockSpec((d, d), lambda i, j: (0, 0), pipeline_mode=pl.Buffered(1)), tile, vec,
                  pl.BlockSpec((1, d), lambda i, j: (0, 0)), vec, vec, wr_spec],
        out_specs=[tile, tile,
                   pl.BlockSpec((1, N_EXPERTS, tm), lambda i, j: (i, 0, j)),
                   pl.BlockSpec((1, tm, LANES), lambda i, j: (i, j, 0))],
        out_shape=[jax.ShapeDtypeStruct((b, n, d), F32), jax.ShapeDtypeStruct((b, n, d), BF16),
                   jax.ShapeDtypeStruct((b, N_EXPERTS, n), F32), jax.ShapeDtypeStruct((b, n, LANES), BF16)],
        compiler_params=_params(("parallel", "parallel")),
        name="out_proj",
    )(merged, w_out_bf, x, gate, norm_w.reshape(1, d), shift, scale, wr_cat)


def _lane_cumsum(mask_f, ut):
    rows, n = mask_f.shape
    carry = jnp.zeros((rows, 1), F32)
    parts = []
    for j in range(n // LANES):
        c = _dot(mask_f[:, j * LANES:(j + 1) * LANES].astype(BF16), ut) + carry
        parts.append(c)
        carry = c[:, LANES - 1:LANES]
    return jnp.concatenate(parts, axis=1)


def _select_kernel(aff_ref, slot_ref, slott_ref, *, cap):
    aff = aff_ref[0]
    e, n = aff.shape
    bits = lax.bitcast_convert_type(aff, I32)

    def body(_, carry):
        lo, hi = carry
        mid = lo + ((hi - lo) >> 1)
        cnt = jnp.sum(jnp.where(bits >= mid, 1.0, 0.0), axis=1, keepdims=True)
        ge = cnt >= float(cap)
        return jnp.where(ge, mid, lo), jnp.where(ge, hi, mid)

    lo, _ = lax.fori_loop(0, 31, body, (jnp.zeros((e, 1), I32), jnp.full((e, 1), 0x7F800000, I32)))
    ri = lax.broadcasted_iota(I32, (LANES, LANES), 0)
    ci = lax.broadcasted_iota(I32, (LANES, LANES), 1)
    ut = jnp.where(ri <= ci, 1.0, 0.0).astype(BF16)
    gt = bits > lo
    eq = bits == lo
    need = float(cap) - jnp.sum(jnp.where(gt, 1.0, 0.0), axis=1, keepdims=True)
    ceq = _lane_cumsum(jnp.where(eq, 1.0, 0.0), ut)
    sel = gt | (eq & (ceq <= need))
    csel = _lane_cumsum(jnp.where(sel, 1.0, 0.0), ut)
    slot = jnp.where(sel, csel - 1.0, -1.0)
    slot_ref[0] = slot.astype(I32)
    padded = jnp.concatenate([slot, jnp.full((LANES - e, n), -1.0, F32)], axis=0)
    slott_ref[0] = padded.T


def _select(aff, cap):
    b, e, n = aff.shape
    return pl.pallas_call(
        functools.partial(_select_kernel, cap=cap),
        grid=(b,),
        in_specs=[pl.BlockSpec((1, e, n), lambda i: (i, 0, 0))],
        out_specs=[pl.BlockSpec((1, e, n), lambda i: (i, 0, 0)),
                   pl.BlockSpec((1, n, LANES), lambda i: (i, 0, 0))],
        out_shape=[jax.ShapeDtypeStruct((b, e, n), I32), jax.ShapeDtypeStruct((b, n, LANES), F32)],
        compiler_params=_params(("parallel",)),
        name="select",
    )(aff)


def _window(cum_ref, b, e, j, r, n_tiles, cap, win):
    c_lo = cum_ref[(b * N_EXPERTS + e) * (n_tiles + 1) + j]
    first = (c_lo // ROW_ALIGN) * ROW_ALIGN + r * win
    start = pl.multiple_of(jnp.minimum(first, cap - win), ROW_ALIGN)
    return first, start


def _gather_kernel(cum_ref, rnd_ref, slot_ref, h_ref, affn_ref, xs_ref, gate_ref, gacc, *, cap, win, n_tiles):
    b = pl.program_id(0)
    dq = pl.program_id(1)
    xs_ref[...] = jnp.zeros_like(xs_ref)

    @pl.when(dq == 0)
    def _():
        gacc[...] = jnp.zeros_like(gacc)

    rows = lax.broadcasted_iota(I32, (win, SEL_TILE), 0)

    def tile_body(j, carry):
        toks = pl.ds(pl.multiple_of(j * SEL_TILE, SEL_TILE), SEL_TILE)
        slots = slot_ref[0, j]
        h_tile = h_ref[0, toks, :]

        def round_body(r, carry2):
            starts, pieces = [], []
            for e in range(N_EXPERTS):
                first, start = _window(cum_ref, b, e, j, r, n_tiles, cap, win)
                s_row = slots[e:e + 1, :]
                hit = (rows + start == s_row) & (s_row >= first)
                pieces.append(jnp.where(hit, 1.0, 0.0).astype(BF16))
                starts.append(start)
            ge = N_EXPERTS // GATHER_CHUNKS
            onehots = [jnp.concatenate(pieces[lo:lo + ge], axis=0) for lo in range(0, N_EXPERTS, ge)]
            for c, onehot in enumerate(onehots):
                res = _dot(onehot, h_tile)
                for i in range(ge):
                    e = c * ge + i
                    sl = pl.ds(starts[e], win)
                    xs_ref[e, 0, sl, :] = xs_ref[e, 0, sl, :] + res[i * win:(i + 1) * win].astype(BF16)

            @pl.when(dq == 0)
            def _():
                for c, onehot in enumerate(onehots):
                    g = _dot(onehot, affn_ref[0, toks, :])
                    for i in range(ge):
                        gacc[c * ge + i, pl.ds(starts[c * ge + i], win), :] += g[i * win:(i + 1) * win]
            return carry2

        lax.fori_loop(0, rnd_ref[b * n_tiles + j], round_body, 0)
        return carry

    lax.fori_loop(0, n_tiles, tile_body, 0)

    @pl.when(dq == 0)
    def _():
        lane = lax.broadcasted_iota(I32, (cap, LANES), 1)
        for e in range(N_EXPERTS):
            mine = (lane == e) | (lane == N_EXPERTS + e) | (lane == 2 * N_EXPERTS + e)
            col = jnp.sum(jnp.where(mine, gacc[e], 0.0), axis=1, keepdims=True)
            gate_ref[e, 0] = jnp.broadcast_to(col, (cap, LANES))


def _gather(cum, rounds, slot, h2, affn, cap):
    b, n, d = h2.shape
    n_tiles = n // SEL_TILE
    win = min(EC_WINDOW, cap)
    dn = 512
    slot_tiles = slot.reshape(b, N_EXPERTS, n_tiles, SEL_TILE).transpose(0, 2, 1, 3)
    grid_spec = pltpu.PrefetchScalarGridSpec(
        num_scalar_prefetch=2,
        grid=(b, d // dn),
        in_specs=[
            pl.BlockSpec((1, n_tiles, N_EXPERTS, SEL_TILE), lambda i, q, c, r: (i, 0, 0, 0)),
            pl.BlockSpec((1, n, dn), lambda i, q, c, r: (i, 0, q)),
            pl.BlockSpec((1, n, LANES), lambda i, q, c, r: (i, 0, 0)),
        ],
        out_specs=[pl.BlockSpec((N_EXPERTS, 1, cap, dn), lambda i, q, c, r: (0, i, 0, q)),
                   pl.BlockSpec((N_EXPERTS, 1, cap, LANES), lambda i, q, c, r: (0, i, 0, 0))],
        scratch_shapes=[pltpu.VMEM((N_EXPERTS, cap, LANES), F32)],
    )
    return pl.pallas_call(
        functools.partial(_gather_kernel, cap=cap, win=win, n_tiles=n_tiles),
        grid_spec=grid_spec,
        out_shape=[jax.ShapeDtypeStruct((N_EXPERTS, b, cap, d), BF16),
                   jax.ShapeDtypeStruct((N_EXPERTS, b, cap, LANES), F32)],
        compiler_params=_params(("parallel", "arbitrary")),
        name="ec_gather",
    )(cum, rounds, slot_tiles, h2, affn)


def _expert_up_kernel(xs_ref, wg_ref, wu_ref, h_ref, wg_bf, wu_bf):
    @pl.when(pl.program_id(2) == 0)
    def _():
        wg_bf[...] = wg_ref[0, 0].astype(BF16)
        wu_bf[...] = wu_ref[0, 0].astype(BF16)

    x = xs_ref[0]
    a = _dot(x, wg_bf[...])
    u = _dot(x, wu_bf[...])
    h_ref[0] = (a * jax.nn.sigmoid(a) * u).astype(BF16)


def _expert_down_kernel(h_ref, wd_ref, gate_ref, ys_ref, wd_bf):
    @pl.when(pl.program_id(2) == 0)
    def _():
        wd_bf[...] = wd_ref[0, 0].astype(BF16)

    y = _dot(h_ref[0], wd_bf[...])
    g = gate_ref[0]
    ys_ref[0] = (y * jnp.concatenate([g] * (y.shape[1] // LANES), axis=1)).astype(BF16)


def _experts(xs, gate, wg, wu, wd, layer):
    e, m, d = xs.shape
    ff = wg.shape[3]
    tm = min(1024, m)
    tf = 512
    tn = 1024
    sem = ("parallel", "parallel", "arbitrary")
    hidden = pl.pallas_call(
        _expert_up_kernel,
        grid=(e, ff // tf, m // tm),
        in_specs=[
            pl.BlockSpec((1, tm, d), lambda ei, f, i: (ei, i, 0)),
            pl.BlockSpec((1, 1, d, tf), lambda ei, f, i: (layer, ei, 0, f)),
            pl.BlockSpec((1, 1, d, tf), lambda ei, f, i: (layer, ei, 0, f)),
        ],
        out_specs=pl.BlockSpec((1, tm, tf), lambda ei, f, i: (ei, i, f)),
        out_shape=jax.ShapeDtypeStruct((e, m, ff), BF16),
        scratch_shapes=[pltpu.VMEM((d, tf), BF16), pltpu.VMEM((d, tf), BF16)],
        compiler_params=_params(sem),
        name="expert_up",
    )(xs, wg, wu)
    return pl.pallas_call(
        _expert_down_kernel,
        grid=(e, d // tn, m // tm),
        in_specs=[
            pl.BlockSpec((1, tm, ff), lambda ei, j, i: (ei, i, 0)),
            pl.BlockSpec((1, 1, ff, tn), lambda ei, j, i: (layer, ei, 0, j)),
            pl.BlockSpec((1, tm, LANES), lambda ei, j, i: (ei, i, 0)),
        ],
        out_specs=pl.BlockSpec((1, tm, tn), lambda ei, j, i: (ei, i, j)),
        out_shape=jax.ShapeDtypeStruct((e, m, d), BF16),
        scratch_shapes=[pltpu.VMEM((ff, tn), BF16)],
        compiler_params=_params(sem),
        name="expert_down",
    )(hidden, wd, gate)


def _scatter_kernel(cum_ref, rnd_ref, slott_ref, ys_ref, x_ref, g_ref, *rest, cap, win, n_tiles, next_norm):
    if next_norm:
        nw_ref, sh_ref, sc_ref, o_ref, hn_ref = rest
    else:
        (o_ref,) = rest
    b = pl.program_id(0)
    j = pl.program_id(1)
    st = slott_ref[0]
    per_block = LANES // win
    lane = lax.broadcasted_iota(I32, (SEL_TILE, LANES), 1)
    lane_f = lane.astype(F32)
    group = lane // win

    def round_body(r, acc):
        blocks, pieces = [], []
        for k in range(N_EXPERTS // per_block):
            target = None
            for i in range(per_block):
                e = k * per_block + i
                first, start = _window(cum_ref, b, e, j, r, n_tiles, cap, win)
                col = st[:, e:e + 1]
                tgt = jnp.where(col >= first.astype(F32), col - (start - i * win).astype(F32), -1.0)
                target = tgt if target is None else jnp.where(group == i, tgt, target)
                pieces.append(ys_ref[e, 0, pl.ds(start, win), :])
            blocks.append(jnp.where(lane_f == target, 1.0, 0.0).astype(BF16))
        nb = max(1, len(blocks) // SCATTER_CHUNKS)
        for lo in range(0, len(blocks), nb):
            onehot = jnp.concatenate(blocks[lo:lo + nb], axis=1)
            stacked = jnp.concatenate(pieces[lo * per_block:(lo + nb) * per_block], axis=0)
            part = _dot(onehot, stacked)
            acc = part if acc is None else acc + part
        return acc

    acc = lax.fori_loop(1, rnd_ref[b * n_tiles + j], round_body, round_body(0, None))
    x2 = x_ref[0] + g_ref[0] * acc
    o_ref[0] = x2
    if next_norm:
        hn_ref[0] = _prenorm_math(x2, nw_ref[...], sh_ref[0], sc_ref[0]).astype(BF16)


def _scatter(cum, rounds, slott, ys, x1, gate, cap, next_norm=None):
    b, n, d = x1.shape
    n_tiles = n // SEL_TILE
    win = min(EC_WINDOW, cap)
    tile = pl.BlockSpec((1, SEL_TILE, d), lambda i, t, c, r: (i, t, 0))
    vec = pl.BlockSpec((1, 1, d), lambda i, t, c, r: (i, 0, 0))
    in_specs = [
        pl.BlockSpec((1, SEL_TILE, LANES), lambda i, t, c, r: (i, t, 0)),
        pl.BlockSpec((N_EXPERTS, 1, cap, d), lambda i, t, c, r: (0, i, 0, 0), pipeline_mode=pl.Buffered(1)),
        tile, vec,
    ]
    args = [cum, rounds, slott, ys, x1, gate]
    out_specs, out_shape = tile, jax.ShapeDtypeStruct((b, n, d), F32)
    if next_norm is not None:
        in_specs += [pl.BlockSpec((1, d), lambda i, t, c, r: (0, 0)), vec, vec]
        args += [next_norm[0].reshape(1, d), next_norm[1], next_norm[2]]
        out_specs, out_shape = [tile, tile], [out_shape, jax.ShapeDtypeStruct((b, n, d), BF16)]
    grid_spec = pltpu.PrefetchScalarGridSpec(
        num_scalar_prefetch=2, grid=(b, n_tiles), in_specs=in_specs, out_specs=out_specs)
    return pl.pallas_call(
        functools.partial(_scatter_kernel, cap=cap, win=win, n_tiles=n_tiles, next_norm=next_norm is not None),
        grid_spec=grid_spec,
        out_shape=out_shape,
        compiler_params=_params(("parallel", "arbitrary")),
        name="ec_scatter",
    )(*args)


def _mod_chunks(mod_rows):
    b = mod_rows.shape[0]
    return [mod_rows[:, i * D_MODEL:(i + 1) * D_MODEL].reshape(b, 1, D_MODEL) for i in range(N_MOD)]


def _token_mixer(h, zmix, att, lw, dft):
    b, n, d = h.shape
    y_conv, y_pool = _stencil(zmix, lw["conv_w"], lw["pool_w"], lw["pool_scale"])
    y_f = _fourier(zmix, *dft)
    branches = [y.reshape(b * n, MIX_W) for y in (y_conv, att, y_f, y_pool)]
    merged = _merge(h.reshape(b * n, d), branches, lw["w_in"], lw["w_branch"], lw["layer"])
    return merged.reshape(b, n, d)


def _route(h2, aff, affn):
    b, n, d = h2.shape
    cap = EC_CAPACITY * n // N_EXPERTS
    slot, slott = _select(aff, cap)
    per_tile = jnp.sum((slot >= 0).reshape(b, N_EXPERTS, n // SEL_TILE, SEL_TILE), axis=-1, dtype=I32)
    cum = jnp.concatenate([jnp.zeros((b, N_EXPERTS, 1), I32), jnp.cumsum(per_tile, axis=-1)], axis=-1)
    win = min(EC_WINDOW, cap)
    span = cum[..., 1:] - (cum[..., :-1] // ROW_ALIGN) * ROW_ALIGN
    rounds = jnp.max((span + win - 1) // win, axis=1).reshape(-1)
    cum = cum.reshape(-1)
    xs, gate = _gather(cum, rounds, slot, h2, affn, cap)
    tables = (cum, rounds, slott, cap)
    return tables, (xs.reshape(N_EXPERTS, b * cap, d), gate.reshape(N_EXPERTS, b * cap, LANES))


def _combine(tables, ys, x1, g2, next_norm=None):
    cum, rounds, slott, cap = tables
    b, _, d = x1.shape
    return _scatter(cum, rounds, slott, ys.reshape(N_EXPERTS, b, cap, d), x1, g2, cap, next_norm)


def kernel(x, c, ctx, c_ctx, w_ada, b_ada, norm1_w, norm2_w, w_in, conv_w, q_norm_w, k_norm_w,
           na_rpb, pool_w, pool_scale, w_branch, w_out, w_router, w_exp_gate, w_exp_up, w_exp_down):
    bsz, n, d = x.shape
    l_ctx = ctx.shape[1]
    depth = w_ada.shape[0]
    n_rows = n // GRID_W

    c16 = jnp.zeros((16, d), F32).at[:bsz].set(c).at[bsz].set(c_ctx)
    mod_all = _ada(c16, w_ada, b_ada)

    hd = np.arange(MIX_W) // HEAD_DIM
    bd = jnp.asarray(hd[:, None] == hd[None, :], dtype=BF16)
    dft_lat = _dft_mats(n, 1.0, n // 2) + _dft_mats(FOURIER_GC, (n * FOURIER_GC) ** -0.5)
    dft_ctx = _dft_mats(l_ctx, 1.0, l_ctx // 2) + _dft_mats(FOURIER_GC, (l_ctx * FOURIER_GC) ** -0.5)

    mods = [_mod_chunks(mod_all[l, :bsz]) for l in range(depth)]
    h = _prenorm(x, norm1_w[0], mods[0][0], mods[0][1])
    for l in range(depth):
        last = l == depth - 1
        lw = {
            "w_in": w_in, "conv_w": conv_w[l], "pool_w": pool_w[l],
            "pool_scale": pool_scale[l], "w_branch": w_branch, "layer": l,
        }
        w_out_bf = w_out[l].astype(BF16)
        wr_pad = jnp.zeros((d, LANES), F32).at[:, :N_EXPERTS].set(w_router[l])
        wr_hi = wr_pad.astype(BF16)
        wr_cat = jnp.concatenate([wr_hi, (wr_pad - wr_hi.astype(F32)).astype(BF16)], axis=1)
        q_w = jnp.tile(q_norm_w[l], N_HEADS).reshape(1, MIX_W)
        k_w = jnp.tile(k_norm_w[l], N_HEADS).reshape(1, MIX_W)
        tb = _na_bias_table(na_rpb[l], n_rows)
        _, _, g1, sh2, sc2, g2 = mods[l]
        mc = _mod_chunks(jnp.broadcast_to(mod_all[l, bsz:bsz + 1], (bsz, N_MOD * d)))

        hc = _prenorm(ctx, norm1_w[l], mc[0], mc[1])
        zc_mix = _in_proj(hc.reshape(bsz * l_ctx, d), w_in, l, MIX_COLS).reshape(bsz, l_ctx, MIX_COLS)

        zmix = _in_proj(h.reshape(bsz * n, d), w_in, l, MIX_COLS).reshape(bsz, n, MIX_COLS)
        att = _neighbourhood_attention(zmix, zc_mix, q_w, k_w, bd, tb)
        merged = _token_mixer(h, zmix, att, lw, dft_lat)
        x1, h2, aff, affn = _out_proj(merged, w_out_bf, x, g1, norm2_w[l], sh2, sc2, wr_cat)
        tables, routed = _route(h2, aff, affn)
        expert_w = (w_exp_gate, w_exp_up, w_exp_down, l)
        if last:
            x = _combine(tables, _experts(*routed, *expert_w), x1, g2)
        else:
            att_c = _context_attention(zc_mix, q_w, k_w, bd)
            merged_c = _token_mixer(hc, zc_mix, att_c, lw, dft_ctx)
            c1, hc2, aff_c, affn_c = _out_proj(merged_c, w_out_bf, ctx, mc[2], norm2_w[l], mc[3], mc[4], wr_cat)
            tables_c, routed_c = _route(hc2, aff_c, affn_c)
            x, h = _combine(tables, _experts(*routed, *expert_w), x1, g2,
                            (norm1_w[l + 1], mods[l + 1][0], mods[l + 1][1]))
            ctx = _combine(tables_c, _experts(*routed_c, *expert_w), c1, mc[5])
    return x
```

```python
import functools

import numpy as np
import jax
import jax.numpy as jnp
from jax import lax
from jax.experimental import pallas as pl
from jax.experimental.pallas import tpu as pltpu

F32 = jnp.float32
BF16 = jnp.bfloat16
I32 = jnp.int32

D_MODEL = 2048
GRID_W = 64
MIX_W = D_MODEL // 4
N_HEADS = 8
HEAD_DIM = MIX_W // N_HEADS
WIN_H = 8
WIN_W = 16
ATT_SCALE = HEAD_DIM ** -0.5
POOL_WINDOWS = (2, 4, 8, 16)
POOL_GC = MIX_W // len(POOL_WINDOWS)
FOURIER_GC = MIX_W // 4
N_EXPERTS = 16
EC_CAPACITY = 2
N_MOD = 6
EPS = 1e-6
NEG_INF = -1e30
MIX_COLS = 8 * MIX_W
LANES = 128
HALO = 16
SEL_TILE = 256
ROW_ALIGN = 16
EC_WINDOW = 64
ROWS_PER_PASS = 2
SCATTER_CHUNKS = 4
GATHER_CHUNKS = 4
VMEM_LIMIT = 56 * 1024 * 1024

COL_XA, COL_GB, COL_GC, COL_Q, COL_K, COL_V, COL_F, COL_P = range(8)


def _params(sem, vmem=VMEM_LIMIT):
    return pltpu.CompilerParams(dimension_semantics=sem, vmem_limit_bytes=vmem)


def _dot(a, b):
    return jnp.dot(a, b, preferred_element_type=F32)


def _dot_nt(a, b):
    return lax.dot_general(a, b, (((1,), (1,)), ((), ())), preferred_element_type=F32)


def _ada_kernel(c_ref, w_ref, b_ref, o_ref):
    c = c_ref[...]
    s = c * jax.nn.sigmoid(c)
    w = w_ref[0]
    s_hi, w_hi = s.astype(BF16), w.astype(BF16)
    s_lo = (s - s_hi.astype(F32)).astype(BF16)
    w_lo = (w - w_hi.astype(F32)).astype(BF16)
    o_ref[0] = _dot(s_hi, w_hi) + (_dot(s_lo, w_hi) + _dot(s_hi, w_lo)) + b_ref[0]


def _ada(c16, w_ada, b_ada):
    depth, d, ncol = w_ada.shape
    tn = 1024
    return pl.pallas_call(
        _ada_kernel,
        grid=(depth, ncol // tn),
        in_specs=[
            pl.BlockSpec((16, d), lambda l, j: (0, 0)),
            pl.BlockSpec((1, d, tn), lambda l, j: (l, 0, j)),
            pl.BlockSpec((1, 1, tn), lambda l, j: (l, 0, j)),
        ],
        out_specs=pl.BlockSpec((1, 16, tn), lambda l, j: (l, 0, j)),
        out_shape=jax.ShapeDtypeStruct((depth, 16, ncol), F32),
        compiler_params=_params(("parallel", "parallel")),
        name="ada",
    )(c16, w_ada, b_ada.reshape(depth, 1, ncol))


def _prenorm_math(x, w, shift, scale):
    ms = jnp.mean(x * x, axis=-1, keepdims=True)
    y = x * lax.rsqrt(ms + EPS) * w
    return y * (1.0 + scale) + shift


def _prenorm_kernel(x_ref, w_ref, sh_ref, sc_ref, o_ref):
    o_ref[0] = _prenorm_math(x_ref[0], w_ref[...], sh_ref[0], sc_ref[0]).astype(BF16)


def _prenorm(x, w, shift, scale):
    b, n, d = x.shape
    tm = min(1024, n)
    return pl.pallas_call(
        _prenorm_kernel,
        grid=(b, n // tm),
        in_specs=[
            pl.BlockSpec((1, tm, d), lambda i, j: (i, j, 0)),
            pl.BlockSpec((1, d), lambda i, j: (0, 0)),
            pl.BlockSpec((1, 1, d), lambda i, j: (i, 0, 0)),
            pl.BlockSpec((1, 1, d), lambda i, j: (i, 0, 0)),
        ],
        out_specs=pl.BlockSpec((1, tm, d), lambda i, j: (i, j, 0)),
        out_shape=jax.ShapeDtypeStruct((b, n, d), BF16),
        compiler_params=_params(("parallel", "parallel")),
        name="prenorm",
    )(x, w.reshape(1, d), shift, scale)


def _mm_kernel(a_ref, w_ref, o_ref, w_bf):
    @pl.when(pl.program_id(1) == 0)
    def _():
        w_bf[...] = w_ref[0].astype(BF16)

    o_ref[...] = _dot(a_ref[...], w_bf[...]).astype(o_ref.dtype)


def _in_proj(h2d, w, layer, ncols):
    m, k = h2d.shape
    tm = min(1024, m)
    tn = 1024
    return pl.pallas_call(
        _mm_kernel,
        grid=(ncols // tn, m // tm),
        in_specs=[
            pl.BlockSpec((tm, k), lambda j, i: (i, 0)),
            pl.BlockSpec((1, k, tn), lambda j, i: (layer, 0, j)),
        ],
        out_specs=pl.BlockSpec((tm, tn), lambda j, i: (i, j)),
        out_shape=jax.ShapeDtypeStruct((m, ncols), BF16),
        scratch_shapes=[pltpu.VMEM((k, tn), BF16)],
        compiler_params=_params(("parallel", "arbitrary")),
        name="in_proj",
    )(h2d, w)


def _stencil_kernel(xa_ref, gb_ref, gc_ref, zp_ref, xa_p, xa_n, gc_p, gc_n, zp_p, zp_n,
                    cw_ref, pw_ref, ps_ref, oc_ref, op_ref, *, tm, n_seq):
    i = pl.program_id(1)
    nt = pl.num_programs(1)
    rows = tm + 2 * HALO

    def ext(main_ref, prev_ref, next_ref):
        main = main_ref[0].astype(F32)
        prev = jnp.where(i > 0, prev_ref[0].astype(F32), 0.0)
        nxt = jnp.where(i < nt - 1, next_ref[0].astype(F32), 0.0)
        return jnp.concatenate([prev, main, nxt], axis=0)

    def shifted(a, d):
        if d == 0:
            return a[HALO:HALO + tm]
        return pltpu.roll(a, (-d) % rows, axis=0)[HALO:HALO + tm]

    u = ext(gc_ref, gc_p, gc_n) * ext(xa_ref, xa_p, xa_n)
    cw = cw_ref[...]
    y = cw[0:1] * shifted(u, -1) + cw[1:2] * shifted(u, 0) + cw[2:3] * shifted(u, 1)
    oc_ref[0] = (gb_ref[0].astype(F32) * y).astype(BF16)

    t = (i * tm + lax.broadcasted_iota(I32, (tm, 1), 0)).astype(F32)
    zp = ext(zp_ref, zp_p, zp_n)
    for g, win in enumerate(POOL_WINDOWS):
        ug = zp[:, g * POOL_GC:(g + 1) * POOL_GC]
        acc = ug + pltpu.roll(ug, 1, axis=0)
        half = 1
        while 2 * half < win:
            acc = pltpu.roll(acc, rows - half, axis=0) + pltpu.roll(acc, half, axis=0)
            half *= 2
        wsum = acc[HALO:HALO + tm]
        lo = jnp.maximum(t - (win // 2), 0.0)
        hi = jnp.minimum(t + (win - win // 2 - 1), float(n_seq - 1))
        pooled = wsum / (hi - lo + 1.0) - ug[HALO:HALO + tm]
        yg = _dot(pooled.astype(BF16), pw_ref[g])
        op_ref[0, :, g * POOL_GC:(g + 1) * POOL_GC] = (
            yg * ps_ref[:, g * POOL_GC:(g + 1) * POOL_GC]).astype(BF16)


def _stencil(zmix, conv_w, pool_w, pool_scale):
    b, n, _ = zmix.shape
    tm = min(512, n)
    hb = tm // HALO
    last = n // HALO - 1

    def main(col):
        return pl.BlockSpec((1, tm, MIX_W), lambda bi, i: (bi, i, col))

    def prev(col):
        return pl.BlockSpec((1, HALO, MIX_W), lambda bi, i: (bi, jnp.maximum(i * hb - 1, 0), col))

    def nxt(col):
        return pl.BlockSpec((1, HALO, MIX_W), lambda bi, i: (bi, jnp.minimum((i + 1) * hb, last), col))

    out_spec = pl.BlockSpec((1, tm, MIX_W), lambda bi, i: (bi, i, 0))
    return pl.pallas_call(
        functools.partial(_stencil_kernel, tm=tm, n_seq=n),
        grid=(b, n // tm),
        in_specs=[main(COL_XA), main(COL_GB), main(COL_GC), main(COL_P),
                  prev(COL_XA), nxt(COL_XA), prev(COL_GC), nxt(COL_GC), prev(COL_P), nxt(COL_P),
                  pl.BlockSpec((3, MIX_W), lambda bi, i: (0, 0)),
                  pl.BlockSpec((len(POOL_WINDOWS), POOL_GC, POOL_GC), lambda bi, i: (0, 0, 0)),
                  pl.BlockSpec((1, MIX_W), lambda bi, i: (0, 0))],
        out_specs=[out_spec, out_spec],
        out_shape=[jax.ShapeDtypeStruct((b, n, MIX_W), BF16)] * 2,
        compiler_params=_params(("parallel", "parallel")),
        name="stencil",
    )(zmix, zmix, zmix, zmix, zmix, zmix, zmix, zmix, zmix, zmix,
      conv_w, pool_w.astype(BF16), pool_scale.reshape(1, MIX_W))


def _fourier_kernel(cn_ref, sn_ref, u_ref, cc_ref, sc_ref, o_ref, even, odd):
    n = u_ref.shape[1]
    half, tk = n // 2, cn_ref.shape[0]
    rb = min(256, half)

    @pl.when(pl.program_id(1) == 0)
    def _():
        i = lax.broadcasted_iota(I32, (rb, rb), 0)
        j = lax.broadcasted_iota(I32, (rb, rb), 1)
        flip = jnp.where(i + j == rb, 1.0, 0.0).astype(BF16)
        row = lax.broadcasted_iota(I32, (rb, 1), 0)
        for a in range(half // rb):
            src = n - (a + 1) * rb
            rev = _dot(flip, u_ref[0, src:src + rb, :])
            first = (n - a * rb) % n
            rev = jnp.where(row == 0, u_ref[0, first:first + 1, :].astype(F32), rev)
            ua = u_ref[0, a * rb:(a + 1) * rb, :].astype(F32)
            fold = ua + rev if a else jnp.where(row == 0, ua, ua + rev)
            even[a * rb:(a + 1) * rb, :] = fold.astype(BF16)
            odd[a * rb:(a + 1) * rb, :] = (ua - rev).astype(BF16)

    k = pl.program_id(1) * tk + lax.broadcasted_iota(I32, (tk, 1), 0)
    sign = (1 - 2 * jnp.bitwise_and(k, 1)).astype(F32)
    p = (_dot(cn_ref[...], even[...]) + sign * u_ref[0, half:half + 1, :].astype(F32)).astype(BF16)
    q = _dot(sn_ref[...], odd[...]).astype(BF16)
    for g in range(MIX_W // FOURIER_GC):
        sl = slice(g * FOURIER_GC, (g + 1) * FOURIER_GC)
        o_ref[0, :, sl] = (_dot(p[:, sl], cc_ref[...]) - _dot(q[:, sl], sc_ref[...])).astype(BF16)


def _dft_mats(n, scale, ncols=None):
    ncols = n if ncols is None else ncols
    k = jnp.arange(n, dtype=I32)[:, None]

    def table(count, period):
        r = (k * jnp.arange(count, dtype=I32)[None, :]) % period
        ang = r.astype(F32) * np.float32(2.0 * np.pi / period)
        return jnp.cos(ang), jnp.sin(ang)

    inner = 64
    if n <= inner or n % inner or ncols % inner:
        c, s = table(ncols, n)
    else:
        ca, sa = table(ncols // inner, n // inner)
        cb, sb = table(inner, n)
        c = (ca[:, :, None] * cb[:, None, :] - sa[:, :, None] * sb[:, None, :]).reshape(n, ncols)
        s = (sa[:, :, None] * cb[:, None, :] + ca[:, :, None] * sb[:, None, :]).reshape(n, ncols)
    return (c * scale).astype(BF16), (s * scale).astype(BF16)


def _fourier(zmix, cn, sn, cc, sc):
    b, n, _ = zmix.shape
    half = n // 2
    tk = min(512, n)
    return pl.pallas_call(
        _fourier_kernel,
        grid=(b, n // tk),
        in_specs=[
            pl.BlockSpec((tk, half), lambda bi, k: (k, 0)),
            pl.BlockSpec((tk, half), lambda bi, k: (k, 0)),
            pl.BlockSpec((1, n, MIX_W), lambda bi, k: (bi, 0, COL_F)),
            pl.BlockSpec((FOURIER_GC, FOURIER_GC), lambda bi, k: (0, 0)),
            pl.BlockSpec((FOURIER_GC, FOURIER_GC), lambda bi, k: (0, 0)),
        ],
        out_specs=pl.BlockSpec((1, tk, MIX_W), lambda bi, k: (bi, k, 0)),
        out_shape=jax.ShapeDtypeStruct((b, n, MIX_W), BF16),
        scratch_shapes=[pltpu.VMEM((half, MIX_W), BF16), pltpu.VMEM((half, MIX_W), BF16)],
        compiler_params=_params(("parallel", "arbitrary")),
        name="fourier",
    )(cn, sn, zmix, cc, sc)


def _head_rms(x, w, bd):
    x2 = x * x
    hi = x2.astype(BF16)
    lo = (x2 - hi.astype(F32)).astype(BF16)
    ms = (_dot(hi, bd) + _dot(lo, bd)) * (1.0 / HEAD_DIM)
    return x * lax.rsqrt(ms + EPS) * w


def _pair_stack(q2):
    lane = lax.broadcasted_iota(I32, q2.shape, 1)
    zero = jnp.zeros_like(q2)
    return jnp.concatenate([jnp.where(lane < HEAD_DIM, q2, zero),
                            jnp.where(lane >= HEAD_DIM, q2, zero)], axis=0)


def _pair_unstack(o2, m):
    lane = lax.broadcasted_iota(I32, (m, LANES), 1)
    return jnp.where(lane < HEAD_DIM, o2[:m], o2[m:])


def _na_kernel(q_ref, k_ref, v_ref, kc_ref, vc_ref, qw_ref, kw_ref, bd_ref, tb_ref, o_ref,
               kn_scr, kcn_scr, qn_scr, *, rt, n_rows):
    rb = pl.program_id(1)
    n_seq = n_rows * GRID_W
    chunk = 512

    @pl.when(rb == 0)
    def _():
        def body(c, carry):
            sl = pl.ds(pl.multiple_of(c * chunk, chunk), chunk)
            kn_scr[sl, :] = _head_rms(k_ref[0, sl, :].astype(F32), kw_ref[...], bd_ref[...]).astype(BF16)
            return carry
        lax.fori_loop(0, n_seq // chunk, body, 0)
        kcn_scr[...] = _head_rms(kc_ref[0].astype(F32), kw_ref[...], bd_ref[...]).astype(BF16)

    qn_scr[...] = (_head_rms(q_ref[0].astype(F32), qw_ref[...], bd_ref[...]) * ATT_SCALE).astype(BF16)

    kh = min(WIN_H, n_rows)
    nloc = kh * GRID_W

    def row_body(jj, carry):
        units = []
        for i in range(ROWS_PER_PASS):
            j = jj * ROWS_PER_PASS + i
            r = rb * rt + j
            rs = jnp.clip(r - kh // 2, 0, n_rows - kh)
            q_rows = pl.ds(pl.multiple_of(j * GRID_W, GRID_W), GRID_W)
            k_rows = pl.ds(pl.multiple_of(rs * GRID_W, GRID_W), nloc)
            for p in range(N_HEADS // 2):
                units.append((q_rows, k_rows, rs - r + (WIN_H - 1), p, slice(p * LANES, (p + 1) * LANES)))
        qs = [_pair_stack(qn_scr[q_rows, cols]) for q_rows, _, _, _, cols in units]
        s_loc = [_dot_nt(q, kn_scr[k_rows, cols]) for q, (_, k_rows, _, _, cols) in zip(qs, units)]
        s_ctx = [_dot_nt(q, kcn_scr[:, cols]) for q, (_, _, _, _, cols) in zip(qs, units)]
        s_loc = [s + tb_ref[dr0, p] for s, (_, _, dr0, p, _) in zip(s_loc, units)]
        m = [jnp.maximum(jnp.max(a, axis=1, keepdims=True), jnp.max(c, axis=1, keepdims=True))
             for a, c in zip(s_loc, s_ctx)]
        p_loc = [jnp.exp(a - mx) for a, mx in zip(s_loc, m)]
        p_ctx = [jnp.exp(c - mx) for c, mx in zip(s_ctx, m)]
        denom = [jnp.sum(a, axis=1, keepdims=True) + jnp.sum(c, axis=1, keepdims=True) for a, c in zip(p_loc, p_ctx)]
        o2 = [_dot(a.astype(BF16), v_ref[0, k_rows, cols]) + _dot(c.astype(BF16), vc_ref[0, :, cols])
              for a, c, (_, k_rows, _, _, cols) in zip(p_loc, p_ctx, units)]
        for o, d, (q_rows, _, _, _, cols) in zip(o2, denom, units):
            o_ref[0, q_rows, cols] = _pair_unstack(o / d, GRID_W).astype(BF16)
        return carry

    lax.fori_loop(0, rt // ROWS_PER_PASS, row_body, 0)


def _na_bias_table(rpb, n_rows):
    kh = min(WIN_H, n_rows)
    c = np.arange(GRID_W)
    cs = np.clip(c - WIN_W // 2, 0, GRID_W - WIN_W)
    kc = np.arange(GRID_W)
    ok = (kc[None, :] >= cs[:, None]) & (kc[None, :] < cs[:, None] + WIN_W)
    edge = GRID_W - WIN_W
    padded = jnp.pad(rpb.astype(F32), ((0, 0), (0, 0), (edge, edge)))
    toeplitz = jnp.stack([padded[:, :, GRID_W - 1 - q:2 * GRID_W - 1 - q] for q in range(GRID_W)], axis=2)
    masked = jnp.where(ok[None, None], toeplitz, NEG_INF)
    t = jnp.stack([masked[:, d0:d0 + kh] for d0 in range(WIN_H)], axis=0)
    t = jnp.transpose(t, (0, 1, 3, 2, 4))
    return t.reshape(WIN_H, N_HEADS // 2, 2 * GRID_W, kh * GRID_W)


def _neighbourhood_attention(zmix, zc_mix, q_w, k_w, bd, tb):
    b, n, _ = zmix.shape
    l = zc_mix.shape[1]
    n_rows = n // GRID_W
    rt = 8
    nloc = min(WIN_H, n_rows) * GRID_W
    return pl.pallas_call(
        functools.partial(_na_kernel, rt=rt, n_rows=n_rows),
        grid=(b, n_rows // rt),
        in_specs=[
            pl.BlockSpec((1, rt * GRID_W, MIX_W), lambda bi, r: (bi, r, COL_Q)),
            pl.BlockSpec((1, n, MIX_W), lambda bi, r: (bi, 0, COL_K)),
            pl.BlockSpec((1, n, MIX_W), lambda bi, r: (bi, 0, COL_V)),
            pl.BlockSpec((1, l, MIX_W), lambda bi, r: (bi, 0, COL_K)),
            pl.BlockSpec((1, l, MIX_W), lambda bi, r: (bi, 0, COL_V)),
            pl.BlockSpec((1, MIX_W), lambda bi, r: (0, 0)),
            pl.BlockSpec((1, MIX_W), lambda bi, r: (0, 0)),
            pl.BlockSpec((MIX_W, MIX_W), lambda bi, r: (0, 0)),
            pl.BlockSpec((WIN_H, N_HEADS // 2, 2 * GRID_W, nloc), lambda bi, r: (0, 0, 0, 0),
                         pipeline_mode=pl.Buffered(1)),
        ],
        out_specs=pl.BlockSpec((1, rt * GRID_W, MIX_W), lambda bi, r: (bi, r, 0)),
        out_shape=jax.ShapeDtypeStruct((b, n, MIX_W), BF16),
        scratch_shapes=[pltpu.VMEM((n, MIX_W), BF16), pltpu.VMEM((l, MIX_W), BF16),
                        pltpu.VMEM((rt * GRID_W, MIX_W), BF16)],
        compiler_params=_params(("parallel", "arbitrary")),
        name="nbr_attention",
    )(zmix, zmix, zmix, zc_mix, zc_mix, q_w, k_w, bd, tb)


def _ctx_attn_kernel(q_ref, k_ref, v_ref, qw_ref, kw_ref, bd_ref, o_ref):
    l = q_ref.shape[1]
    qn = (_head_rms(q_ref[0].astype(F32), qw_ref[...], bd_ref[...]) * ATT_SCALE).astype(BF16)
    kn = _head_rms(k_ref[0].astype(F32), kw_ref[...], bd_ref[...]).astype(BF16)
    for p in range(N_HEADS // 2):
        cols = slice(p * LANES, (p + 1) * LANES)
        qs = _pair_stack(qn[:, cols])
        s = _dot_nt(qs, kn[:, cols])
        m = jnp.max(s, axis=1, keepdims=True)
        e = jnp.exp(s - m)
        o2 = _dot(e.astype(BF16), v_ref[0, :, cols]) / jnp.sum(e, axis=1, keepdims=True)
        o_ref[0, :, cols] = _pair_unstack(o2, l).astype(BF16)


def _context_attention(zc_mix, q_w, k_w, bd):
    b, l, _ = zc_mix.shape
    vec = pl.BlockSpec((1, MIX_W), lambda bi: (0, 0))
    return pl.pallas_call(
        _ctx_attn_kernel,
        grid=(b,),
        in_specs=[
            pl.BlockSpec((1, l, MIX_W), lambda bi: (bi, 0, COL_Q)),
            pl.BlockSpec((1, l, MIX_W), lambda bi: (bi, 0, COL_K)),
            pl.BlockSpec((1, l, MIX_W), lambda bi: (bi, 0, COL_V)),
            vec, vec,
            pl.BlockSpec((MIX_W, MIX_W), lambda bi: (0, 0)),
        ],
        out_specs=pl.BlockSpec((1, l, MIX_W), lambda bi: (bi, 0, 0)),
        out_shape=jax.ShapeDtypeStruct((b, l, MIX_W), BF16),
        compiler_params=_params(("parallel",)),
        name="ctx_attention",
    )(zc_mix, zc_mix, zc_mix, q_w, k_w, bd)


def _merge_kernel(h_ref, y0, y1, y2, y3, g0, g1, g2, g3, wb_ref, o_ref, wg_bf, wb_bf):
    @pl.when(pl.program_id(1) == 0)
    def _():
        for i, g in enumerate((g0, g1, g2, g3)):
            wg_bf[i] = g[0].astype(BF16)
        wb_bf[...] = wb_ref[0].astype(BF16)

    h = h_ref[...]
    acc = None
    for i, y in enumerate((y0, y1, y2, y3)):
        gate = jax.nn.sigmoid(_dot(h, wg_bf[i]))
        term = gate * _dot(y[...], wb_bf[i])
        acc = term if acc is None else acc + term
    o_ref[...] = acc.astype(BF16)


def _merge(h2d, branches, w_in, w_branch, layer):
    m, d = h2d.shape
    tm = min(1024, m)
    tn = 256
    gate0 = MIX_COLS // tn

    def gspec(i):
        return pl.BlockSpec((1, d, tn), lambda n, r: (layer, 0, gate0 + i * (d // tn) + n))

    yspec = pl.BlockSpec((tm, MIX_W), lambda n, r: (r, 0))
    return pl.pallas_call(
        _merge_kernel,
        grid=(d // tn, m // tm),
        in_specs=[pl.BlockSpec((tm, d), lambda n, r: (r, 0)), yspec, yspec, yspec, yspec,
                  gspec(0), gspec(1), gspec(2), gspec(3),
                  pl.BlockSpec((1, 4, MIX_W, tn), lambda n, r: (layer, 0, 0, n))],
        out_specs=pl.BlockSpec((tm, tn), lambda n, r: (r, n)),
        out_shape=jax.ShapeDtypeStruct((m, d), BF16),
        scratch_shapes=[pltpu.VMEM((4, d, tn), BF16), pltpu.VMEM((4, MIX_W, tn), BF16)],
        compiler_params=_params(("parallel", "arbitrary")),
        name="merge",
    )(h2d, *branches, w_in, w_in, w_in, w_in, w_branch)


def _out_proj_kernel(m_ref, w_ref, x_ref, g_ref, nw_ref, sh_ref, sc_ref, wrc_ref,
                     x1_ref, h2_ref, aff_ref, affn_ref):
    x1 = x_ref[0] + g_ref[0] * _dot(m_ref[0], w_ref[...])
    x1_ref[0] = x1
    h2 = _prenorm_math(x1, nw_ref[...], sh_ref[0], sc_ref[0])
    h_hi = h2.astype(BF16)
    h2_ref[0] = h_hi
    h_lo = (h2 - h_hi.astype(F32)).astype(BF16)
    both = _dot(h_hi, wrc_ref[...])
    logits = both[:, :LANES] + (both[:, LANES:] + _dot(h_lo, wrc_ref[:, :LANES]))
    lane = lax.broadcasted_iota(I32, logits.shape, 1)
    logits = jnp.where(lane < N_EXPERTS, logits, NEG_INF)
    e = jnp.exp(logits - jnp.max(logits, axis=1, keepdims=True))
    aff = e / jnp.sum(e, axis=1, keepdims=True)
    aff_ref[0] = aff.T[:N_EXPERTS]
    hi = aff.astype(BF16).astype(F32)
    r1 = aff - hi
    mid = r1.astype(BF16).astype(F32)
    lo = (r1 - mid).astype(BF16).astype(F32)
    packed = hi + pltpu.roll(mid, N_EXPERTS, axis=1) + pltpu.roll(lo, 2 * N_EXPERTS, axis=1)
    affn_ref[0] = packed.astype(BF16)


def _out_proj(merged, w_out_bf, x, gate, norm_w, shift, scale, wr_cat):
    b, n, d = x.shape
    tm = min(512, n)
    vec = pl.BlockSpec((1, 1, d), lambda i, j: (i, 0, 0))
    tile = pl.BlockSpec((1, tm, d), lambda i, j: (i, j, 0))
    wr_spec = pl.BlockSpec((d, 2 * LANES), lambda i, j: (0, 0))
    return pl.pallas_call(
        _out_proj_kernel,
        grid=(b, n // tm),
        in_specs=[tile, pl.BlockSpec((d, d), lambda i, j: (0, 0), pipeline_mode=pl.Buffered(1)), tile, vec,
                  pl.BlockSpec((1, d), lambda i, j: (0, 0)), vec, vec, wr_spec],
        out_specs=[tile, tile,
                   pl.BlockSpec((1, N_EXPERTS, tm), lambda i, j: (i, 0, j)),
                   pl.BlockSpec((1, tm, LANES), lambda i, j: (i, j, 0))],
        out_shape=[jax.ShapeDtypeStruct((b, n, d), F32), jax.ShapeDtypeStruct((b, n, d), BF16),
                   jax.ShapeDtypeStruct((b, N_EXPERTS, n), F32), jax.ShapeDtypeStruct((b, n, LANES), BF16)],
        compiler_params=_params(("parallel", "parallel")),
        name="out_proj",
    )(merged, w_out_bf, x, gate, norm_w.reshape(1, d), shift, scale, wr_cat)


def _lane_cumsum(mask_f, ut):
    rows, n = mask_f.shape
    carry = jnp.zeros((rows, 1), F32)
    parts = []
    for j in range(n // LANES):
        c = _dot(mask_f[:, j * LANES:(j + 1) * LANES].astype(BF16), ut) + carry
        parts.append(c)
        carry = c[:, LANES - 1:LANES]
    return jnp.concatenate(parts, axis=1)


def _select_kernel(aff_ref, slot_ref, slott_ref, *, cap):
    aff = aff_ref[0]
    e, n = aff.shape
    bits = lax.bitcast_convert_type(aff, I32)

    def body(_, carry):
        lo, hi = carry
        mid = lo + ((hi - lo) >> 1)
        cnt = jnp.sum(jnp.where(bits >= mid, 1.0, 0.0), axis=1, keepdims=True)
        ge = cnt >= float(cap)
        return jnp.where(ge, mid, lo), jnp.where(ge, hi, mid)

    lo, _ = lax.fori_loop(0, 31, body, (jnp.zeros((e, 1), I32), jnp.full((e, 1), 0x7F800000, I32)))
    ri = lax.broadcasted_iota(I32, (LANES, LANES), 0)
    ci = lax.broadcasted_iota(I32, (LANES, LANES), 1)
    ut = jnp.where(ri <= ci, 1.0, 0.0).astype(BF16)
    gt = bits > lo
    eq = bits == lo
    need = float(cap) - jnp.sum(jnp.where(gt, 1.0, 0.0), axis=1, keepdims=True)
    ceq = _lane_cumsum(jnp.where(eq, 1.0, 0.0), ut)
    sel = gt | (eq & (ceq <= need))
    csel = _lane_cumsum(jnp.where(sel, 1.0, 0.0), ut)
    slot = jnp.where(sel, csel - 1.0, -1.0)
    slot_ref[0] = slot.astype(I32)
    padded = jnp.concatenate([slot, jnp.full((LANES - e, n), -1.0, F32)], axis=0)
    slott_ref[0] = padded.T


def _select(aff, cap):
    b, e, n = aff.shape
    return pl.pallas_call(
        functools.partial(_select_kernel, cap=cap),
        grid=(b,),
        in_specs=[pl.BlockSpec((1, e, n), lambda i: (i, 0, 0))],
        out_specs=[pl.BlockSpec((1, e, n), lambda i: (i, 0, 0)),
                   pl.BlockSpec((1, n, LANES), lambda i: (i, 0, 0))],
        out_shape=[jax.ShapeDtypeStruct((b, e, n), I32), jax.ShapeDtypeStruct((b, n, LANES), F32)],
        compiler_params=_params(("parallel",)),
        name="select",
    )(aff)


def _window(cum_ref, b, e, j, r, n_tiles, cap, win):
    c_lo = cum_ref[(b * N_EXPERTS + e) * (n_tiles + 1) + j]
    first = (c_lo // ROW_ALIGN) * ROW_ALIGN + r * win
    start = pl.multiple_of(jnp.minimum(first, cap - win), ROW_ALIGN)
    return first, start


def _gather_kernel(cum_ref, rnd_ref, slot_ref, h_ref, affn_ref, xs_ref, gate_ref, *, cap, win, n_tiles):
    b = pl.program_id(0)
    dq = pl.program_id(1)
    xs_ref[...] = jnp.zeros_like(xs_ref)

    @pl.when(dq == 0)
    def _():
        gate_ref[...] = jnp.zeros_like(gate_ref)

    rows = lax.broadcasted_iota(I32, (win, SEL_TILE), 0)

    def tile_body(j, carry):
        toks = pl.ds(pl.multiple_of(j * SEL_TILE, SEL_TILE), SEL_TILE)
        slots = slot_ref[0, j]
        h_tile = h_ref[0, toks, :]

        def round_body(r, carry2):
            starts, pieces = [], []
            for e in range(N_EXPERTS):
                first, start = _window(cum_ref, b, e, j, r, n_tiles, cap, win)
                s_row = slots[e:e + 1, :]
                hit = (rows + start == s_row) & (s_row >= first)
                pieces.append(jnp.where(hit, 1.0, 0.0).astype(BF16))
                starts.append(start)
            ge = N_EXPERTS // GATHER_CHUNKS
            onehots = [jnp.concatenate(pieces[lo:lo + ge], axis=0) for lo in range(0, N_EXPERTS, ge)]
            for c, onehot in enumerate(onehots):
                res = _dot(onehot, h_tile)
                for i in range(ge):
                    e = c * ge + i
                    sl = pl.ds(starts[e], win)
                    xs_ref[e, 0, sl, :] = xs_ref[e, 0, sl, :] + res[i * win:(i + 1) * win].astype(BF16)

            @pl.when(dq == 0)
            def _():
                for c, onehot in enumerate(onehots):
                    g = _dot(onehot, affn_ref[0, toks, :])
                    for i in range(ge):
                        gate_ref[c * ge + i, 0, pl.ds(starts[c * ge + i], win), :] += g[i * win:(i + 1) * win]
            return carry2

        lax.fori_loop(0, rnd_ref[b * n_tiles + j], round_body, 0)
        return carry

    lax.fori_loop(0, n_tiles, tile_body, 0)

    @pl.when(dq == 0)
    def _():
        lane = lax.broadcasted_iota(I32, (cap, LANES), 1)
        for e in range(N_EXPERTS):
            mine = (lane == e) | (lane == N_EXPERTS + e) | (lane == 2 * N_EXPERTS + e)
            col = jnp.sum(jnp.where(mine, gate_ref[e, 0], 0.0), axis=1, keepdims=True)
            gate_ref[e, 0] = jnp.broadcast_to(col, (cap, LANES))


def _gather(cum, rounds, slot, h2, affn, cap):
    b, n, d = h2.shape
    n_tiles = n // SEL_TILE
    win = min(EC_WINDOW, cap)
    dn = 1024
    slot_tiles = slot.reshape(b, N_EXPERTS, n_tiles, SEL_TILE).transpose(0, 2, 1, 3)
    grid_spec = pltpu.PrefetchScalarGridSpec(
        num_scalar_prefetch=2,
        grid=(b, d // dn),
        in_specs=[
            pl.BlockSpec((1, n_tiles, N_EXPERTS, SEL_TILE), lambda i, q, c, r: (i, 0, 0, 0)),
            pl.BlockSpec((1, n, dn), lambda i, q, c, r: (i, 0, q), pipeline_mode=pl.Buffered(1)),
            pl.BlockSpec((1, n, LANES), lambda i, q, c, r: (i, 0, 0)),
        ],
        out_specs=[pl.BlockSpec((N_EXPERTS, 1, cap, dn), lambda i, q, c, r: (0, i, 0, q)),
                   pl.BlockSpec((N_EXPERTS, 1, cap, LANES), lambda i, q, c, r: (0, i, 0, 0))],
    )
    return pl.pallas_call(
        functools.partial(_gather_kernel, cap=cap, win=win, n_tiles=n_tiles),
        grid_spec=grid_spec,
        out_shape=[jax.ShapeDtypeStruct((N_EXPERTS, b, cap, d), BF16),
                   jax.ShapeDtypeStruct((N_EXPERTS, b, cap, LANES), F32)],
        compiler_params=_params(("parallel", "arbitrary")),
        name="ec_gather",
    )(cum, rounds, slot_tiles, h2, affn)


def _expert_up_kernel(xs_ref, wg_ref, wu_ref, h_ref, wg_bf, wu_bf):
    @pl.when(pl.program_id(2) == 0)
    def _():
        wg_bf[...] = wg_ref[0, 0].astype(BF16)
        wu_bf[...] = wu_ref[0, 0].astype(BF16)

    x = xs_ref[0]
    a = _dot(x, wg_bf[...])
    u = _dot(x, wu_bf[...])
    h_ref[0] = (a * jax.nn.sigmoid(a) * u).astype(BF16)


def _expert_down_kernel(h_ref, wd_ref, gate_ref, ys_ref, wd_bf):
    @pl.when(pl.program_id(2) == 0)
    def _():
        wd_bf[...] = wd_ref[0, 0].astype(BF16)

    y = _dot(h_ref[0], wd_bf[...])
    g = gate_ref[0]
    ys_ref[0] = (y * jnp.concatenate([g] * (y.shape[1] // LANES), axis=1)).astype(BF16)


def _experts(xs, gate, wg, wu, wd, layer):
    e, m, d = xs.shape
    ff = wg.shape[3]
    tm = min(1024, m)
    tf = 512
    tn = 1024
    sem = ("parallel", "parallel", "arbitrary")
    hidden = pl.pallas_call(
        _expert_up_kernel,
        grid=(e, ff // tf, m // tm),
        in_specs=[
            pl.BlockSpec((1, tm, d), lambda ei, f, i: (ei, i, 0)),
            pl.BlockSpec((1, 1, d, tf), lambda ei, f, i: (layer, ei, 0, f)),
            pl.BlockSpec((1, 1, d, tf), lambda ei, f, i: (layer, ei, 0, f)),
        ],
        out_specs=pl.BlockSpec((1, tm, tf), lambda ei, f, i: (ei, i, f)),
        out_shape=jax.ShapeDtypeStruct((e, m, ff), BF16),
        scratch_shapes=[pltpu.VMEM((d, tf), BF16), pltpu.VMEM((d, tf), BF16)],
        compiler_params=_params(sem),
        name="expert_up",
    )(xs, wg, wu)
    return pl.pallas_call(
        _expert_down_kernel,
        grid=(e, d // tn, m // tm),
        in_specs=[
            pl.BlockSpec((1, tm, ff), lambda ei, j, i: (ei, i, 0)),
            pl.BlockSpec((1, 1, ff, tn), lambda ei, j, i: (layer, ei, 0, j)),
            pl.BlockSpec((1, tm, LANES), lambda ei, j, i: (ei, i, 0)),
        ],
        out_specs=pl.BlockSpec((1, tm, tn), lambda ei, j, i: (ei, i, j)),
        out_shape=jax.ShapeDtypeStruct((e, m, d), BF16),
        scratch_shapes=[pltpu.VMEM((ff, tn), BF16)],
        compiler_params=_params(sem),
        name="expert_down",
    )(hidden, wd, gate)


def _scatter_kernel(cum_ref, rnd_ref, slott_ref, ys_ref, x_ref, g_ref, *rest, cap, win, n_tiles, next_norm):
    if next_norm:
        nw_ref, sh_ref, sc_ref, o_ref, hn_ref = rest
    else:
        (o_ref,) = rest
    b = pl.program_id(0)
    j = pl.program_id(1)
    st = slott_ref[0]
    per_block = LANES // win
    lane = lax.broadcasted_iota(I32, (SEL_TILE, LANES), 1)
    lane_f = lane.astype(F32)
    group = lane // win

    def round_body(r, acc):
        blocks, pieces = [], []
        for k in range(N_EXPERTS // per_block):
            target = None
            for i in range(per_block):
                e = k * per_block + i
                first, start = _window(cum_ref, b, e, j, r, n_tiles, cap, win)
                col = st[:, e:e + 1]
                tgt = jnp.where(col >= first.astype(F32), col - (start - i * win).astype(F32), -1.0)
                target = tgt if target is None else jnp.where(group == i, tgt, target)
                pieces.append(ys_ref[e, 0, pl.ds(start, win), :])
            blocks.append(jnp.where(lane_f == target, 1.0, 0.0).astype(BF16))
        nb = max(1, len(blocks) // SCATTER_CHUNKS)
        for lo in range(0, len(blocks), nb):
            onehot = jnp.concatenate(blocks[lo:lo + nb], axis=1)
            stacked = jnp.concatenate(pieces[lo * per_block:(lo + nb) * per_block], axis=0)
            part = _dot(onehot, stacked)
            acc = part if acc is None else acc + part
        return acc

    acc = lax.fori_loop(1, rnd_ref[b * n_tiles + j], round_body, round_body(0, None))
    x2 = x_ref[0] + g_ref[0] * acc
    o_ref[0] = x2
    if next_norm:
        hn_ref[0] = _prenorm_math(x2, nw_ref[...], sh_ref[0], sc_ref[0]).astype(BF16)


def _scatter(cum, rounds, slott, ys, x1, gate, cap, next_norm=None):
    b, n, d = x1.shape
    n_tiles = n // SEL_TILE
    win = min(EC_WINDOW, cap)
    tile = pl.BlockSpec((1, SEL_TILE, d), lambda i, t, c, r: (i, t, 0))
    vec = pl.BlockSpec((1, 1, d), lambda i, t, c, r: (i, 0, 0))
    in_specs = [
        pl.BlockSpec((1, SEL_TILE, LANES), lambda i, t, c, r: (i, t, 0)),
        pl.BlockSpec((N_EXPERTS, 1, cap, d), lambda i, t, c, r: (0, i, 0, 0), pipeline_mode=pl.Buffered(1)),
        tile, vec,
    ]
    args = [cum, rounds, slott, ys, x1, gate]
    out_specs, out_shape = tile, jax.ShapeDtypeStruct((b, n, d), F32)
    if next_norm is not None:
        in_specs += [pl.BlockSpec((1, d), lambda i, t, c, r: (0, 0)), vec, vec]
        args += [next_norm[0].reshape(1, d), next_norm[1], next_norm[2]]
        out_specs, out_shape = [tile, tile], [out_shape, jax.ShapeDtypeStruct((b, n, d), BF16)]
    grid_spec = pltpu.PrefetchScalarGridSpec(
        num_scalar_prefetch=2, grid=(b, n_tiles), in_specs=in_specs, out_specs=out_specs)
    return pl.pallas_call(
        functools.partial(_scatter_kernel, cap=cap, win=win, n_tiles=n_tiles, next_norm=next_norm is not None),
        grid_spec=grid_spec,
        out_shape=out_shape,
        compiler_params=_params(("parallel", "arbitrary")),
        name="ec_scatter",
    )(*args)


def _mod_chunks(mod_rows):
    b = mod_rows.shape[0]
    return [mod_rows[:, i * D_MODEL:(i + 1) * D_MODEL].reshape(b, 1, D_MODEL) for i in range(N_MOD)]


def _token_mixer(h, zmix, att, lw, dft):
    b, n, d = h.shape
    y_conv, y_pool = _stencil(zmix, lw["conv_w"], lw["pool_w"], lw["pool_scale"])
    y_f = _fourier(zmix, *dft)
    branches = [y.reshape(b * n, MIX_W) for y in (y_conv, att, y_f, y_pool)]
    merged = _merge(h.reshape(b * n, d), branches, lw["w_in"], lw["w_branch"], lw["layer"])
    return merged.reshape(b, n, d)


def _route(h2, aff, affn):
    b, n, d = h2.shape
    cap = EC_CAPACITY * n // N_EXPERTS
    slot, slott = _select(aff, cap)
    per_tile = jnp.sum((slot >= 0).reshape(b, N_EXPERTS, n // SEL_TILE, SEL_TILE), axis=-1, dtype=I32)
    cum = jnp.concatenate([jnp.zeros((b, N_EXPERTS, 1), I32), jnp.cumsum(per_tile, axis=-1)], axis=-1)
    win = min(EC_WINDOW, cap)
    span = cum[..., 1:] - (cum[..., :-1] // ROW_ALIGN) * ROW_ALIGN
    rounds = jnp.max((span + win - 1) // win, axis=1).reshape(-1)
    cum = cum.reshape(-1)
    xs, gate = _gather(cum, rounds, slot, h2, affn, cap)
    tables = (cum, rounds, slott, cap)
    return tables, (xs.reshape(N_EXPERTS, b * cap, d), gate.reshape(N_EXPERTS, b * cap, LANES))


def _combine(tables, ys, x1, g2, next_norm=None):
    cum, rounds, slott, cap = tables
    b, _, d = x1.shape
    return _scatter(cum, rounds, slott, ys.reshape(N_EXPERTS, b, cap, d), x1, g2, cap, next_norm)


def kernel(x, c, ctx, c_ctx, w_ada, b_ada, norm1_w, norm2_w, w_in, conv_w, q_norm_w, k_norm_w,
           na_rpb, pool_w, pool_scale, w_branch, w_out, w_router, w_exp_gate, w_exp_up, w_exp_down):
    bsz, n, d = x.shape
    l_ctx = ctx.shape[1]
    depth = w_ada.shape[0]
    n_rows = n // GRID_W

    c16 = jnp.zeros((16, d), F32).at[:bsz].set(c).at[bsz].set(c_ctx)
    mod_all = _ada(c16, w_ada, b_ada)

    hd = np.arange(MIX_W) // HEAD_DIM
    bd = jnp.asarray(hd[:, None] == hd[None, :], dtype=BF16)
    dft_lat = _dft_mats(n, 1.0, n // 2) + _dft_mats(FOURIER_GC, (n * FOURIER_GC) ** -0.5)
    dft_ctx = _dft_mats(l_ctx, 1.0, l_ctx // 2) + _dft_mats(FOURIER_GC, (l_ctx * FOURIER_GC) ** -0.5)

    mods = [_mod_chunks(mod_all[l, :bsz]) for l in range(depth)]
    h = _prenorm(x, norm1_w[0], mods[0][0], mods[0][1])
    for l in range(depth):
        last = l == depth - 1
        lw = {
            "w_in": w_in, "conv_w": conv_w[l], "pool_w": pool_w[l],
            "pool_scale": pool_scale[l], "w_branch": w_branch, "layer": l,
        }
        w_out_bf = w_out[l].astype(BF16)
        wr_pad = jnp.zeros((d, LANES), F32).at[:, :N_EXPERTS].set(w_router[l])
        wr_hi = wr_pad.astype(BF16)
        wr_cat = jnp.concatenate([wr_hi, (wr_pad - wr_hi.astype(F32)).astype(BF16)], axis=1)
        q_w = jnp.tile(q_norm_w[l], N_HEADS).reshape(1, MIX_W)
        k_w = jnp.tile(k_norm_w[l], N_HEADS).reshape(1, MIX_W)
        tb = _na_bias_table(na_rpb[l], n_rows)
        _, _, g1, sh2, sc2, g2 = mods[l]
        mc = _mod_chunks(jnp.broadcast_to(mod_all[l, bsz:bsz + 1], (bsz, N_MOD * d)))

        hc = _prenorm(ctx, norm1_w[l], mc[0], mc[1])
        zc_mix = _in_proj(hc.reshape(bsz * l_ctx, d), w_in, l, MIX_COLS).reshape(bsz, l_ctx, MIX_COLS)

        zmix = _in_proj(h.reshape(bsz * n, d), w_in, l, MIX_COLS).reshape(bsz, n, MIX_COLS)
        att = _neighbourhood_attention(zmix, zc_mix, q_w, k_w, bd, tb)
        merged = _token_mixer(h, zmix, att, lw, dft_lat)
        x1, h2, aff, affn = _out_proj(merged, w_out_bf, x, g1, norm2_w[l], sh2, sc2, wr_cat)
        tables, routed = _route(h2, aff, affn)
        expert_w = (w_exp_gate, w_exp_up, w_exp_down, l)
        if last:
            x = _combine(tables, _experts(*routed, *expert_w), x1, g2)
        else:
            att_c = _context_attention(zc_mix, q_w, k_w, bd)
            merged_c = _token_mixer(hc, zc_mix, att_c, lw, dft_ctx)
            c1, hc2, aff_c, affn_c = _out_proj(merged_c, w_out_bf, ctx, mc[2], norm2_w[l], mc[3], mc[4], wr_cat)
            tables_c, routed_c = _route(hc2, aff_c, affn_c)
            x, h = _combine(tables, _experts(*routed, *expert_w), x1, g2,
                            (norm1_w[l + 1], mods[l + 1][0], mods[l + 1][1]))
            ctx = _combine(tables_c, _experts(*routed_c, *expert_w), c1, mc[5])
    return x
```

```python
import functools

import numpy as np
import jax
import jax.numpy as jnp
from jax import lax
from jax.experimental import pallas as pl
from jax.experimental.pallas import tpu as pltpu

F32 = jnp.float32
BF16 = jnp.bfloat16
I32 = jnp.int32

D_MODEL = 2048
GRID_W = 64
MIX_W = D_MODEL // 4
N_HEADS = 8
HEAD_DIM = MIX_W // N_HEADS
WIN_H = 8
WIN_W = 16
ATT_SCALE = HEAD_DIM ** -0.5
POOL_WINDOWS = (2, 4, 8, 16)
POOL_GC = MIX_W // len(POOL_WINDOWS)
FOURIER_GC = MIX_W // 4
N_EXPERTS = 16
EC_CAPACITY = 2
N_MOD = 6
EPS = 1e-6
NEG_INF = -1e30
MIX_COLS = 8 * MIX_W
LANES = 128
HALO = 16
SEL_TILE = 256
ROW_ALIGN = 16
EC_WINDOW = 64
ROWS_PER_PASS = 2
SCATTER_CHUNKS = 4
GATHER_CHUNKS = 4
VMEM_LIMIT = 56 * 1024 * 1024

COL_XA, COL_GB, COL_GC, COL_Q, COL_K, COL_V, COL_F, COL_P = range(8)


def _params(sem, vmem=VMEM_LIMIT):
    return pltpu.CompilerParams(dimension_semantics=sem, vmem_limit_bytes=vmem)


def _dot(a, b):
    return jnp.dot(a, b, preferred_element_type=F32)


def _dot_nt(a, b):
    return lax.dot_general(a, b, (((1,), (1,)), ((), ())), preferred_element_type=F32)


def _ada_kernel(c_ref, w_ref, b_ref, o_ref):
    c = c_ref[...]
    s = c * jax.nn.sigmoid(c)
    w = w_ref[0]
    s_hi, w_hi = s.astype(BF16), w.astype(BF16)
    s_lo = (s - s_hi.astype(F32)).astype(BF16)
    w_lo = (w - w_hi.astype(F32)).astype(BF16)
    o_ref[0] = _dot(s_hi, w_hi) + (_dot(s_lo, w_hi) + _dot(s_hi, w_lo)) + b_ref[0]


def _ada(c16, w_ada, b_ada):
    depth, d, ncol = w_ada.shape
    tn = 1024
    return pl.pallas_call(
        _ada_kernel,
        grid=(depth, ncol // tn),
        in_specs=[
            pl.BlockSpec((16, d), lambda l, j: (0, 0)),
            pl.BlockSpec((1, d, tn), lambda l, j: (l, 0, j)),
            pl.BlockSpec((1, 1, tn), lambda l, j: (l, 0, j)),
        ],
        out_specs=pl.BlockSpec((1, 16, tn), lambda l, j: (l, 0, j)),
        out_shape=jax.ShapeDtypeStruct((depth, 16, ncol), F32),
        compiler_params=_params(("parallel", "parallel")),
        name="ada",
    )(c16, w_ada, b_ada.reshape(depth, 1, ncol))


def _prenorm_math(x, w, shift, scale):
    ms = jnp.mean(x * x, axis=-1, keepdims=True)
    y = x * lax.rsqrt(ms + EPS) * w
    return y * (1.0 + scale) + shift


def _prenorm_kernel(x_ref, w_ref, sh_ref, sc_ref, o_ref):
    o_ref[0] = _prenorm_math(x_ref[0], w_ref[...], sh_ref[0], sc_ref[0]).astype(BF16)


def _prenorm(x, w, shift, scale):
    b, n, d = x.shape
    tm = min(1024, n)
    return pl.pallas_call(
        _prenorm_kernel,
        grid=(b, n // tm),
        in_specs=[
            pl.BlockSpec((1, tm, d), lambda i, j: (i, j, 0)),
            pl.BlockSpec((1, d), lambda i, j: (0, 0)),
            pl.BlockSpec((1, 1, d), lambda i, j: (i, 0, 0)),
            pl.BlockSpec((1, 1, d), lambda i, j: (i, 0, 0)),
        ],
        out_specs=pl.BlockSpec((1, tm, d), lambda i, j: (i, j, 0)),
        out_shape=jax.ShapeDtypeStruct((b, n, d), BF16),
        compiler_params=_params(("parallel", "parallel")),
        name="prenorm",
    )(x, w.reshape(1, d), shift, scale)


def _mm_kernel(a_ref, w_ref, o_ref, w_bf):
    @pl.when(pl.program_id(1) == 0)
    def _():
        w_bf[...] = w_ref[0].astype(BF16)

    o_ref[...] = _dot(a_ref[...], w_bf[...]).astype(o_ref.dtype)


def _in_proj(h2d, w, layer, ncols):
    m, k = h2d.shape
    tm = min(1024, m)
    tn = 1024
    return pl.pallas_call(
        _mm_kernel,
        grid=(ncols // tn, m // tm),
        in_specs=[
            pl.BlockSpec((tm, k), lambda j, i: (i, 0)),
            pl.BlockSpec((1, k, tn), lambda j, i: (layer, 0, j)),
        ],
        out_specs=pl.BlockSpec((tm, tn), lambda j, i: (i, j)),
        out_shape=jax.ShapeDtypeStruct((m, ncols), BF16),
        scratch_shapes=[pltpu.VMEM((k, tn), BF16)],
        compiler_params=_params(("parallel", "arbitrary")),
        name="in_proj",
    )(h2d, w)


def _stencil_kernel(xa_ref, gb_ref, gc_ref, zp_ref, xa_p, xa_n, gc_p, gc_n, zp_p, zp_n,
                    cw_ref, pw_ref, ps_ref, oc_ref, op_ref, *, tm, n_seq):
    i = pl.program_id(1)
    nt = pl.num_programs(1)
    rows = tm + 2 * HALO

    def ext(main_ref, prev_ref, next_ref):
        main = main_ref[0].astype(F32)
        prev = jnp.where(i > 0, prev_ref[0].astype(F32), 0.0)
        nxt = jnp.where(i < nt - 1, next_ref[0].astype(F32), 0.0)
        return jnp.concatenate([prev, main, nxt], axis=0)

    def shifted(a, d):
        if d == 0:
            return a[HALO:HALO + tm]
        return pltpu.roll(a, (-d) % rows, axis=0)[HALO:HALO + tm]

    u = ext(gc_ref, gc_p, gc_n) * ext(xa_ref, xa_p, xa_n)
    cw = cw_ref[...]
    y = cw[0:1] * shifted(u, -1) + cw[1:2] * shifted(u, 0) + cw[2:3] * shifted(u, 1)
    oc_ref[0] = (gb_ref[0].astype(F32) * y).astype(BF16)

    t = (i * tm + lax.broadcasted_iota(I32, (tm, 1), 0)).astype(F32)
    zp = ext(zp_ref, zp_p, zp_n)
    for g, win in enumerate(POOL_WINDOWS):
        ug = zp[:, g * POOL_GC:(g + 1) * POOL_GC]
        acc = ug + pltpu.roll(ug, 1, axis=0)
        half = 1
        while 2 * half < win:
            acc = pltpu.roll(acc, rows - half, axis=0) + pltpu.roll(acc, half, axis=0)
            half *= 2
        wsum = acc[HALO:HALO + tm]
        lo = jnp.maximum(t - (win // 2), 0.0)
        hi = jnp.minimum(t + (win - win // 2 - 1), float(n_seq - 1))
        pooled = wsum / (hi - lo + 1.0) - ug[HALO:HALO + tm]
        yg = _dot(pooled.astype(BF16), pw_ref[g])
        op_ref[0, :, g * POOL_GC:(g + 1) * POOL_GC] = (
            yg * ps_ref[:, g * POOL_GC:(g + 1) * POOL_GC]).astype(BF16)


def _stencil(zmix, conv_w, pool_w, pool_scale):
    b, n, _ = zmix.shape
    tm = min(512, n)
    hb = tm // HALO
    last = n // HALO - 1

    def main(col):
        return pl.BlockSpec((1, tm, MIX_W), lambda bi, i: (bi, i, col))

    def prev(col):
        return pl.BlockSpec((1, HALO, MIX_W), lambda bi, i: (bi, jnp.maximum(i * hb - 1, 0), col))

    def nxt(col):
        return pl.BlockSpec((1, HALO, MIX_W), lambda bi, i: (bi, jnp.minimum((i + 1) * hb, last), col))

    out_spec = pl.BlockSpec((1, tm, MIX_W), lambda bi, i: (bi, i, 0))
    return pl.pallas_call(
        functools.partial(_stencil_kernel, tm=tm, n_seq=n),
        grid=(b, n // tm),
        in_specs=[main(COL_XA), main(COL_GB), main(COL_GC), main(COL_P),
                  prev(COL_XA), nxt(COL_XA), prev(COL_GC), nxt(COL_GC), prev(COL_P), nxt(COL_P),
                  pl.BlockSpec((3, MIX_W), lambda bi, i: (0, 0)),
                  pl.BlockSpec((len(POOL_WINDOWS), POOL_GC, POOL_GC), lambda bi, i: (0, 0, 0)),
                  pl.BlockSpec((1, MIX_W), lambda bi, i: (0, 0))],
        out_specs=[out_spec, out_spec],
        out_shape=[jax.ShapeDtypeStruct((b, n, MIX_W), BF16)] * 2,
        compiler_params=_params(("parallel", "parallel")),
        name="stencil",
    )(zmix, zmix, zmix, zmix, zmix, zmix, zmix, zmix, zmix, zmix,
      conv_w, pool_w.astype(BF16), pool_scale.reshape(1, MIX_W))


def _fourier_kernel(cn_ref, sn_ref, u_ref, cc_ref, sc_ref, o_ref, even, odd):
    n = u_ref.shape[1]
    half, tk = n // 2, cn_ref.shape[0]
    rb = min(256, half)

    @pl.when(pl.program_id(1) == 0)
    def _():
        i = lax.broadcasted_iota(I32, (rb, rb), 0)
        j = lax.broadcasted_iota(I32, (rb, rb), 1)
        flip = jnp.where(i + j == rb, 1.0, 0.0).astype(BF16)
        row = lax.broadcasted_iota(I32, (rb, 1), 0)
        for a in range(half // rb):
            src = n - (a + 1) * rb
            rev = _dot(flip, u_ref[0, src:src + rb, :])
            first = (n - a * rb) % n
            rev = jnp.where(row == 0, u_ref[0, first:first + 1, :].astype(F32), rev)
            ua = u_ref[0, a * rb:(a + 1) * rb, :].astype(F32)
            fold = ua + rev if a else jnp.where(row == 0, ua, ua + rev)
            even[a * rb:(a + 1) * rb, :] = fold.astype(BF16)
            odd[a * rb:(a + 1) * rb, :] = (ua - rev).astype(BF16)

    k = pl.program_id(1) * tk + lax.broadcasted_iota(I32, (tk, 1), 0)
    sign = (1 - 2 * jnp.bitwise_and(k, 1)).astype(F32)
    p = (_dot(cn_ref[...], even[...]) + sign * u_ref[0, half:half + 1, :].astype(F32)).astype(BF16)
    q = _dot(sn_ref[...], odd[...]).astype(BF16)
    for g in range(MIX_W // FOURIER_GC):
        sl = slice(g * FOURIER_GC, (g + 1) * FOURIER_GC)
        o_ref[0, :, sl] = (_dot(p[:, sl], cc_ref[...]) - _dot(q[:, sl], sc_ref[...])).astype(BF16)


def _dft_mats(n, scale, ncols=None):
    ncols = n if ncols is None else ncols
    k = jnp.arange(n, dtype=I32)[:, None]

    def table(count, period):
        r = (k * jnp.arange(count, dtype=I32)[None, :]) % period
        ang = r.astype(F32) * np.float32(2.0 * np.pi / period)
        return jnp.cos(ang), jnp.sin(ang)

    inner = 64
    if n <= inner or n % inner or ncols % inner:
        c, s = table(ncols, n)
    else:
        ca, sa = table(ncols // inner, n // inner)
        cb, sb = table(inner, n)
        c = (ca[:, :, None] * cb[:, None, :] - sa[:, :, None] * sb[:, None, :]).reshape(n, ncols)
        s = (sa[:, :, None] * cb[:, None, :] + ca[:, :, None] * sb[:, None, :]).reshape(n, ncols)
    return (c * scale).astype(BF16), (s * scale).astype(BF16)


def _fourier(zmix, cn, sn, cc, sc):
    b, n, _ = zmix.shape
    half = n // 2
    tk = min(512, n)
    return pl.pallas_call(
        _fourier_kernel,
        grid=(b, n // tk),
        in_specs=[
            pl.BlockSpec((tk, half), lambda bi, k: (k, 0)),
            pl.BlockSpec((tk, half), lambda bi, k: (k, 0)),
            pl.BlockSpec((1, n, MIX_W), lambda bi, k: (bi, 0, COL_F)),
            pl.BlockSpec((FOURIER_GC, FOURIER_GC), lambda bi, k: (0, 0)),
            pl.BlockSpec((FOURIER_GC, FOURIER_GC), lambda bi, k: (0, 0)),
        ],
        out_specs=pl.BlockSpec((1, tk, MIX_W), lambda bi, k: (bi, k, 0)),
        out_shape=jax.ShapeDtypeStruct((b, n, MIX_W), BF16),
        scratch_shapes=[pltpu.VMEM((half, MIX_W), BF16), pltpu.VMEM((half, MIX_W), BF16)],
        compiler_params=_params(("parallel", "arbitrary")),
        name="fourier",
    )(cn, sn, zmix, cc, sc)


def _head_rms(x, w, bd):
    x2 = x * x
    hi = x2.astype(BF16)
    lo = (x2 - hi.astype(F32)).astype(BF16)
    ms = (_dot(hi, bd) + _dot(lo, bd)) * (1.0 / HEAD_DIM)
    return x * lax.rsqrt(ms + EPS) * w


def _pair_stack(q2):
    lane = lax.broadcasted_iota(I32, q2.shape, 1)
    zero = jnp.zeros_like(q2)
    return jnp.concatenate([jnp.where(lane < HEAD_DIM, q2, zero),
                            jnp.where(lane >= HEAD_DIM, q2, zero)], axis=0)


def _pair_unstack(o2, m):
    lane = lax.broadcasted_iota(I32, (m, LANES), 1)
    return jnp.where(lane < HEAD_DIM, o2[:m], o2[m:])


def _na_kernel(q_ref, k_ref, v_ref, kc_ref, vc_ref, qw_ref, kw_ref, bd_ref, tb_ref, o_ref,
               kn_scr, kcn_scr, qn_scr, *, rt, n_rows):
    rb = pl.program_id(1)
    n_seq = n_rows * GRID_W
    chunk = 512

    @pl.when(rb == 0)
    def _():
        def body(c, carry):
            sl = pl.ds(pl.multiple_of(c * chunk, chunk), chunk)
            kn_scr[sl, :] = _head_rms(k_ref[0, sl, :].astype(F32), kw_ref[...], bd_ref[...]).astype(BF16)
            return carry
        lax.fori_loop(0, n_seq // chunk, body, 0)
        kcn_scr[...] = _head_rms(kc_ref[0].astype(F32), kw_ref[...], bd_ref[...]).astype(BF16)

    qn_scr[...] = (_head_rms(q_ref[0].astype(F32), qw_ref[...], bd_ref[...]) * ATT_SCALE).astype(BF16)

    kh = min(WIN_H, n_rows)
    nloc = kh * GRID_W

    def row_body(jj, carry):
        units = []
        for i in range(ROWS_PER_PASS):
            j = jj * ROWS_PER_PASS + i
            r = rb * rt + j
            rs = jnp.clip(r - kh // 2, 0, n_rows - kh)
            q_rows = pl.ds(pl.multiple_of(j * GRID_W, GRID_W), GRID_W)
            k_rows = pl.ds(pl.multiple_of(rs * GRID_W, GRID_W), nloc)
            for p in range(N_HEADS // 2):
                units.append((q_rows, k_rows, rs - r + (WIN_H - 1), p, slice(p * LANES, (p + 1) * LANES)))
        qs = [_pair_stack(qn_scr[q_rows, cols]) for q_rows, _, _, _, cols in units]
        s_loc = [_dot_nt(q, kn_scr[k_rows, cols]) for q, (_, k_rows, _, _, cols) in zip(qs, units)]
        s_ctx = [_dot_nt(q, kcn_scr[:, cols]) for q, (_, _, _, _, cols) in zip(qs, units)]
        s_loc = [s + tb_ref[dr0, p] for s, (_, _, dr0, p, _) in zip(s_loc, units)]
        m = [jnp.maximum(jnp.max(a, axis=1, keepdims=True), jnp.max(c, axis=1, keepdims=True))
             for a, c in zip(s_loc, s_ctx)]
        p_loc = [jnp.exp(a - mx) for a, mx in zip(s_loc, m)]
        p_ctx = [jnp.exp(c - mx) for c, mx in zip(s_ctx, m)]
        denom = [jnp.sum(a, axis=1, keepdims=True) + jnp.sum(c, axis=1, keepdims=True) for a, c in zip(p_loc, p_ctx)]
        o2 = [_dot(a.astype(BF16), v_ref[0, k_rows, cols]) + _dot(c.astype(BF16), vc_ref[0, :, cols])
              for a, c, (_, k_rows, _, _, cols) in zip(p_loc, p_ctx, units)]
        for o, d, (q_rows, _, _, _, cols) in zip(o2, denom, units):
            o_ref[0, q_rows, cols] = _pair_unstack(o / d, GRID_W).astype(BF16)
        return carry

    lax.fori_loop(0, rt // ROWS_PER_PASS, row_body, 0)


def _na_bias_table(rpb, n_rows):
    kh = min(WIN_H, n_rows)
    c = np.arange(GRID_W)
    cs = np.clip(c - WIN_W // 2, 0, GRID_W - WIN_W)
    kc = np.arange(GRID_W)
    ok = (kc[None, :] >= cs[:, None]) & (kc[None, :] < cs[:, None] + WIN_W)
    edge = GRID_W - WIN_W
    padded = jnp.pad(rpb.astype(F32), ((0, 0), (0, 0), (edge, edge)))
    toeplitz = jnp.stack([padded[:, :, GRID_W - 1 - q:2 * GRID_W - 1 - q] for q in range(GRID_W)], axis=2)
    masked = jnp.where(ok[None, None], toeplitz, NEG_INF)
    t = jnp.stack([masked[:, d0:d0 + kh] for d0 in range(WIN_H)], axis=0)
    t = jnp.transpose(t, (0, 1, 3, 2, 4))
    return t.reshape(WIN_H, N_HEADS // 2, 2 * GRID_W, kh * GRID_W)


def _neighbourhood_attention(zmix, zc_mix, q_w, k_w, bd, tb):
    b, n, _ = zmix.shape
    l = zc_mix.shape[1]
    n_rows = n // GRID_W
    rt = 16
    nloc = min(WIN_H, n_rows) * GRID_W
    return pl.pallas_call(
        functools.partial(_na_kernel, rt=rt, n_rows=n_rows),
        grid=(b, n_rows // rt),
        in_specs=[
            pl.BlockSpec((1, rt * GRID_W, MIX_W), lambda bi, r: (bi, r, COL_Q)),
            pl.BlockSpec((1, n, MIX_W), lambda bi, r: (bi, 0, COL_K)),
            pl.BlockSpec((1, n, MIX_W), lambda bi, r: (bi, 0, COL_V)),
            pl.BlockSpec((1, l, MIX_W), lambda bi, r: (bi, 0, COL_K)),
            pl.BlockSpec((1, l, MIX_W), lambda bi, r: (bi, 0, COL_V)),
            pl.BlockSpec((1, MIX_W), lambda bi, r: (0, 0)),
            pl.BlockSpec((1, MIX_W), lambda bi, r: (0, 0)),
            pl.BlockSpec((MIX_W, MIX_W), lambda bi, r: (0, 0)),
            pl.BlockSpec((WIN_H, N_HEADS // 2, 2 * GRID_W, nloc), lambda bi, r: (0, 0, 0, 0),
                         pipeline_mode=pl.Buffered(1)),
        ],
        out_specs=pl.BlockSpec((1, rt * GRID_W, MIX_W), lambda bi, r: (bi, r, 0)),
        out_shape=jax.ShapeDtypeStruct((b, n, MIX_W), BF16),
        scratch_shapes=[pltpu.VMEM((n, MIX_W), BF16), pltpu.VMEM((l, MIX_W), BF16),
                        pltpu.VMEM((rt * GRID_W, MIX_W), BF16)],
        compiler_params=_params(("parallel", "arbitrary")),
        name="nbr_attention",
    )(zmix, zmix, zmix, zc_mix, zc_mix, q_w, k_w, bd, tb)


def _ctx_attn_kernel(q_ref, k_ref, v_ref, qw_ref, kw_ref, bd_ref, o_ref):
    l = q_ref.shape[1]
    qn = (_head_rms(q_ref[0].astype(F32), qw_ref[...], bd_ref[...]) * ATT_SCALE).astype(BF16)
    kn = _head_rms(k_ref[0].astype(F32), kw_ref[...], bd_ref[...]).astype(BF16)
    for p in range(N_HEADS // 2):
        cols = slice(p * LANES, (p + 1) * LANES)
        qs = _pair_stack(qn[:, cols])
        s = _dot_nt(qs, kn[:, cols])
        m = jnp.max(s, axis=1, keepdims=True)
        e = jnp.exp(s - m)
        o2 = _dot(e.astype(BF16), v_ref[0, :, cols]) / jnp.sum(e, axis=1, keepdims=True)
        o_ref[0, :, cols] = _pair_unstack(o2, l).astype(BF16)


def _context_attention(zc_mix, q_w, k_w, bd):
    b, l, _ = zc_mix.shape
    vec = pl.BlockSpec((1, MIX_W), lambda bi: (0, 0))
    return pl.pallas_call(
        _ctx_attn_kernel,
        grid=(b,),
        in_specs=[
            pl.BlockSpec((1, l, MIX_W), lambda bi: (bi, 0, COL_Q)),
            pl.BlockSpec((1, l, MIX_W), lambda bi: (bi, 0, COL_K)),
            pl.BlockSpec((1, l, MIX_W), lambda bi: (bi, 0, COL_V)),
            vec, vec,
            pl.BlockSpec((MIX_W, MIX_W), lambda bi: (0, 0)),
        ],
        out_specs=pl.BlockSpec((1, l, MIX_W), lambda bi: (bi, 0, 0)),
        out_shape=jax.ShapeDtypeStruct((b, l, MIX_W), BF16),
        compiler_params=_params(("parallel",)),
        name="ctx_attention",
    )(zc_mix, zc_mix, zc_mix, q_w, k_w, bd)


def _merge_kernel(h_ref, y0, y1, y2, y3, g0, g1, g2, g3, wb_ref, o_ref, wg_bf, wb_bf):
    @pl.when(pl.program_id(1) == 0)
    def _():
        for i, g in enumerate((g0, g1, g2, g3)):
            wg_bf[i] = g[0].astype(BF16)
        wb_bf[...] = wb_ref[0].astype(BF16)

    h = h_ref[...]
    acc = None
    for i, y in enumerate((y0, y1, y2, y3)):
        gate = jax.nn.sigmoid(_dot(h, wg_bf[i]))
        term = gate * _dot(y[...], wb_bf[i])
        acc = term if acc is None else acc + term
    o_ref[...] = acc.astype(BF16)


def _merge(h2d, branches, w_in, w_branch, layer):
    m, d = h2d.shape
    tm = min(1024, m)
    tn = 256
    gate0 = MIX_COLS // tn

    def gspec(i):
        return pl.BlockSpec((1, d, tn), lambda n, r: (layer, 0, gate0 + i * (d // tn) + n))

    yspec = pl.BlockSpec((tm, MIX_W), lambda n, r: (r, 0))
    return pl.pallas_call(
        _merge_kernel,
        grid=(d // tn, m // tm),
        in_specs=[pl.BlockSpec((tm, d), lambda n, r: (r, 0)), yspec, yspec, yspec, yspec,
                  gspec(0), gspec(1), gspec(2), gspec(3),
                  pl.BlockSpec((1, 4, MIX_W, tn), lambda n, r: (layer, 0, 0, n))],
        out_specs=pl.BlockSpec((tm, tn), lambda n, r: (r, n)),
        out_shape=jax.ShapeDtypeStruct((m, d), BF16),
        scratch_shapes=[pltpu.VMEM((4, d, tn), BF16), pltpu.VMEM((4, MIX_W, tn), BF16)],
        compiler_params=_params(("parallel", "arbitrary")),
        name="merge",
    )(h2d, *branches, w_in, w_in, w_in, w_in, w_branch)


def _out_proj_kernel(m_ref, w_ref, x_ref, g_ref, nw_ref, sh_ref, sc_ref, wrc_ref,
                     x1_ref, h2_ref, aff_ref, affn_ref):
    x1 = x_ref[0] + g_ref[0] * _dot(m_ref[0], w_ref[...])
    x1_ref[0] = x1
    h2 = _prenorm_math(x1, nw_ref[...], sh_ref[0], sc_ref[0])
    h_hi = h2.astype(BF16)
    h2_ref[0] = h_hi
    h_lo = (h2 - h_hi.astype(F32)).astype(BF16)
    both = _dot(h_hi, wrc_ref[...])
    logits = both[:, :LANES] + (both[:, LANES:] + _dot(h_lo, wrc_ref[:, :LANES]))
    lane = lax.broadcasted_iota(I32, logits.shape, 1)
    logits = jnp.where(lane < N_EXPERTS, logits, NEG_INF)
    e = jnp.exp(logits - jnp.max(logits, axis=1, keepdims=True))
    aff = e / jnp.sum(e, axis=1, keepdims=True)
    aff_ref[0] = aff.T[:N_EXPERTS]
    hi = aff.astype(BF16).astype(F32)
    r1 = aff - hi
    mid = r1.astype(BF16).astype(F32)
    lo = (r1 - mid).astype(BF16).astype(F32)
    packed = hi + pltpu.roll(mid, N_EXPERTS, axis=1) + pltpu.roll(lo, 2 * N_EXPERTS, axis=1)
    affn_ref[0] = packed.astype(BF16)


def _out_proj(merged, w_out_bf, x, gate, norm_w, shift, scale, wr_cat):
    b, n, d = x.shape
    tm = min(512, n)
    vec = pl.BlockSpec((1, 1, d), lambda i, j: (i, 0, 0))
    tile = pl.BlockSpec((1, tm, d), lambda i, j: (i, j, 0))
    wr_spec = pl.BlockSpec((d, 2 * LANES), lambda i, j: (0, 0))
    return pl.pallas_call(
        _out_proj_kernel,
        grid=(b, n // tm),
        in_specs=[tile, pl.BlockSpec((d, d), lambda i, j: (0, 0), pipeline_mode=pl.Buffered(1)), tile, vec,
                  pl.BlockSpec((1, d), lambda i, j: (0, 0)), vec, vec, wr_spec],
        out_specs=[tile, tile,
                   pl.BlockSpec((1, N_EXPERTS, tm), lambda i, j: (i, 0, j)),
                   pl.BlockSpec((1, tm, LANES), lambda i, j: (i, j, 0))],
        out_shape=[jax.ShapeDtypeStruct((b, n, d), F32), jax.ShapeDtypeStruct((b, n, d), BF16),
                   jax.ShapeDtypeStruct((b, N_EXPERTS, n), F32), jax.ShapeDtypeStruct((b, n, LANES), BF16)],
        compiler_params=_params(("parallel", "parallel")),
        name="out_proj",
    )(merged, w_out_bf, x, gate, norm_w.reshape(1, d), shift, scale, wr_cat)


def _lane_cumsum(mask_f, ut):
    rows, n = mask_f.shape
    carry = jnp.zeros((rows, 1), F32)
    parts = []
    for j in range(n // LANES):
        c = _dot(mask_f[:, j * LANES:(j + 1) * LANES].astype(BF16), ut) + carry
        parts.append(c)
        carry = c[:, LANES - 1:LANES]
    return jnp.concatenate(parts, axis=1)


def _select_kernel(aff_ref, slot_ref, slott_ref, *, cap):
    aff = aff_ref[0]
    e, n = aff.shape
    bits = lax.bitcast_convert_type(aff, I32)

    def body(_, carry):
        lo, hi = carry
        mid = lo + ((hi - lo) >> 1)
        cnt = jnp.sum(jnp.where(bits >= mid, 1.0, 0.0), axis=1, keepdims=True)
        ge = cnt >= float(cap)
        return jnp.where(ge, mid, lo), jnp.where(ge, hi, mid)

    lo, _ = lax.fori_loop(0, 31, body, (jnp.zeros((e, 1), I32), jnp.full((e, 1), 0x7F800000, I32)))
    ri = lax.broadcasted_iota(I32, (LANES, LANES), 0)
    ci = lax.broadcasted_iota(I32, (LANES, LANES), 1)
    ut = jnp.where(ri <= ci, 1.0, 0.0).astype(BF16)
    gt = bits > lo
    eq = bits == lo
    need = float(cap) - jnp.sum(jnp.where(gt, 1.0, 0.0), axis=1, keepdims=True)
    ceq = _lane_cumsum(jnp.where(eq, 1.0, 0.0), ut)
    sel = gt | (eq & (ceq <= need))
    csel = _lane_cumsum(jnp.where(sel, 1.0, 0.0), ut)
    slot = jnp.where(sel, csel - 1.0, -1.0)
    slot_ref[0] = slot.astype(I32)
    padded = jnp.concatenate([slot, jnp.full((LANES - e, n), -1.0, F32)], axis=0)
    slott_ref[0] = padded.T


def _select(aff, cap):
    b, e, n = aff.shape
    return pl.pallas_call(
        functools.partial(_select_kernel, cap=cap),
        grid=(b,),
        in_specs=[pl.BlockSpec((1, e, n), lambda i: (i, 0, 0))],
        out_specs=[pl.BlockSpec((1, e, n), lambda i: (i, 0, 0)),
                   pl.BlockSpec((1, n, LANES), lambda i: (i, 0, 0))],
        out_shape=[jax.ShapeDtypeStruct((b, e, n), I32), jax.ShapeDtypeStruct((b, n, LANES), F32)],
        compiler_params=_params(("parallel",)),
        name="select",
    )(aff)


def _window(cum_ref, b, e, j, r, n_tiles, cap, win):
    c_lo = cum_ref[(b * N_EXPERTS + e) * (n_tiles + 1) + j]
    first = (c_lo // ROW_ALIGN) * ROW_ALIGN + r * win
    start = pl.multiple_of(jnp.minimum(first, cap - win), ROW_ALIGN)
    return first, start


def _gather_kernel(cum_ref, rnd_ref, slot_ref, h_ref, affn_ref, xs_ref, gate_ref, *, cap, win, n_tiles):
    b = pl.program_id(0)
    dq = pl.program_id(1)
    xs_ref[...] = jnp.zeros_like(xs_ref)

    @pl.when(dq == 0)
    def _():
        gate_ref[...] = jnp.zeros_like(gate_ref)

    rows = lax.broadcasted_iota(I32, (win, SEL_TILE), 0)

    def tile_body(j, carry):
        toks = pl.ds(pl.multiple_of(j * SEL_TILE, SEL_TILE), SEL_TILE)
        slots = slot_ref[0, j]
        h_tile = h_ref[0, toks, :]

        def round_body(r, carry2):
            starts, pieces = [], []
            for e in range(N_EXPERTS):
                first, start = _window(cum_ref, b, e, j, r, n_tiles, cap, win)
                s_row = slots[e:e + 1, :]
                hit = (rows + start == s_row) & (s_row >= first)
                pieces.append(jnp.where(hit, 1.0, 0.0).astype(BF16))
                starts.append(start)
            ge = N_EXPERTS // GATHER_CHUNKS
            onehots = [jnp.concatenate(pieces[lo:lo + ge], axis=0) for lo in range(0, N_EXPERTS, ge)]
            for c, onehot in enumerate(onehots):
                res = _dot(onehot, h_tile)
                for i in range(ge):
                    e = c * ge + i
                    sl = pl.ds(starts[e], win)
                    xs_ref[e, 0, sl, :] = xs_ref[e, 0, sl, :] + res[i * win:(i + 1) * win].astype(BF16)

            @pl.when(dq == 0)
            def _():
                for c, onehot in enumerate(onehots):
                    g = _dot(onehot, affn_ref[0, toks, :])
                    for i in range(ge):
                        gate_ref[c * ge + i, 0, pl.ds(starts[c * ge + i], win), :] += g[i * win:(i + 1) * win]
            return carry2

        lax.fori_loop(0, rnd_ref[b * n_tiles + j], round_body, 0)
        return carry

    lax.fori_loop(0, n_tiles, tile_body, 0)

    @pl.when(dq == 0)
    def _():
        lane = lax.broadcasted_iota(I32, (cap, LANES), 1)
        for e in range(N_EXPERTS):
            mine = (lane == e) | (lane == N_EXPERTS + e) | (lane == 2 * N_EXPERTS + e)
            col = jnp.sum(jnp.where(mine, gate_ref[e, 0], 0.0), axis=1, keepdims=True)
            gate_ref[e, 0] = jnp.broadcast_to(col, (cap, LANES))


def _gather(cum, rounds, slot, h2, affn, cap):
    b, n, d = h2.shape
    n_tiles = n // SEL_TILE
    win = min(EC_WINDOW, cap)
    dn = 512
    slot_tiles = slot.reshape(b, N_EXPERTS, n_tiles, SEL_TILE).transpose(0, 2, 1, 3)
    grid_spec = pltpu.PrefetchScalarGridSpec(
        num_scalar_prefetch=2,
        grid=(b, d // dn),
        in_specs=[
            pl.BlockSpec((1, n_tiles, N_EXPERTS, SEL_TILE), lambda i, q, c, r: (i, 0, 0, 0)),
            pl.BlockSpec((1, n, dn), lambda i, q, c, r: (i, 0, q)),
            pl.BlockSpec((1, n, LANES), lambda i, q, c, r: (i, 0, 0)),
        ],
        out_specs=[pl.BlockSpec((N_EXPERTS, 1, cap, dn), lambda i, q, c, r: (0, i, 0, q)),
                   pl.BlockSpec((N_EXPERTS, 1, cap, LANES), lambda i, q, c, r: (0, i, 0, 0))],
    )
    return pl.pallas_call(
        functools.partial(_gather_kernel, cap=cap, win=win, n_tiles=n_tiles),
        grid_spec=grid_spec,
        out_shape=[jax.ShapeDtypeStruct((N_EXPERTS, b, cap, d), BF16),
                   jax.ShapeDtypeStruct((N_EXPERTS, b, cap, LANES), F32)],
        compiler_params=_params(("parallel", "arbitrary")),
        name="ec_gather",
    )(cum, rounds, slot_tiles, h2, affn)


def _expert_up_kernel(xs_ref, wg_ref, wu_ref, h_ref, wg_bf, wu_bf):
    @pl.when(pl.program_id(2) == 0)
    def _():
        wg_bf[...] = wg_ref[0, 0].astype(BF16)
        wu_bf[...] = wu_ref[0, 0].astype(BF16)

    x = xs_ref[0]
    a = _dot(x, wg_bf[...])
    u = _dot(x, wu_bf[...])
    h_ref[0] = (a * jax.nn.sigmoid(a) * u).astype(BF16)


def _expert_down_kernel(h_ref, wd_ref, gate_ref, ys_ref, wd_bf):
    @pl.when(pl.program_id(2) == 0)
    def _():
        wd_bf[...] = wd_ref[0, 0].astype(BF16)

    y = _dot(h_ref[0], wd_bf[...])
    g = gate_ref[0]
    ys_ref[0] = (y * jnp.concatenate([g] * (y.shape[1] // LANES), axis=1)).astype(BF16)


def _experts(xs, gate, wg, wu, wd, layer):
    e, m, d = xs.shape
    ff = wg.shape[3]
    tm = min(1024, m)
    tf = 512
    tn = 1024
    sem = ("parallel", "parallel", "arbitrary")
    hidden = pl.pallas_call(
        _expert_up_kernel,
        grid=(e, ff // tf, m // tm),
        in_specs=[
            pl.BlockSpec((1, tm, d), lambda ei, f, i: (ei, i, 0)),
            pl.BlockSpec((1, 1, d, tf), lambda ei, f, i: (layer, ei, 0, f)),
            pl.BlockSpec((1, 1, d, tf), lambda ei, f, i: (layer, ei, 0, f)),
        ],
        out_specs=pl.BlockSpec((1, tm, tf), lambda ei, f, i: (ei, i, f)),
        out_shape=jax.ShapeDtypeStruct((e, m, ff), BF16),
        scratch_shapes=[pltpu.VMEM((d, tf), BF16), pltpu.VMEM((d, tf), BF16)],
        compiler_params=_params(sem),
        name="expert_up",
    )(xs, wg, wu)
    return pl.pallas_call(
        _expert_down_kernel,
        grid=(e, d // tn, m // tm),
        in_specs=[
            pl.BlockSpec((1, tm, ff), lambda ei, j, i: (ei, i, 0)),
            pl.BlockSpec((1, 1, ff, tn), lambda ei, j, i: (layer, ei, 0, j)),
            pl.BlockSpec((1, tm, LANES), lambda ei, j, i: (ei, i, 0)),
        ],
        out_specs=pl.BlockSpec((1, tm, tn), lambda ei, j, i: (ei, i, j)),
        out_shape=jax.ShapeDtypeStruct((e, m, d), BF16),
        scratch_shapes=[pltpu.VMEM((ff, tn), BF16)],
        compiler_params=_params(sem),
        name="expert_down",
    )(hidden, wd, gate)


def _scatter_kernel(cum_ref, rnd_ref, slott_ref, ys_ref, x_ref, g_ref, *rest, cap, win, n_tiles, next_norm):
    if next_norm:
        nw_ref, sh_ref, sc_ref, o_ref, hn_ref = rest
    else:
        (o_ref,) = rest
    b = pl.program_id(0)
    j = pl.program_id(1)
    st = slott_ref[0]
    per_block = LANES // win
    lane = lax.broadcasted_iota(I32, (SEL_TILE, LANES), 1)
    lane_f = lane.astype(F32)
    group = lane // win

    def round_body(r, acc):
        blocks, pieces = [], []
        for k in range(N_EXPERTS // per_block):
            target = None
            for i in range(per_block):
                e = k * per_block + i
                first, start = _window(cum_ref, b, e, j, r, n_tiles, cap, win)
                col = st[:, e:e + 1]
                tgt = jnp.where(col >= first.astype(F32), col - (start - i * win).astype(F32), -1.0)
                target = tgt if target is None else jnp.where(group == i, tgt, target)
                pieces.append(ys_ref[e, 0, pl.ds(start, win), :])
            blocks.append(jnp.where(lane_f == target, 1.0, 0.0).astype(BF16))
        nb = max(1, len(blocks) // SCATTER_CHUNKS)
        for lo in range(0, len(blocks), nb):
            onehot = jnp.concatenate(blocks[lo:lo + nb], axis=1)
            stacked = jnp.concatenate(pieces[lo * per_block:(lo + nb) * per_block], axis=0)
            part = _dot(onehot, stacked)
            acc = part if acc is None else acc + part
        return acc

    acc = lax.fori_loop(1, rnd_ref[b * n_tiles + j], round_body, round_body(0, None))
    x2 = x_ref[0] + g_ref[0] * acc
    o_ref[0] = x2
    if next_norm:
        hn_ref[0] = _prenorm_math(x2, nw_ref[...], sh_ref[0], sc_ref[0]).astype(BF16)


def _scatter(cum, rounds, slott, ys, x1, gate, cap, next_norm=None):
    b, n, d = x1.shape
    n_tiles = n // SEL_TILE
    win = min(EC_WINDOW, cap)
    tile = pl.BlockSpec((1, SEL_TILE, d), lambda i, t, c, r: (i, t, 0))
    vec = pl.BlockSpec((1, 1, d), lambda i, t, c, r: (i, 0, 0))
    in_specs = [
        pl.BlockSpec((1, SEL_TILE, LANES), lambda i, t, c, r: (i, t, 0)),
        pl.BlockSpec((N_EXPERTS, 1, cap, d), lambda i, t, c, r: (0, i, 0, 0), pipeline_mode=pl.Buffered(1)),
        tile, vec,
    ]
    args = [cum, rounds, slott, ys, x1, gate]
    out_specs, out_shape = tile, jax.ShapeDtypeStruct((b, n, d), F32)
    if next_norm is not None:
        in_specs += [pl.BlockSpec((1, d), lambda i, t, c, r: (0, 0)), vec, vec]
        args += [next_norm[0].reshape(1, d), next_norm[1], next_norm[2]]
        out_specs, out_shape = [tile, tile], [out_shape, jax.ShapeDtypeStruct((b, n, d), BF16)]
    grid_spec = pltpu.PrefetchScalarGridSpec(
        num_scalar_prefetch=2, grid=(b, n_tiles), in_specs=in_specs, out_specs=out_specs)
    return pl.pallas_call(
        functools.partial(_scatter_kernel, cap=cap, win=win, n_tiles=n_tiles, next_norm=next_norm is not None),
        grid_spec=grid_spec,
        out_shape=out_shape,
        compiler_params=_params(("parallel", "arbitrary")),
        name="ec_scatter",
    )(*args)


def _mod_chunks(mod_rows):
    b = mod_rows.shape[0]
    return [mod_rows[:, i * D_MODEL:(i + 1) * D_MODEL].reshape(b, 1, D_MODEL) for i in range(N_MOD)]


def _token_mixer(h, zmix, att, lw, dft):
    b, n, d = h.shape
    y_conv, y_pool = _stencil(zmix, lw["conv_w"], lw["pool_w"], lw["pool_scale"])
    y_f = _fourier(zmix, *dft)
    branches = [y.reshape(b * n, MIX_W) for y in (y_conv, att, y_f, y_pool)]
    merged = _merge(h.reshape(b * n, d), branches, lw["w_in"], lw["w_branch"], lw["layer"])
    return merged.reshape(b, n, d)


def _route(h2, aff, affn):
    b, n, d = h2.shape
    cap = EC_CAPACITY * n // N_EXPERTS
    slot, slott = _select(aff, cap)
    per_tile = jnp.sum((slot >= 0).reshape(b, N_EXPERTS, n // SEL_TILE, SEL_TILE), axis=-1, dtype=I32)
    cum = jnp.concatenate([jnp.zeros((b, N_EXPERTS, 1), I32), jnp.cumsum(per_tile, axis=-1)], axis=-1)
    win = min(EC_WINDOW, cap)
    span = cum[..., 1:] - (cum[..., :-1] // ROW_ALIGN) * ROW_ALIGN
    rounds = jnp.max((span + win - 1) // win, axis=1).reshape(-1)
    cum = cum.reshape(-1)
    xs, gate = _gather(cum, rounds, slot, h2, affn, cap)
    tables = (cum, rounds, slott, cap)
    return tables, (xs.reshape(N_EXPERTS, b * cap, d), gate.reshape(N_EXPERTS, b * cap, LANES))


def _combine(tables, ys, x1, g2, next_norm=None):
    cum, rounds, slott, cap = tables
    b, _, d = x1.shape
    return _scatter(cum, rounds, slott, ys.reshape(N_EXPERTS, b, cap, d), x1, g2, cap, next_norm)


def kernel(x, c, ctx, c_ctx, w_ada, b_ada, norm1_w, norm2_w, w_in, conv_w, q_norm_w, k_norm_w,
           na_rpb, pool_w, pool_scale, w_branch, w_out, w_router, w_exp_gate, w_exp_up, w_exp_down):
    bsz, n, d = x.shape
    l_ctx = ctx.shape[1]
    depth = w_ada.shape[0]
    n_rows = n // GRID_W

    c16 = jnp.zeros((16, d), F32).at[:bsz].set(c).at[bsz].set(c_ctx)
    mod_all = _ada(c16, w_ada, b_ada)

    hd = np.arange(MIX_W) // HEAD_DIM
    bd = jnp.asarray(hd[:, None] == hd[None, :], dtype=BF16)
    dft_lat = _dft_mats(n, 1.0, n // 2) + _dft_mats(FOURIER_GC, (n * FOURIER_GC) ** -0.5)
    dft_ctx = _dft_mats(l_ctx, 1.0, l_ctx // 2) + _dft_mats(FOURIER_GC, (l_ctx * FOURIER_GC) ** -0.5)

    mods = [_mod_chunks(mod_all[l, :bsz]) for l in range(depth)]
    h = _prenorm(x, norm1_w[0], mods[0][0], mods[0][1])
    for l in range(depth):
        last = l == depth - 1
        lw = {
            "w_in": w_in, "conv_w": conv_w[l], "pool_w": pool_w[l],
            "pool_scale": pool_scale[l], "w_branch": w_branch, "layer": l,
        }
        w_out_bf = w_out[l].astype(BF16)
        wr_pad = jnp.zeros((d, LANES), F32).at[:, :N_EXPERTS].set(w_router[l])
        wr_hi = wr_pad.astype(BF16)
        wr_cat = jnp.concatenate([wr_hi, (wr_pad - wr_hi.astype(F32)).astype(BF16)], axis=1)
        q_w = jnp.tile(q_norm_w[l], N_HEADS).reshape(1, MIX_W)
        k_w = jnp.tile(k_norm_w[l], N_HEADS).reshape(1, MIX_W)
        tb = _na_bias_table(na_rpb[l], n_rows)
        _, _, g1, sh2, sc2, g2 = mods[l]
        mc = _mod_chunks(jnp.broadcast_to(mod_all[l, bsz:bsz + 1], (bsz, N_MOD * d)))

        hc = _prenorm(ctx, norm1_w[l], mc[0], mc[1])
        zc_mix = _in_proj(hc.reshape(bsz * l_ctx, d), w_in, l, MIX_COLS).reshape(bsz, l_ctx, MIX_COLS)

        zmix = _in_proj(h.reshape(bsz * n, d), w_in, l, MIX_COLS).reshape(bsz, n, MIX_COLS)
        att = _neighbourhood_attention(zmix, zc_mix, q_w, k_w, bd, tb)
        merged = _token_mixer(h, zmix, att, lw, dft_lat)
        x1, h2, aff, affn = _out_proj(merged, w_out_bf, x, g1, norm2_w[l], sh2, sc2, wr_cat)
        tables, routed = _route(h2, aff, affn)
        expert_w = (w_exp_gate, w_exp_up, w_exp_down, l)
        if last:
            x = _combine(tables, _experts(*routed, *expert_w), x1, g2)
        else:
            att_c = _context_attention(zc_mix, q_w, k_w, bd)
            merged_c = _token_mixer(hc, zc_mix, att_c, lw, dft_ctx)
            c1, hc2, aff_c, affn_c = _out_proj(merged_c, w_out_bf, ctx, mc[2], norm2_w[l], mc[3], mc[4], wr_cat)
            tables_c, routed_c = _route(hc2, aff_c, affn_c)
            x, h = _combine(tables, _experts(*routed, *expert_w), x1, g2,
                            (norm1_w[l + 1], mods[l + 1][0], mods[l + 1][1]))
            ctx = _combine(tables_c, _experts(*routed_c, *expert_w), c1, mc[5])
    return x
```

```python
import functools

import numpy as np
import jax
import jax.numpy as jnp
from jax import lax
from jax.experimental import pallas as pl
from jax.experimental.pallas import tpu as pltpu

F32 = jnp.float32
BF16 = jnp.bfloat16
I32 = jnp.int32

D_MODEL = 2048
GRID_W = 64
MIX_W = D_MODEL // 4
N_HEADS = 8
HEAD_DIM = MIX_W // N_HEADS
WIN_H = 8
WIN_W = 16
ATT_SCALE = HEAD_DIM ** -0.5
POOL_WINDOWS = (2, 4, 8, 16)
POOL_GC = MIX_W // len(POOL_WINDOWS)
FOURIER_GC = MIX_W // 4
N_EXPERTS = 16
EC_CAPACITY = 2
N_MOD = 6
EPS = 1e-6
NEG_INF = -1e30
MIX_COLS = 8 * MIX_W
LANES = 128
HALO = 16
SEL_TILE = 256
ROW_ALIGN = 16
EC_WINDOW = 64
ROWS_PER_PASS = 2
SCATTER_CHUNKS = 4
GATHER_CHUNKS = 4
VMEM_LIMIT = 56 * 1024 * 1024

COL_XA, COL_GB, COL_GC, COL_Q, COL_K, COL_V, COL_F, COL_P = range(8)


def _params(sem, vmem=VMEM_LIMIT):
    return pltpu.CompilerParams(dimension_semantics=sem, vmem_limit_bytes=vmem)


def _dot(a, b):
    return jnp.dot(a, b, preferred_element_type=F32)


def _dot_nt(a, b):
    return lax.dot_general(a, b, (((1,), (1,)), ((), ())), preferred_element_type=F32)


def _ada_kernel(c_ref, w_ref, b_ref, o_ref):
    c = c_ref[...]
    s = c * jax.nn.sigmoid(c)
    w = w_ref[0]
    s_hi, w_hi = s.astype(BF16), w.astype(BF16)
    s_lo = (s - s_hi.astype(F32)).astype(BF16)
    w_lo = (w - w_hi.astype(F32)).astype(BF16)
    o_ref[0] = _dot(s_hi, w_hi) + (_dot(s_lo, w_hi) + _dot(s_hi, w_lo)) + b_ref[0]


def _ada(c16, w_ada, b_ada):
    depth, d, ncol = w_ada.shape
    tn = 1024
    return pl.pallas_call(
        _ada_kernel,
        grid=(depth, ncol // tn),
        in_specs=[
            pl.BlockSpec((16, d), lambda l, j: (0, 0)),
            pl.BlockSpec((1, d, tn), lambda l, j: (l, 0, j)),
            pl.BlockSpec((1, 1, tn), lambda l, j: (l, 0, j)),
        ],
        out_specs=pl.BlockSpec((1, 16, tn), lambda l, j: (l, 0, j)),
        out_shape=jax.ShapeDtypeStruct((depth, 16, ncol), F32),
        compiler_params=_params(("parallel", "parallel")),
        name="ada",
    )(c16, w_ada, b_ada.reshape(depth, 1, ncol))


def _prenorm_math(x, w, shift, scale):
    ms = jnp.mean(x * x, axis=-1, keepdims=True)
    y = x * lax.rsqrt(ms + EPS) * w
    return y * (1.0 + scale) + shift


def _prenorm_kernel(x_ref, w_ref, sh_ref, sc_ref, o_ref):
    o_ref[0] = _prenorm_math(x_ref[0], w_ref[...], sh_ref[0], sc_ref[0]).astype(BF16)


def _prenorm(x, w, shift, scale):
    b, n, d = x.shape
    tm = min(1024, n)
    return pl.pallas_call(
        _prenorm_kernel,
        grid=(b, n // tm),
        in_specs=[
            pl.BlockSpec((1, tm, d), lambda i, j: (i, j, 0)),
            pl.BlockSpec((1, d), lambda i, j: (0, 0)),
            pl.BlockSpec((1, 1, d), lambda i, j: (i, 0, 0)),
            pl.BlockSpec((1, 1, d), lambda i, j: (i, 0, 0)),
        ],
        out_specs=pl.BlockSpec((1, tm, d), lambda i, j: (i, j, 0)),
        out_shape=jax.ShapeDtypeStruct((b, n, d), BF16),
        compiler_params=_params(("parallel", "parallel")),
        name="prenorm",
    )(x, w.reshape(1, d), shift, scale)


def _mm_kernel(a_ref, w_ref, o_ref, w_bf):
    @pl.when(pl.program_id(1) == 0)
    def _():
        w_bf[...] = w_ref[0].astype(BF16)

    o_ref[...] = _dot(a_ref[...], w_bf[...]).astype(o_ref.dtype)


def _in_proj(h2d, w, layer, ncols):
    m, k = h2d.shape
    tm = min(1024, m)
    tn = 1024
    return pl.pallas_call(
        _mm_kernel,
        grid=(ncols // tn, m // tm),
        in_specs=[
            pl.BlockSpec((tm, k), lambda j, i: (i, 0)),
            pl.BlockSpec((1, k, tn), lambda j, i: (layer, 0, j)),
        ],
        out_specs=pl.BlockSpec((tm, tn), lambda j, i: (i, j)),
        out_shape=jax.ShapeDtypeStruct((m, ncols), BF16),
        scratch_shapes=[pltpu.VMEM((k, tn), BF16)],
        compiler_params=_params(("parallel", "arbitrary")),
        name="in_proj",
    )(h2d, w)


def _stencil_kernel(xa_ref, gb_ref, gc_ref, zp_ref, xa_p, xa_n, gc_p, gc_n, zp_p, zp_n,
                    cw_ref, pw_ref, ps_ref, oc_ref, op_ref, *, tm, n_seq):
    i = pl.program_id(1)
    nt = pl.num_programs(1)
    rows = tm + 2 * HALO

    def ext(main_ref, prev_ref, next_ref):
        main = main_ref[0].astype(F32)
        prev = jnp.where(i > 0, prev_ref[0].astype(F32), 0.0)
        nxt = jnp.where(i < nt - 1, next_ref[0].astype(F32), 0.0)
        return jnp.concatenate([prev, main, nxt], axis=0)

    def shifted(a, d):
        if d == 0:
            return a[HALO:HALO + tm]
        return pltpu.roll(a, (-d) % rows, axis=0)[HALO:HALO + tm]

    u = ext(gc_ref, gc_p, gc_n) * ext(xa_ref, xa_p, xa_n)
    cw = cw_ref[...]
    y = cw[0:1] * shifted(u, -1) + cw[1:2] * shifted(u, 0) + cw[2:3] * shifted(u, 1)
    oc_ref[0] = (gb_ref[0].astype(F32) * y).astype(BF16)

    t = (i * tm + lax.broadcasted_iota(I32, (tm, 1), 0)).astype(F32)
    zp = ext(zp_ref, zp_p, zp_n)
    for g, win in enumerate(POOL_WINDOWS):
        ug = zp[:, g * POOL_GC:(g + 1) * POOL_GC]
        acc = ug + pltpu.roll(ug, 1, axis=0)
        half = 1
        while 2 * half < win:
            acc = pltpu.roll(acc, rows - half, axis=0) + pltpu.roll(acc, half, axis=0)
            half *= 2
        wsum = acc[HALO:HALO + tm]
        lo = jnp.maximum(t - (win // 2), 0.0)
        hi = jnp.minimum(t + (win - win // 2 - 1), float(n_seq - 1))
        pooled = wsum / (hi - lo + 1.0) - ug[HALO:HALO + tm]
        yg = _dot(pooled.astype(BF16), pw_ref[g])
        op_ref[0, :, g * POOL_GC:(g + 1) * POOL_GC] = (
            yg * ps_ref[:, g * POOL_GC:(g + 1) * POOL_GC]).astype(BF16)


def _stencil(zmix, conv_w, pool_w, pool_scale):
    b, n, _ = zmix.shape
    tm = min(512, n)
    hb = tm // HALO
    last = n // HALO - 1

    def main(col):
        return pl.BlockSpec((1, tm, MIX_W), lambda bi, i: (bi, i, col))

    def prev(col):
        return pl.BlockSpec((1, HALO, MIX_W), lambda bi, i: (bi, jnp.maximum(i * hb - 1, 0), col))

    def nxt(col):
        return pl.BlockSpec((1, HALO, MIX_W), lambda bi, i: (bi, jnp.minimum((i + 1) * hb, last), col))

    out_spec = pl.BlockSpec((1, tm, MIX_W), lambda bi, i: (bi, i, 0))
    return pl.pallas_call(
        functools.partial(_stencil_kernel, tm=tm, n_seq=n),
        grid=(b, n // tm),
        in_specs=[main(COL_XA), main(COL_GB), main(COL_GC), main(COL_P),
                  prev(COL_XA), nxt(COL_XA), prev(COL_GC), nxt(COL_GC), prev(COL_P), nxt(COL_P),
                  pl.BlockSpec((3, MIX_W), lambda bi, i: (0, 0)),
                  pl.BlockSpec((len(POOL_WINDOWS), POOL_GC, POOL_GC), lambda bi, i: (0, 0, 0)),
                  pl.BlockSpec((1, MIX_W), lambda bi, i: (0, 0))],
        out_specs=[out_spec, out_spec],
        out_shape=[jax.ShapeDtypeStruct((b, n, MIX_W), BF16)] * 2,
        compiler_params=_params(("parallel", "parallel")),
        name="stencil",
    )(zmix, zmix, zmix, zmix, zmix, zmix, zmix, zmix, zmix, zmix,
      conv_w, pool_w.astype(BF16), pool_scale.reshape(1, MIX_W))


def _fourier_kernel(cn_ref, sn_ref, u_ref, cc_ref, sc_ref, o_ref, even, odd):
    n = u_ref.shape[1]
    half, tk = n // 2, cn_ref.shape[0]
    rb = min(256, half)

    @pl.when(pl.program_id(1) == 0)
    def _():
        i = lax.broadcasted_iota(I32, (rb, rb), 0)
        j = lax.broadcasted_iota(I32, (rb, rb), 1)
        flip = jnp.where(i + j == rb, 1.0, 0.0).astype(BF16)
        row = lax.broadcasted_iota(I32, (rb, 1), 0)
        for a in range(half // rb):
            src = n - (a + 1) * rb
            rev = _dot(flip, u_ref[0, src:src + rb, :])
            first = (n - a * rb) % n
            rev = jnp.where(row == 0, u_ref[0, first:first + 1, :].astype(F32), rev)
            ua = u_ref[0, a * rb:(a + 1) * rb, :].astype(F32)
            fold = ua + rev if a else jnp.where(row == 0, ua, ua + rev)
            even[a * rb:(a + 1) * rb, :] = fold.astype(BF16)
            odd[a * rb:(a + 1) * rb, :] = (ua - rev).astype(BF16)

    k = pl.program_id(1) * tk + lax.broadcasted_iota(I32, (tk, 1), 0)
    sign = (1 - 2 * jnp.bitwise_and(k, 1)).astype(F32)
    p = (_dot(cn_ref[...], even[...]) + sign * u_ref[0, half:half + 1, :].astype(F32)).astype(BF16)
    q = _dot(sn_ref[...], odd[...]).astype(BF16)
    for g in range(MIX_W // FOURIER_GC):
        sl = slice(g * FOURIER_GC, (g + 1) * FOURIER_GC)
        o_ref[0, :, sl] = (_dot(p[:, sl], cc_ref[...]) - _dot(q[:, sl], sc_ref[...])).astype(BF16)


def _dft_mats(n, scale, ncols=None):
    ncols = n if ncols is None else ncols
    k = jnp.arange(n, dtype=I32)[:, None]

    def table(count, period):
        r = (k * jnp.arange(count, dtype=I32)[None, :]) % period
        ang = r.astype(F32) * np.float32(2.0 * np.pi / period)
        return jnp.cos(ang), jnp.sin(ang)

    inner = 64
    if n <= inner or n % inner or ncols % inner:
        c, s = table(ncols, n)
    else:
        ca, sa = table(ncols // inner, n // inner)
        cb, sb = table(inner, n)
        c = (ca[:, :, None] * cb[:, None, :] - sa[:, :, None] * sb[:, None, :]).reshape(n, ncols)
        s = (sa[:, :, None] * cb[:, None, :] + ca[:, :, None] * sb[:, None, :]).reshape(n, ncols)
    return (c * scale).astype(BF16), (s * scale).astype(BF16)


def _fourier(zmix, cn, sn, cc, sc):
    b, n, _ = zmix.shape
    half = n // 2
    tk = min(512, n)
    return pl.pallas_call(
        _fourier_kernel,
        grid=(b, n // tk),
        in_specs=[
            pl.BlockSpec((tk, half), lambda bi, k: (k, 0)),
            pl.BlockSpec((tk, half), lambda bi, k: (k, 0)),
            pl.BlockSpec((1, n, MIX_W), lambda bi, k: (bi, 0, COL_F)),
            pl.BlockSpec((FOURIER_GC, FOURIER_GC), lambda bi, k: (0, 0)),
            pl.BlockSpec((FOURIER_GC, FOURIER_GC), lambda bi, k: (0, 0)),
        ],
        out_specs=pl.BlockSpec((1, tk, MIX_W), lambda bi, k: (bi, k, 0)),
        out_shape=jax.ShapeDtypeStruct((b, n, MIX_W), BF16),
        scratch_shapes=[pltpu.VMEM((half, MIX_W), BF16), pltpu.VMEM((half, MIX_W), BF16)],
        compiler_params=_params(("parallel", "arbitrary")),
        name="fourier",
    )(cn, sn, zmix, cc, sc)


def _head_rms(x, w, bd):
    x2 = x * x
    hi = x2.astype(BF16)
    lo = (x2 - hi.astype(F32)).astype(BF16)
    ms = (_dot(hi, bd) + _dot(lo, bd)) * (1.0 / HEAD_DIM)
    return x * lax.rsqrt(ms + EPS) * w


def _pair_stack(q2):
    lane = lax.broadcasted_iota(I32, q2.shape, 1)
    zero = jnp.zeros_like(q2)
    return jnp.concatenate([jnp.where(lane < HEAD_DIM, q2, zero),
                            jnp.where(lane >= HEAD_DIM, q2, zero)], axis=0)


def _pair_unstack(o2, m):
    lane = lax.broadcasted_iota(I32, (m, LANES), 1)
    return jnp.where(lane < HEAD_DIM, o2[:m], o2[m:])


def _na_kernel(q_ref, k_ref, v_ref, kc_ref, vc_ref, qw_ref, kw_ref, bd_ref, tb_ref, o_ref,
               kn_scr, kcn_scr, qn_scr, *, rt, n_rows):
    rb = pl.program_id(1)
    n_seq = n_rows * GRID_W
    chunk = 512

    @pl.when(rb == 0)
    def _():
        def body(c, carry):
            sl = pl.ds(pl.multiple_of(c * chunk, chunk), chunk)
            kn_scr[sl, :] = _head_rms(k_ref[0, sl, :].astype(F32), kw_ref[...], bd_ref[...]).astype(BF16)
            return carry
        lax.fori_loop(0, n_seq // chunk, body, 0)
        kcn_scr[...] = _head_rms(kc_ref[0].astype(F32), kw_ref[...], bd_ref[...]).astype(BF16)

    qn_scr[...] = (_head_rms(q_ref[0].astype(F32), qw_ref[...], bd_ref[...]) * ATT_SCALE).astype(BF16)

    kh = min(WIN_H, n_rows)
    nloc = kh * GRID_W

    def row_body(jj, carry):
        units = []
        for i in range(ROWS_PER_PASS):
            j = jj * ROWS_PER_PASS + i
            r = rb * rt + j
            rs = jnp.clip(r - kh // 2, 0, n_rows - kh)
            q_rows = pl.ds(pl.multiple_of(j * GRID_W, GRID_W), GRID_W)
            k_rows = pl.ds(pl.multiple_of(rs * GRID_W, GRID_W), nloc)
            for p in range(N_HEADS // 2):
                units.append((q_rows, k_rows, rs - r + (WIN_H - 1), p, slice(p * LANES, (p + 1) * LANES)))
        qs = [_pair_stack(qn_scr[q_rows, cols]) for q_rows, _, _, _, cols in units]
        s_loc = [_dot_nt(q, kn_scr[k_rows, cols]) for q, (_, k_rows, _, _, cols) in zip(qs, units)]
        s_ctx = [_dot_nt(q, kcn_scr[:, cols]) for q, (_, _, _, _, cols) in zip(qs, units)]
        s_loc = [s + tb_ref[dr0, p] for s, (_, _, dr0, p, _) in zip(s_loc, units)]
        m = [jnp.maximum(jnp.max(a, axis=1, keepdims=True), jnp.max(c, axis=1, keepdims=True))
             for a, c in zip(s_loc, s_ctx)]
        p_loc = [jnp.exp(a - mx) for a, mx in zip(s_loc, m)]
        p_ctx = [jnp.exp(c - mx) for c, mx in zip(s_ctx, m)]
        denom = [jnp.sum(a, axis=1, keepdims=True) + jnp.sum(c, axis=1, keepdims=True) for a, c in zip(p_loc, p_ctx)]
        o2 = [_dot(a.astype(BF16), v_ref[0, k_rows, cols]) + _dot(c.astype(BF16), vc_ref[0, :, cols])
              for a, c, (_, k_rows, _, _, cols) in zip(p_loc, p_ctx, units)]
        for o, d, (q_rows, _, _, _, cols) in zip(o2, denom, units):
            o_ref[0, q_rows, cols] = _pair_unstack(o / d, GRID_W).astype(BF16)
        return carry

    lax.fori_loop(0, rt // ROWS_PER_PASS, row_body, 0)


def _na_bias_table(rpb, n_rows):
    kh = min(WIN_H, n_rows)
    c = np.arange(GRID_W)
    cs = np.clip(c - WIN_W // 2, 0, GRID_W - WIN_W)
    kc = np.arange(GRID_W)
    ok = (kc[None, :] >= cs[:, None]) & (kc[None, :] < cs[:, None] + WIN_W)
    edge = GRID_W - WIN_W
    padded = jnp.pad(rpb.astype(F32), ((0, 0), (0, 0), (edge, edge)))
    toeplitz = jnp.stack([padded[:, :, GRID_W - 1 - q:2 * GRID_W - 1 - q] for q in range(GRID_W)], axis=2)
    masked = jnp.where(ok[None, None], toeplitz, NEG_INF)
    t = jnp.stack([masked[:, d0:d0 + kh] for d0 in range(WIN_H)], axis=0)
    t = jnp.transpose(t, (0, 1, 3, 2, 4))
    return t.reshape(WIN_H, N_HEADS // 2, 2 * GRID_W, kh * GRID_W)


def _neighbourhood_attention(zmix, zc_mix, q_w, k_w, bd, tb):
    b, n, _ = zmix.shape
    l = zc_mix.shape[1]
    n_rows = n // GRID_W
    rt = 32
    nloc = min(WIN_H, n_rows) * GRID_W
    return pl.pallas_call(
        functools.partial(_na_kernel, rt=rt, n_rows=n_rows),
        grid=(b, n_rows // rt),
        in_specs=[
            pl.BlockSpec((1, rt * GRID_W, MIX_W), lambda bi, r: (bi, r, COL_Q)),
            pl.BlockSpec((1, n, MIX_W), lambda bi, r: (bi, 0, COL_K)),
            pl.BlockSpec((1, n, MIX_W), lambda bi, r: (bi, 0, COL_V)),
            pl.BlockSpec((1, l, MIX_W), lambda bi, r: (bi, 0, COL_K)),
            pl.BlockSpec((1, l, MIX_W), lambda bi, r: (bi, 0, COL_V)),
            pl.BlockSpec((1, MIX_W), lambda bi, r: (0, 0)),
            pl.BlockSpec((1, MIX_W), lambda bi, r: (0, 0)),
            pl.BlockSpec((MIX_W, MIX_W), lambda bi, r: (0, 0)),
            pl.BlockSpec((WIN_H, N_HEADS // 2, 2 * GRID_W, nloc), lambda bi, r: (0, 0, 0, 0),
                         pipeline_mode=pl.Buffered(1)),
        ],
        out_specs=pl.BlockSpec((1, rt * GRID_W, MIX_W), lambda bi, r: (bi, r, 0)),
        out_shape=jax.ShapeDtypeStruct((b, n, MIX_W), BF16),
        scratch_shapes=[pltpu.VMEM((n, MIX_W), BF16), pltpu.VMEM((l, MIX_W), BF16),
                        pltpu.VMEM((rt * GRID_W, MIX_W), BF16)],
        compiler_params=_params(("parallel", "arbitrary")),
        name="nbr_attention",
    )(zmix, zmix, zmix, zc_mix, zc_mix, q_w, k_w, bd, tb)


def _ctx_attn_kernel(q_ref, k_ref, v_ref, qw_ref, kw_ref, bd_ref, o_ref):
    l = q_ref.shape[1]
    qn = (_head_rms(q_ref[0].astype(F32), qw_ref[...], bd_ref[...]) * ATT_SCALE).astype(BF16)
    kn = _head_rms(k_ref[0].astype(F32), kw_ref[...], bd_ref[...]).astype(BF16)
    for p in range(N_HEADS // 2):
        cols = slice(p * LANES, (p + 1) * LANES)
        qs = _pair_stack(qn[:, cols])
        s = _dot_nt(qs, kn[:, cols])
        m = jnp.max(s, axis=1, keepdims=True)
        e = jnp.exp(s - m)
        o2 = _dot(e.astype(BF16), v_ref[0, :, cols]) / jnp.sum(e, axis=1, keepdims=True)
        o_ref[0, :, cols] = _pair_unstack(o2, l).astype(BF16)


def _context_attention(zc_mix, q_w, k_w, bd):
    b, l, _ = zc_mix.shape
    vec = pl.BlockSpec((1, MIX_W), lambda bi: (0, 0))
    return pl.pallas_call(
        _ctx_attn_kernel,
        grid=(b,),
        in_specs=[
            pl.BlockSpec((1, l, MIX_W), lambda bi: (bi, 0, COL_Q)),
            pl.BlockSpec((1, l, MIX_W), lambda bi: (bi, 0, COL_K)),
            pl.BlockSpec((1, l, MIX_W), lambda bi: (bi, 0, COL_V)),
            vec, vec,
            pl.BlockSpec((MIX_W, MIX_W), lambda bi: (0, 0)),
        ],
        out_specs=pl.BlockSpec((1, l, MIX_W), lambda bi: (bi, 0, 0)),
        out_shape=jax.ShapeDtypeStruct((b, l, MIX_W), BF16),
        compiler_params=_params(("parallel",)),
        name="ctx_attention",
    )(zc_mix, zc_mix, zc_mix, q_w, k_w, bd)


def _merge_kernel(h_ref, y0, y1, y2, y3, g0, g1, g2, g3, wb_ref, o_ref, wg_bf, wb_bf):
    @pl.when(pl.program_id(1) == 0)
    def _():
        for i, g in enumerate((g0, g1, g2, g3)):
            wg_bf[i] = g[0].astype(BF16)
        wb_bf[...] = wb_ref[0].astype(BF16)

    h = h_ref[...]
    acc = None
    for i, y in enumerate((y0, y1, y2, y3)):
        gate = jax.nn.sigmoid(_dot(h, wg_bf[i]))
        term = gate * _dot(y[...], wb_bf[i])
        acc = term if acc is None else acc + term
    o_ref[...] = acc.astype(BF16)


def _merge(h2d, branches, w_in, w_branch, layer):
    m, d = h2d.shape
    tm = min(1024, m)
    tn = 256
    gate0 = MIX_COLS // tn

    def gspec(i):
        return pl.BlockSpec((1, d, tn), lambda n, r: (layer, 0, gate0 + i * (d // tn) + n))

    yspec = pl.BlockSpec((tm, MIX_W), lambda n, r: (r, 0))
    return pl.pallas_call(
        _merge_kernel,
        grid=(d // tn, m // tm),
        in_specs=[pl.BlockSpec((tm, d), lambda n, r: (r, 0)), yspec, yspec, yspec, yspec,
                  gspec(0), gspec(1), gspec(2), gspec(3),
                  pl.BlockSpec((1, 4, MIX_W, tn), lambda n, r: (layer, 0, 0, n))],
        out_specs=pl.BlockSpec((tm, tn), lambda n, r: (r, n)),
        out_shape=jax.ShapeDtypeStruct((m, d), BF16),
        scratch_shapes=[pltpu.VMEM((4, d, tn), BF16), pltpu.VMEM((4, MIX_W, tn), BF16)],
        compiler_params=_params(("parallel", "arbitrary")),
        name="merge",
    )(h2d, *branches, w_in, w_in, w_in, w_in, w_branch)


def _out_proj_kernel(m_ref, w_ref, x_ref, g_ref, nw_ref, sh_ref, sc_ref, wrc_ref,
                     x1_ref, h2_ref, aff_ref, affn_ref):
    x1 = x_ref[0] + g_ref[0] * _dot(m_ref[0], w_ref[...])
    x1_ref[0] = x1
    h2 = _prenorm_math(x1, nw_ref[...], sh_ref[0], sc_ref[0])
    h_hi = h2.astype(BF16)
    h2_ref[0] = h_hi
    h_lo = (h2 - h_hi.astype(F32)).astype(BF16)
    both = _dot(h_hi, wrc_ref[...])
    logits = both[:, :LANES] + (both[:, LANES:] + _dot(h_lo, wrc_ref[:, :LANES]))
    lane = lax.broadcasted_iota(I32, logits.shape, 1)
    logits = jnp.where(lane < N_EXPERTS, logits, NEG_INF)
    e = jnp.exp(logits - jnp.max(logits, axis=1, keepdims=True))
    aff = e / jnp.sum(e, axis=1, keepdims=True)
    aff_ref[0] = aff.T[:N_EXPERTS]
    hi = aff.astype(BF16).astype(F32)
    r1 = aff - hi
    mid = r1.astype(BF16).astype(F32)
    lo = (r1 - mid).astype(BF16).astype(F32)
    packed = hi + pltpu.roll(mid, N_EXPERTS, axis=1) + pltpu.roll(lo, 2 * N_EXPERTS, axis=1)
    affn_ref[0] = packed.astype(BF16)


def _out_proj(merged, w_out_bf, x, gate, norm_w, shift, scale, wr_cat):
    b, n, d = x.shape
    tm = min(512, n)
    vec = pl.BlockSpec((1, 1, d), lambda i, j: (i, 0, 0))
    tile = pl.BlockSpec((1, tm, d), lambda i, j: (i, j, 0))
    wr_spec = pl.BlockSpec((d, 2 * LANES), lambda i, j: (0, 0))
    return pl.pallas_call(
        _out_proj_kernel,
        grid=(b, n // tm),
        in_specs=[tile, pl.BlockSpec((d, d), lambda i, j: (0, 0), pipeline_mode=pl.Buffered(1)), tile, vec,
                  pl.BlockSpec((1, d), lambda i, j: (0, 0)), vec, vec, wr_spec],
        out_specs=[tile, tile,
                   pl.BlockSpec((1, N_EXPERTS, tm), lambda i, j: (i, 0, j)),
                   pl.BlockSpec((1, tm, LANES), lambda i, j: (i, j, 0))],
        out_shape=[jax.ShapeDtypeStruct((b, n, d), F32), jax.ShapeDtypeStruct((b, n, d), BF16),
                   jax.ShapeDtypeStruct((b, N_EXPERTS, n), F32), jax.ShapeDtypeStruct((b, n, LANES), BF16)],
        compiler_params=_params(("parallel", "parallel")),
        name="out_proj",
    )(merged, w_out_bf, x, gate, norm_w.reshape(1, d), shift, scale, wr_cat)


def _lane_cumsum(mask_f, ut):
    rows, n = mask_f.shape
    carry = jnp.zeros((rows, 1), F32)
    parts = []
    for j in range(n // LANES):
        c = _dot(mask_f[:, j * LANES:(j + 1) * LANES].astype(BF16), ut) + carry
        parts.append(c)
        carry = c[:, LANES - 1:LANES]
    return jnp.concatenate(parts, axis=1)


def _select_kernel(aff_ref, slot_ref, slott_ref, *, cap):
    aff = aff_ref[0]
    e, n = aff.shape
    bits = lax.bitcast_convert_type(aff, I32)

    def body(_, carry):
        lo, hi = carry
        mid = lo + ((hi - lo) >> 1)
        cnt = jnp.sum(jnp.where(bits >= mid, 1.0, 0.0), axis=1, keepdims=True)
        ge = cnt >= float(cap)
        return jnp.where(ge, mid, lo), jnp.where(ge, hi, mid)

    lo, _ = lax.fori_loop(0, 31, body, (jnp.zeros((e, 1), I32), jnp.full((e, 1), 0x7F800000, I32)))
    ri = lax.broadcasted_iota(I32, (LANES, LANES), 0)
    ci = lax.broadcasted_iota(I32, (LANES, LANES), 1)
    ut = jnp.where(ri <= ci, 1.0, 0.0).astype(BF16)
    gt = bits > lo
    eq = bits == lo
    need = float(cap) - jnp.sum(jnp.where(gt, 1.0, 0.0), axis=1, keepdims=True)
    ceq = _lane_cumsum(jnp.where(eq, 1.0, 0.0), ut)
    sel = gt | (eq & (ceq <= need))
    csel = _lane_cumsum(jnp.where(sel, 1.0, 0.0), ut)
    slot = jnp.where(sel, csel - 1.0, -1.0)
    slot_ref[0] = slot.astype(I32)
    padded = jnp.concatenate([slot, jnp.full((LANES - e, n), -1.0, F32)], axis=0)
    slott_ref[0] = padded.T


def _select(aff, cap):
    b, e, n = aff.shape
    return pl.pallas_call(
        functools.partial(_select_kernel, cap=cap),
        grid=(b,),
        in_specs=[pl.BlockSpec((1, e, n), lambda i: (i, 0, 0))],
        out_specs=[pl.BlockSpec((1, e, n), lambda i: (i, 0, 0)),
                   pl.BlockSpec((1, n, LANES), lambda i: (i, 0, 0))],
        out_shape=[jax.ShapeDtypeStruct((b, e, n), I32), jax.ShapeDtypeStruct((b, n, LANES), F32)],
        compiler_params=_params(("parallel",)),
        name="select",
    )(aff)


def _window(cum_ref, b, e, j, r, n_tiles, cap, win):
    c_lo = cum_ref[(b * N_EXPERTS + e) * (n_tiles + 1) + j]
    first = (c_lo // ROW_ALIGN) * ROW_ALIGN + r * win
    start = pl.multiple_of(jnp.minimum(first, cap - win), ROW_ALIGN)
    return first, start


def _gather_kernel(cum_ref, rnd_ref, slot_ref, h_ref, affn_ref, xs_ref, gate_ref, *, cap, win, n_tiles):
    b = pl.program_id(0)
    dq = pl.program_id(1)
    xs_ref[...] = jnp.zeros_like(xs_ref)

    @pl.when(dq == 0)
    def _():
        gate_ref[...] = jnp.zeros_like(gate_ref)

    rows = lax.broadcasted_iota(I32, (win, SEL_TILE), 0)

    def tile_body(j, carry):
        toks = pl.ds(pl.multiple_of(j * SEL_TILE, SEL_TILE), SEL_TILE)
        slots = slot_ref[0, j]
        h_tile = h_ref[0, toks, :]

        def round_body(r, carry2):
            starts, pieces = [], []
            for e in range(N_EXPERTS):
                first, start = _window(cum_ref, b, e, j, r, n_tiles, cap, win)
                s_row = slots[e:e + 1, :]
                hit = (rows + start == s_row) & (s_row >= first)
                pieces.append(jnp.where(hit, 1.0, 0.0).astype(BF16))
                starts.append(start)
            ge = N_EXPERTS // GATHER_CHUNKS
            onehots = [jnp.concatenate(pieces[lo:lo + ge], axis=0) for lo in range(0, N_EXPERTS, ge)]
            for c, onehot in enumerate(onehots):
                res = _dot(onehot, h_tile)
                for i in range(ge):
                    e = c * ge + i
                    sl = pl.ds(starts[e], win)
                    xs_ref[e, 0, sl, :] = xs_ref[e, 0, sl, :] + res[i * win:(i + 1) * win].astype(BF16)

            @pl.when(dq == 0)
            def _():
                for c, onehot in enumerate(onehots):
                    g = _dot(onehot, affn_ref[0, toks, :])
                    for i in range(ge):
                        gate_ref[c * ge + i, 0, pl.ds(starts[c * ge + i], win), :] += g[i * win:(i + 1) * win]
            return carry2

        lax.fori_loop(0, rnd_ref[b * n_tiles + j], round_body, 0)
        return carry

    lax.fori_loop(0, n_tiles, tile_body, 0)

    @pl.when(dq == 0)
    def _():
        lane = lax.broadcasted_iota(I32, (cap, LANES), 1)
        for e in range(N_EXPERTS):
            mine = (lane == e) | (lane == N_EXPERTS + e) | (lane == 2 * N_EXPERTS + e)
            col = jnp.sum(jnp.where(mine, gate_ref[e, 0], 0.0), axis=1, keepdims=True)
            gate_ref[e, 0] = jnp.broadcast_to(col, (cap, LANES))


def _gather(cum, rounds, slot, h2, affn, cap):
    b, n, d = h2.shape
    n_tiles = n // SEL_TILE
    win = min(EC_WINDOW, cap)
    dn = 512
    slot_tiles = slot.reshape(b, N_EXPERTS, n_tiles, SEL_TILE).transpose(0, 2, 1, 3)
    grid_spec = pltpu.PrefetchScalarGridSpec(
        num_scalar_prefetch=2,
        grid=(b, d // dn),
        in_specs=[
            pl.BlockSpec((1, n_tiles, N_EXPERTS, SEL_TILE), lambda i, q, c, r: (i, 0, 0, 0)),
            pl.BlockSpec((1, n, dn), lambda i, q, c, r: (i, 0, q)),
            pl.BlockSpec((1, n, LANES), lambda i, q, c, r: (i, 0, 0)),
        ],
        out_specs=[pl.BlockSpec((N_EXPERTS, 1, cap, dn), lambda i, q, c, r: (0, i, 0, q)),
                   pl.BlockSpec((N_EXPERTS, 1, cap, LANES), lambda i, q, c, r: (0, i, 0, 0))],
    )
    return pl.pallas_call(
        functools.partial(_gather_kernel, cap=cap, win=win, n_tiles=n_tiles),
        grid_spec=grid_spec,
        out_shape=[jax.ShapeDtypeStruct((N_EXPERTS, b, cap, d), BF16),
                   jax.ShapeDtypeStruct((N_EXPERTS, b, cap, LANES), F32)],
        compiler_params=_params(("parallel", "arbitrary")),
        name="ec_gather",
    )(cum, rounds, slot_tiles, h2, affn)


def _expert_up_kernel(xs_ref, wg_ref, wu_ref, h_ref, wg_bf, wu_bf):
    @pl.when(pl.program_id(2) == 0)
    def _():
        wg_bf[...] = wg_ref[0, 0].astype(BF16)
        wu_bf[...] = wu_ref[0, 0].astype(BF16)

    x = xs_ref[0]
    a = _dot(x, wg_bf[...])
    u = _dot(x, wu_bf[...])
    h_ref[0] = (a * jax.nn.sigmoid(a) * u).astype(BF16)


def _expert_down_kernel(h_ref, wd_ref, gate_ref, ys_ref, wd_bf):
    @pl.when(pl.program_id(2) == 0)
    def _():
        wd_bf[...] = wd_ref[0, 0].astype(BF16)

    y = _dot(h_ref[0], wd_bf[...])
    g = gate_ref[0]
    ys_ref[0] = (y * jnp.concatenate([g] * (y.shape[1] // LANES), axis=1)).astype(BF16)


def _experts(xs, gate, wg, wu, wd, layer):
    e, m, d = xs.shape
    ff = wg.shape[3]
    tm = min(1024, m)
    tf = 512
    tn = 1024
    sem = ("parallel", "parallel", "arbitrary")
    hidden = pl.pallas_call(
        _expert_up_kernel,
        grid=(e, ff // tf, m // tm),
        in_specs=[
            pl.BlockSpec((1, tm, d), lambda ei, f, i: (ei, i, 0)),
            pl.BlockSpec((1, 1, d, tf), lambda ei, f, i: (layer, ei, 0, f)),
            pl.BlockSpec((1, 1, d, tf), lambda ei, f, i: (layer, ei, 0, f)),
        ],
        out_specs=pl.BlockSpec((1, tm, tf), lambda ei, f, i: (ei, i, f)),
        out_shape=jax.ShapeDtypeStruct((e, m, ff), BF16),
        scratch_shapes=[pltpu.VMEM((d, tf), BF16), pltpu.VMEM((d, tf), BF16)],
        compiler_params=_params(sem),
        name="expert_up",
    )(xs, wg, wu)
    return pl.pallas_call(
        _expert_down_kernel,
        grid=(e, d // tn, m // tm),
        in_specs=[
            pl.BlockSpec((1, tm, ff), lambda ei, j, i: (ei, i, 0)),
            pl.BlockSpec((1, 1, ff, tn), lambda ei, j, i: (layer, ei, 0, j)),
            pl.BlockSpec((1, tm, LANES), lambda ei, j, i: (ei, i, 0)),
        ],
        out_specs=pl.BlockSpec((1, tm, tn), lambda ei, j, i: (ei, i, j)),
        out_shape=jax.ShapeDtypeStruct((e, m, d), BF16),
        scratch_shapes=[pltpu.VMEM((ff, tn), BF16)],
        compiler_params=_params(sem),
        name="expert_down",
    )(hidden, wd, gate)


def _scatter_kernel(cum_ref, rnd_ref, slott_ref, ys_ref, x_ref, g_ref, *rest, cap, win, n_tiles, next_norm):
    if next_norm:
        nw_ref, sh_ref, sc_ref, o_ref, hn_ref = rest
    else:
        (o_ref,) = rest
    b = pl.program_id(0)
    j = pl.program_id(1)
    st = slott_ref[0]
    per_block = LANES // win
    lane = lax.broadcasted_iota(I32, (SEL_TILE, LANES), 1)
    lane_f = lane.astype(F32)
    group = lane // win

    def round_body(r, acc):
        blocks, pieces = [], []
        for k in range(N_EXPERTS // per_block):
            target = None
            for i in range(per_block):
                e = k * per_block + i
                first, start = _window(cum_ref, b, e, j, r, n_tiles, cap, win)
                col = st[:, e:e + 1]
                tgt = jnp.where(col >= first.astype(F32), col - (start - i * win).astype(F32), -1.0)
                target = tgt if target is None else jnp.where(group == i, tgt, target)
                pieces.append(ys_ref[e, 0, pl.ds(start, win), :])
            blocks.append(jnp.where(lane_f == target, 1.0, 0.0).astype(BF16))
        nb = max(1, len(blocks) // SCATTER_CHUNKS)
        for lo in range(0, len(blocks), nb):
            onehot = jnp.concatenate(blocks[lo:lo + nb], axis=1)
            stacked = jnp.concatenate(pieces[lo * per_block:(lo + nb) * per_block], axis=0)
            part = _dot(onehot, stacked)
            acc = part if acc is None else acc + part
        return acc

    acc = lax.fori_loop(1, rnd_ref[b * n_tiles + j], round_body, round_body(0, None))
    x2 = x_ref[0] + g_ref[0] * acc
    o_ref[0] = x2
    if next_norm:
        hn_ref[0] = _prenorm_math(x2, nw_ref[...], sh_ref[0], sc_ref[0]).astype(BF16)


def _scatter(cum, rounds, slott, ys, x1, gate, cap, next_norm=None):
    b, n, d = x1.shape
    n_tiles = n // SEL_TILE
    win = min(EC_WINDOW, cap)
    tile = pl.BlockSpec((1, SEL_TILE, d), lambda i, t, c, r: (i, t, 0))
    vec = pl.BlockSpec((1, 1, d), lambda i, t, c, r: (i, 0, 0))
    in_specs = [
        pl.BlockSpec((1, SEL_TILE, LANES), lambda i, t, c, r: (i, t, 0)),
        pl.BlockSpec((N_EXPERTS, 1, cap, d), lambda i, t, c, r: (0, i, 0, 0), pipeline_mode=pl.Buffered(1)),
        tile, vec,
    ]
    args = [cum, rounds, slott, ys, x1, gate]
    out_specs, out_shape = tile, jax.ShapeDtypeStruct((b, n, d), F32)
    if next_norm is not None:
        in_specs += [pl.BlockSpec((1, d), lambda i, t, c, r: (0, 0)), vec, vec]
        args += [next_norm[0].reshape(1, d), next_norm[1], next_norm[2]]
        out_specs, out_shape = [tile, tile], [out_shape, jax.ShapeDtypeStruct((b, n, d), BF16)]
    grid_spec = pltpu.PrefetchScalarGridSpec(
        num_scalar_prefetch=2, grid=(b, n_tiles), in_specs=in_specs, out_specs=out_specs)
    return pl.pallas_call(
        functools.partial(_scatter_kernel, cap=cap, win=win, n_tiles=n_tiles, next_norm=next_norm is not None),
        grid_spec=grid_spec,
        out_shape=out_shape,
        compiler_params=_params(("parallel", "arbitrary")),
        name="ec_scatter",
    )(*args)


def _mod_chunks(mod_rows):
    b = mod_rows.shape[0]
    return [mod_rows[:, i * D_MODEL:(i + 1) * D_MODEL].reshape(b, 1, D_MODEL) for i in range(N_MOD)]


def _token_mixer(h, zmix, att, lw, dft):
    b, n, d = h.shape
    y_conv, y_pool = _stencil(zmix, lw["conv_w"], lw["pool_w"], lw["pool_scale"])
    y_f = _fourier(zmix, *dft)
    branches = [y.reshape(b * n, MIX_W) for y in (y_conv, att, y_f, y_pool)]
    merged = _merge(h.reshape(b * n, d), branches, lw["w_in"], lw["w_branch"], lw["layer"])
    return merged.reshape(b, n, d)


def _route(h2, aff, affn):
    b, n, d = h2.shape
    cap = EC_CAPACITY * n // N_EXPERTS
    slot, slott = _select(aff, cap)
    per_tile = jnp.sum((slot >= 0).reshape(b, N_EXPERTS, n // SEL_TILE, SEL_TILE), axis=-1, dtype=I32)
    cum = jnp.concatenate([jnp.zeros((b, N_EXPERTS, 1), I32), jnp.cumsum(per_tile, axis=-1)], axis=-1)
    win = min(EC_WINDOW, cap)
    span = cum[..., 1:] - (cum[..., :-1] // ROW_ALIGN) * ROW_ALIGN
    rounds = jnp.max((span + win - 1) // win, axis=1).reshape(-1)
    cum = cum.reshape(-1)
    xs, gate = _gather(cum, rounds, slot, h2, affn, cap)
    tables = (cum, rounds, slott, cap)
    return tables, (xs.reshape(N_EXPERTS, b * cap, d), gate.reshape(N_EXPERTS, b * cap, LANES))


def _combine(tables, ys, x1, g2, next_norm=None):
    cum, rounds, slott, cap = tables
    b, _, d = x1.shape
    return _scatter(cum, rounds, slott, ys.reshape(N_EXPERTS, b, cap, d), x1, g2, cap, next_norm)


def kernel(x, c, ctx, c_ctx, w_ada, b_ada, norm1_w, norm2_w, w_in, conv_w, q_norm_w, k_norm_w,
           na_rpb, pool_w, pool_scale, w_branch, w_out, w_router, w_exp_gate, w_exp_up, w_exp_down):
    bsz, n, d = x.shape
    l_ctx = ctx.shape[1]
    depth = w_ada.shape[0]
    n_rows = n // GRID_W

    c16 = jnp.zeros((16, d), F32).at[:bsz].set(c).at[bsz].set(c_ctx)
    mod_all = _ada(c16, w_ada, b_ada)

    hd = np.arange(MIX_W) // HEAD_DIM
    bd = jnp.asarray(hd[:, None] == hd[None, :], dtype=BF16)
    dft_lat = _dft_mats(n, 1.0, n // 2) + _dft_mats(FOURIER_GC, (n * FOURIER_GC) ** -0.5)
    dft_ctx = _dft_mats(l_ctx, 1.0, l_ctx // 2) + _dft_mats(FOURIER_GC, (l_ctx * FOURIER_GC) ** -0.5)

    mods = [_mod_chunks(mod_all[l, :bsz]) for l in range(depth)]
    h = _prenorm(x, norm1_w[0], mods[0][0], mods[0][1])
    for l in range(depth):
        last = l == depth - 1
        lw = {
            "w_in": w_in, "conv_w": conv_w[l], "pool_w": pool_w[l],
            "pool_scale": pool_scale[l], "w_branch": w_branch, "layer": l,
        }
        w_out_bf = w_out[l].astype(BF16)
        wr_pad = jnp.zeros((d, LANES), F32).at[:, :N_EXPERTS].set(w_router[l])
        wr_hi = wr_pad.astype(BF16)
        wr_cat = jnp.concatenate([wr_hi, (wr_pad - wr_hi.astype(F32)).astype(BF16)], axis=1)
        q_w = jnp.tile(q_norm_w[l], N_HEADS).reshape(1, MIX_W)
        k_w = jnp.tile(k_norm_w[l], N_HEADS).reshape(1, MIX_W)
        tb = _na_bias_table(na_rpb[l], n_rows)
        _, _, g1, sh2, sc2, g2 = mods[l]
        mc = _mod_chunks(jnp.broadcast_to(mod_all[l, bsz:bsz + 1], (bsz, N_MOD * d)))

        hc = _prenorm(ctx, norm1_w[l], mc[0], mc[1])
        zc_mix = _in_proj(hc.reshape(bsz * l_ctx, d), w_in, l, MIX_COLS).reshape(bsz, l_ctx, MIX_COLS)

        zmix = _in_proj(h.reshape(bsz * n, d), w_in, l, MIX_COLS).reshape(bsz, n, MIX_COLS)
        att = _neighbourhood_attention(zmix, zc_mix, q_w, k_w, bd, tb)
        merged = _token_mixer(h, zmix, att, lw, dft_lat)
        x1, h2, aff, affn = _out_proj(merged, w_out_bf, x, g1, norm2_w[l], sh2, sc2, wr_cat)
        tables, routed = _route(h2, aff, affn)
        expert_w = (w_exp_gate, w_exp_up, w_exp_down, l)
        if last:
            x = _combine(tables, _experts(*routed, *expert_w), x1, g2)
        else:
            att_c = _context_attention(zc_mix, q_w, k_w, bd)
            merged_c = _token_mixer(hc, zc_mix, att_c, lw, dft_ctx)
            c1, hc2, aff_c, affn_c = _out_proj(merged_c, w_out_bf, ctx, mc[2], norm2_w[l], mc[3], mc[4], wr_cat)
            tables_c, routed_c = _route(hc2, aff_c, affn_c)
            x, h = _combine(tables, _experts(*routed, *expert_w), x1, g2,
                            (norm1_w[l + 1], mods[l + 1][0], mods[l + 1][1]))
            ctx = _combine(tables_c, _experts(*routed_c, *expert_w), c1, mc[5])
    return x
```
